```python
import jax, jax.numpy as jnp
from jax import lax
import numpy as np

D_MODEL = 4096
BATCH = 16
SEQ = 2048
DEPTH = 1

MIX_WIDTH = D_MODEL
HEAD_DIM = 128
CONV_WIDTH = MIX_WIDTH // 2
GMLP_WIDTH = MIX_WIDTH - CONV_WIDTH
CONV_GROUPS = CONV_WIDTH // HEAD_DIM
GMLP_HEADS = GMLP_WIDTH // HEAD_DIM
CONV_K = 3
CHUNK = 128
D_FF = 4 * D_MODEL
IN_PROJ_WIDTH = 3 * CONV_WIDTH + 2 * GMLP_WIDTH
EPS = 1e-5

kernel_name = "hybrid_shortconv_gmlp_block"


def rmsnorm(x, g):
    xf = x.astype(jnp.float32)
    inv = lax.rsqrt(jnp.mean(xf * xf, axis=-1, keepdims=True) + EPS)
    return (xf * inv * g.astype(jnp.float32)).astype(x.dtype)


def short_conv_mixer(b_gate, c_gate, h_in, conv_w):
    h = c_gate * h_in
    S = h.shape[1]
    hp = jnp.pad(h, ((0, 0), (CONV_K - 1, 0), (0, 0)))
    y = conv_w[0] * hp[:, 0:S]
    for k in range(1, CONV_K):
        y = y + conv_w[k] * hp[:, k:k + S]
    return b_gate * y


def chunked_spatial_gating(u, v, spatial_w, spatial_b):
    bsz, S, _ = v.shape
    n_chunks = S // CHUNK
    causal = jnp.tril(jnp.ones((CHUNK, CHUNK), dtype=bool))
    w = jnp.where(causal[None], spatial_w, jnp.zeros((), spatial_w.dtype))
    vc = v.reshape(bsz, n_chunks, CHUNK, GMLP_HEADS, HEAD_DIM)
    s = jnp.einsum('hts,bcshd->bcthd', w, vc) + spatial_b.T[None, None, :, :, None]
    return u * s.reshape(bsz, S, GMLP_WIDTH)


def _fwd_setup_inputs(seed: int = 0) -> dict:
    key = jax.random.key(seed)
    ks = jax.random.split(key, 13)
    f32 = jnp.float32
    x = jax.random.normal(ks[0], (BATCH, SEQ, D_MODEL), f32)
    mix_norm_g = 1.0 + 0.02 * jax.random.normal(ks[1], (DEPTH, D_MODEL), f32)
    w_in = jax.random.normal(ks[2], (DEPTH, D_MODEL, IN_PROJ_WIDTH), f32) * D_MODEL ** -0.5
    conv_w = jax.random.normal(ks[3], (DEPTH, CONV_K, CONV_WIDTH), f32) * CONV_K ** -0.5
    spatial_w = jax.random.normal(ks[4], (DEPTH, GMLP_HEADS, CHUNK, CHUNK), f32) * (0.5 * CHUNK ** -0.5)
    spatial_b = 1.0 + 0.02 * jax.random.normal(ks[5], (DEPTH, GMLP_HEADS, CHUNK), f32)
    conv_out_norm_g = 1.0 + 0.02 * jax.random.normal(ks[6], (DEPTH, CONV_WIDTH), f32)
    gmlp_out_norm_g = 1.0 + 0.02 * jax.random.normal(ks[7], (DEPTH, GMLP_WIDTH), f32)
    w_out = jax.random.normal(ks[8], (DEPTH, MIX_WIDTH, D_MODEL), f32) * MIX_WIDTH ** -0.5
    mlp_norm_g = 1.0 + 0.02 * jax.random.normal(ks[9], (DEPTH, D_MODEL), f32)
    w_up = jax.random.normal(ks[10], (DEPTH, D_MODEL, D_FF), f32) * D_MODEL ** -0.5
    w_down = jax.random.normal(ks[11], (DEPTH, D_FF, D_MODEL), f32) * D_FF ** -0.5
    final_norm_g = 1.0 + 0.02 * jax.random.normal(ks[12], (D_MODEL,), f32)
    return {"x": x, "mix_norm_g": mix_norm_g, "w_in": w_in, "conv_w": conv_w,
            "spatial_w": spatial_w, "spatial_b": spatial_b,
            "conv_out_norm_g": conv_out_norm_g, "gmlp_out_norm_g": gmlp_out_norm_g,
            "w_out": w_out, "mlp_norm_g": mlp_norm_g, "w_up": w_up, "w_down": w_down,
            "final_norm_g": final_norm_g}


def _fwd_reference(x, mix_norm_g, w_in, conv_w, spatial_w, spatial_b, conv_out_norm_g,
              gmlp_out_norm_g, w_out, mlp_norm_g, w_up, w_down, final_norm_g):
    split_points = [CONV_WIDTH, 2 * CONV_WIDTH, 3 * CONV_WIDTH, 3 * CONV_WIDTH + GMLP_WIDTH]
    h = x
    for l in range(DEPTH):
        xn = rmsnorm(h, mix_norm_g[l])
        proj = jnp.einsum('bsd,de->bse', xn, w_in[l])
        b_gate, c_gate, h_in, u, v = jnp.split(proj, split_points, axis=-1)
        y_a = short_conv_mixer(b_gate, c_gate, h_in, conv_w[l])
        y_b = chunked_spatial_gating(jax.nn.gelu(u), jax.nn.gelu(v),
                                     spatial_w[l], spatial_b[l])
        y = jnp.concatenate([rmsnorm(y_a, conv_out_norm_g[l]),
                             rmsnorm(y_b, gmlp_out_norm_g[l])], axis=-1)
        h = h + jnp.einsum('bse,ed->bsd', y, w_out[l])
        xn = rmsnorm(h, mlp_norm_g[l])
        a = jnp.square(jax.nn.relu(jnp.einsum('bsd,df->bsf', xn, w_up[l])))
        h = h + jnp.einsum('bsf,fd->bsd', a, w_down[l])
    return rmsnorm(h, final_norm_g)


import jax as _jax
import jax.numpy as _jnp

TWIN_FORMAT = 'train_step'
FWD_PARAMS = ['x', 'mix_norm_g', 'w_in', 'conv_w', 'spatial_w', 'spatial_b', 'conv_out_norm_g', 'gmlp_out_norm_g', 'w_out', 'mlp_norm_g', 'w_up', 'w_down', 'final_norm_g']
TWIN_WEIGHTS = ['mix_norm_g', 'w_in', 'conv_w', 'spatial_w', 'spatial_b', 'conv_out_norm_g', 'gmlp_out_norm_g', 'w_out', 'mlp_norm_g', 'w_up', 'w_down', 'final_norm_g']
TWIN_DIFF_INPUT = 'x'
TWIN_INPUTS = ['x', 'mix_norm_g', 'w_in', 'conv_w', 'spatial_w', 'spatial_b', 'conv_out_norm_g', 'gmlp_out_norm_g', 'w_out', 'mlp_norm_g', 'w_up', 'w_down', 'final_norm_g', 'loss_target', 'm_mix_norm_g', 'm_w_in', 'm_conv_w', 'm_spatial_w', 'm_spatial_b', 'm_conv_out_norm_g', 'm_gmlp_out_norm_g', 'm_w_out', 'm_mlp_norm_g', 'm_w_up', 'm_w_down', 'm_final_norm_g', 'v_mix_norm_g', 'v_w_in', 'v_conv_w', 'v_spatial_w', 'v_spatial_b', 'v_conv_out_norm_g', 'v_gmlp_out_norm_g', 'v_w_out', 'v_mlp_norm_g', 'v_w_up', 'v_w_down', 'v_final_norm_g']
TWIN_OUTPUTS = ['loss', 'grad_x', 'grad_mix_norm_g', 'grad_w_in', 'grad_conv_w', 'grad_spatial_w', 'grad_spatial_b', 'grad_conv_out_norm_g', 'grad_gmlp_out_norm_g', 'grad_w_out', 'grad_mlp_norm_g', 'grad_w_up', 'grad_w_down', 'grad_final_norm_g', 'delta_mix_norm_g', 'delta_w_in', 'delta_conv_w', 'delta_spatial_w', 'delta_spatial_b', 'delta_conv_out_norm_g', 'delta_gmlp_out_norm_g', 'delta_w_out', 'delta_mlp_norm_g', 'delta_w_up', 'delta_w_down', 'delta_final_norm_g', 'new_m_mix_norm_g', 'new_m_w_in', 'new_m_conv_w', 'new_m_spatial_w', 'new_m_spatial_b', 'new_m_conv_out_norm_g', 'new_m_gmlp_out_norm_g', 'new_m_w_out', 'new_m_mlp_norm_g', 'new_m_w_up', 'new_m_w_down', 'new_m_final_norm_g', 'new_v_mix_norm_g', 'new_v_w_in', 'new_v_conv_w', 'new_v_spatial_w', 'new_v_spatial_b', 'new_v_conv_out_norm_g', 'new_v_gmlp_out_norm_g', 'new_v_w_out', 'new_v_mlp_norm_g', 'new_v_w_up', 'new_v_w_down', 'new_v_final_norm_g']
TWIN_LEAF_KINDS = {'loss': 'loss', 'grad_x': 'grad_x', 'grad_mix_norm_g': 'grad_w', 'grad_w_in': 'grad_w', 'grad_conv_w': 'grad_w', 'grad_spatial_w': 'grad_w', 'grad_spatial_b': 'grad_w', 'grad_conv_out_norm_g': 'grad_w', 'grad_gmlp_out_norm_g': 'grad_w', 'grad_w_out': 'grad_w', 'grad_mlp_norm_g': 'grad_w', 'grad_w_up': 'grad_w', 'grad_w_down': 'grad_w', 'grad_final_norm_g': 'grad_w', 'delta_mix_norm_g': 'delta_w', 'delta_w_in': 'delta_w', 'delta_conv_w': 'delta_w', 'delta_spatial_w': 'delta_w', 'delta_spatial_b': 'delta_w', 'delta_conv_out_norm_g': 'delta_w', 'delta_gmlp_out_norm_g': 'delta_w', 'delta_w_out': 'delta_w', 'delta_mlp_norm_g': 'delta_w', 'delta_w_up': 'delta_w', 'delta_w_down': 'delta_w', 'delta_final_norm_g': 'delta_w', 'new_m_mix_norm_g': 'new_m', 'new_m_w_in': 'new_m', 'new_m_conv_w': 'new_m', 'new_m_spatial_w': 'new_m', 'new_m_spatial_b': 'new_m', 'new_m_conv_out_norm_g': 'new_m', 'new_m_gmlp_out_norm_g': 'new_m', 'new_m_w_out': 'new_m', 'new_m_mlp_norm_g': 'new_m', 'new_m_w_up': 'new_m', 'new_m_w_down': 'new_m', 'new_m_final_norm_g': 'new_m', 'new_v_mix_norm_g': 'new_v', 'new_v_w_in': 'new_v', 'new_v_conv_w': 'new_v', 'new_v_spatial_w': 'new_v', 'new_v_spatial_b': 'new_v', 'new_v_conv_out_norm_g': 'new_v', 'new_v_gmlp_out_norm_g': 'new_v', 'new_v_w_out': 'new_v', 'new_v_mlp_norm_g': 'new_v', 'new_v_w_up': 'new_v', 'new_v_w_down': 'new_v', 'new_v_final_norm_g': 'new_v'}


def _forward(args):
    return _fwd_reference(*[args[k] for k in FWD_PARAMS])


def _output_shape():
    def fwd():
        inp = _fwd_setup_inputs(0)
        return _fwd_reference(*[inp[k] for k in FWD_PARAMS])
    out = _jax.eval_shape(fwd)
    return out.shape, out.dtype

N_MICROBATCH = 1
ADAM_LR = 0.001
ADAM_B1 = 0.9
ADAM_B2 = 0.999
ADAM_EPS = 1e-08
ADAM_WD = 0.01
ADAM_STEP = 10
PER_EXAMPLE_BATCH_AXIS = {'x': 0, 'loss_target': 0}
SHARED_INPUTS = []
_WEIGHT_DTYPES = {'mix_norm_g': _jnp.float32, 'w_in': _jnp.float32, 'conv_w': _jnp.float32, 'spatial_w': _jnp.float32, 'spatial_b': _jnp.float32, 'conv_out_norm_g': _jnp.float32, 'gmlp_out_norm_g': _jnp.float32, 'w_out': _jnp.float32, 'mlp_norm_g': _jnp.float32, 'w_up': _jnp.float32, 'w_down': _jnp.float32, 'final_norm_g': _jnp.float32}
MOMENT_SCALE = {'mix_norm_g': 4.893857e-02, 'w_in': 3.058730e-02, 'conv_w': 3.384040e-02, 'spatial_w': 1.491064e-02, 'spatial_b': 3.216626e-02, 'conv_out_norm_g': 3.483226e-02, 'gmlp_out_norm_g': 3.826783e-02, 'w_out': 3.595456e-02, 'mlp_norm_g': 3.461095e-02, 'w_up': 1.712280e-02, 'w_down': 3.511485e-02, 'final_norm_g': 8.085360e+00}


def _to_microbatches(a, axis):
    t = _jnp.moveaxis(a, axis, 0)
    t = t.reshape((N_MICROBATCH, t.shape[0] // N_MICROBATCH) + t.shape[1:])
    return _jnp.moveaxis(t, 1, axis + 1)


def setup_inputs(seed: int = 0) -> dict:
    inp = _fwd_setup_inputs(seed)
    key = _jax.random.fold_in(_jax.random.key(seed), 7919)
    shape, _ = _output_shape()
    out = dict(inp)
    out["loss_target"] = _jax.random.normal(_jax.random.fold_in(key, 0), shape, _jnp.float32)
    for i, name in enumerate(TWIN_WEIGHTS):
        w = inp[name].astype(_jnp.float32)
        if MOMENT_SCALE is None:
            s = _jnp.sqrt(_jnp.mean(_jnp.square(w)) + 1e-30)
        else:
            s = MOMENT_SCALE[name]
        km, kv = _jax.random.split(_jax.random.fold_in(key, i + 1))
        out[name] = w
        out["m_" + name] = s * _jax.random.normal(km, w.shape, _jnp.float32)
        out["v_" + name] = (s * s) * _jax.random.uniform(kv, w.shape, _jnp.float32, 0.5, 1.5)
    if N_MICROBATCH > 1:
        for name, axis in PER_EXAMPLE_BATCH_AXIS.items():
            out[name] = _to_microbatches(out[name], axis)
    return {'x': out['x'], 'mix_norm_g': out['mix_norm_g'], 'w_in': out['w_in'], 'conv_w': out['conv_w'], 'spatial_w': out['spatial_w'], 'spatial_b': out['spatial_b'], 'conv_out_norm_g': out['conv_out_norm_g'], 'gmlp_out_norm_g': out['gmlp_out_norm_g'], 'w_out': out['w_out'], 'mlp_norm_g': out['mlp_norm_g'], 'w_up': out['w_up'], 'w_down': out['w_down'], 'final_norm_g': out['final_norm_g'], 'loss_target': out['loss_target'], 'm_mix_norm_g': out['m_mix_norm_g'], 'm_w_in': out['m_w_in'], 'm_conv_w': out['m_conv_w'], 'm_spatial_w': out['m_spatial_w'], 'm_spatial_b': out['m_spatial_b'], 'm_conv_out_norm_g': out['m_conv_out_norm_g'], 'm_gmlp_out_norm_g': out['m_gmlp_out_norm_g'], 'm_w_out': out['m_w_out'], 'm_mlp_norm_g': out['m_mlp_norm_g'], 'm_w_up': out['m_w_up'], 'm_w_down': out['m_w_down'], 'm_final_norm_g': out['m_final_norm_g'], 'v_mix_norm_g': out['v_mix_norm_g'], 'v_w_in': out['v_w_in'], 'v_conv_w': out['v_conv_w'], 'v_spatial_w': out['v_spatial_w'], 'v_spatial_b': out['v_spatial_b'], 'v_conv_out_norm_g': out['v_conv_out_norm_g'], 'v_gmlp_out_norm_g': out['v_gmlp_out_norm_g'], 'v_w_out': out['v_w_out'], 'v_mlp_norm_g': out['v_mlp_norm_g'], 'v_w_up': out['v_w_up'], 'v_w_down': out['v_w_down'], 'v_final_norm_g': out['v_final_norm_g']}


def _loss(weights, diff, rest, loss_target):
    with _jax.named_scope("forward"):
        args = {**rest, TWIN_DIFF_INPUT: diff, **{k: w.astype(_WEIGHT_DTYPES[k]) for k, w in weights.items()}}
        y = _forward(args)
    with _jax.named_scope("loss_head"):
        err = _jnp.square(y.astype(_jnp.float32) - loss_target)
        return 0.5 * _jnp.sum(_jnp.mean(err, axis=-1)) if err.ndim else 0.5 * err


def _adamw(w, g, m, v):
    m = ADAM_B1 * m + (1.0 - ADAM_B1) * g
    v = ADAM_B2 * v + (1.0 - ADAM_B2) * _jnp.square(g)
    m_hat = m / (1.0 - ADAM_B1 ** ADAM_STEP)
    v_hat = v / (1.0 - ADAM_B2 ** ADAM_STEP)
    delta = -ADAM_LR * (m_hat / (_jnp.sqrt(v_hat) + ADAM_EPS) + ADAM_WD * w)
    return delta, m, v


def reference(x, mix_norm_g, w_in, conv_w, spatial_w, spatial_b, conv_out_norm_g, gmlp_out_norm_g, w_out, mlp_norm_g, w_up, w_down, final_norm_g, loss_target, m_mix_norm_g, m_w_in, m_conv_w, m_spatial_w, m_spatial_b, m_conv_out_norm_g, m_gmlp_out_norm_g, m_w_out, m_mlp_norm_g, m_w_up, m_w_down, m_final_norm_g, v_mix_norm_g, v_w_in, v_conv_w, v_spatial_w, v_spatial_b, v_conv_out_norm_g, v_gmlp_out_norm_g, v_w_out, v_mlp_norm_g, v_w_up, v_w_down, v_final_norm_g):
    given = dict(x=x, mix_norm_g=mix_norm_g, w_in=w_in, conv_w=conv_w, spatial_w=spatial_w, spatial_b=spatial_b, conv_out_norm_g=conv_out_norm_g, gmlp_out_norm_g=gmlp_out_norm_g, w_out=w_out, mlp_norm_g=mlp_norm_g, w_up=w_up, w_down=w_down, final_norm_g=final_norm_g, loss_target=loss_target, m_mix_norm_g=m_mix_norm_g, m_w_in=m_w_in, m_conv_w=m_conv_w, m_spatial_w=m_spatial_w, m_spatial_b=m_spatial_b, m_conv_out_norm_g=m_conv_out_norm_g, m_gmlp_out_norm_g=m_gmlp_out_norm_g, m_w_out=m_w_out, m_mlp_norm_g=m_mlp_norm_g, m_w_up=m_w_up, m_w_down=m_w_down, m_final_norm_g=m_final_norm_g, v_mix_norm_g=v_mix_norm_g, v_w_in=v_w_in, v_conv_w=v_conv_w, v_spatial_w=v_spatial_w, v_spatial_b=v_spatial_b, v_conv_out_norm_g=v_conv_out_norm_g, v_gmlp_out_norm_g=v_gmlp_out_norm_g, v_w_out=v_w_out, v_mlp_norm_g=v_mlp_norm_g, v_w_up=v_w_up, v_w_down=v_w_down, v_final_norm_g=v_final_norm_g)
    weights = {n: given[n] for n in TWIN_WEIGHTS}
    shared = {n: given[n] for n in SHARED_INPUTS}
    per_example = {n: given[n] for n in ['x']}
    grad_fn = _jax.value_and_grad(_loss, argnums=(0, 1))

    def one_microbatch(ex, loss_target):
        ex = dict(ex)
        diff = ex.pop(TWIN_DIFF_INPUT)
        return grad_fn(weights, diff, {**shared, **ex}, loss_target)

    if N_MICROBATCH == 1:
        loss, (grad_w, grad_x) = one_microbatch(per_example, given["loss_target"])
    else:
        def body(carry, xs):
            loss_sum, grad_sum = carry
            l_k, (gw_k, gx_k) = one_microbatch(xs[0], xs[1])
            with _jax.named_scope("update"):
                return (loss_sum + l_k, _jax.tree.map(_jnp.add, grad_sum, gw_k)), gx_k

        init = (_jnp.zeros((), _jnp.float32), _jax.tree.map(_jnp.zeros_like, weights))
        (loss, grad_w), grad_x = _jax.lax.scan(body, init, (per_example, given["loss_target"]))
    with _jax.named_scope("update"):
        delta_w, new_m, new_v = {}, {}, {}
        for n in TWIN_WEIGHTS:
            delta_w[n], new_m[n], new_v[n] = _adamw(weights[n], grad_w[n], given["m_" + n], given["v_" + n])
    return (loss, grad_x, *[grad_w[n] for n in TWIN_WEIGHTS], *[delta_w[n] for n in TWIN_WEIGHTS],
            *[new_m[n] for n in TWIN_WEIGHTS], *[new_v[n] for n in TWIN_WEIGHTS])
```

```python
import functools
import math

import jax
import jax.numpy as jnp
from jax import lax
from jax.experimental import pallas as pl
from jax.experimental.pallas import tpu as pltpu

F32 = jnp.float32
BF16 = jnp.bfloat16
MESH = pl.DeviceIdType.MESH

NORM_EPS = 1e-5
HEAD_DIM = 128
CHUNK = 128
CONV_K = 3
N_CHIPS = 4
N_DEV = 8

ADAM_LR = 0.001
ADAM_B1 = 0.9
ADAM_B2 = 0.999
ADAM_EPS = 1e-08
ADAM_WD = 0.01
ADAM_STEP = 10
ADAM_C1 = 1.0 - ADAM_B1 ** ADAM_STEP
ADAM_C2 = 1.0 - ADAM_B2 ** ADAM_STEP

GELU_K = math.sqrt(2.0 / math.pi)
GELU_A = 0.044715

VMEM_LIMIT_V7X = 56 * 1024 * 1024
SUBLANES = 8
LANES = 128


def _tile(dim, target, mult=LANES):
    if dim <= target:
        return dim
    t = (target // mult) * mult
    while t > mult and dim % t:
        t -= mult
    assert dim % t == 0, (dim, target, mult)
    return t


def _params(sem=None):
    return pltpu.CompilerParams(dimension_semantics=sem, vmem_limit_bytes=VMEM_LIMIT_V7X)


def _matmul(a, b, *, mode, name, tm, tn, tk, out_dtypes, epilogue=None, extras=(), b_shard=None, out_shard=False):
    if mode == "tn":
        K, M = a.shape
    else:
        M, K = a.shape
    if b_shard == "n":
        S, Kb, Ns = b.shape
        N = S * Ns
    elif b_shard == "k":
        S, N, Ks = b.shape
        Kb = S * Ks
    elif mode == "nt":
        N, Kb = b.shape
    else:
        Kb, N = b.shape
    assert Kb == K, (name, a.shape, b.shape)
    tm, tn, tk = _tile(M, tm), _tile(N, tn), _tile(K, tk)
    if b_shard == "n" or out_shard:
        n_per = N // N_CHIPS
        tn = _tile(n_per, tn)
        njs = n_per // tn
    if b_shard == "k":
        tk = _tile(K // N_CHIPS, tk)
        nks = (K // N_CHIPS) // tk
    gm, gn, gk = M // tm, N // tn, K // tk

    if mode == "tn":
        a_spec = pl.BlockSpec((tk, tm), lambda i, j, k: (k, i))
        dims = (((0,), (0,)), ((), ()))
    else:
        a_spec = pl.BlockSpec((tm, tk), lambda i, j, k: (i, k))
        dims = (((1,), (1,)), ((), ())) if mode == "nt" else (((1,), (0,)), ((), ()))
    if b_shard == "n":
        b_spec = pl.BlockSpec((None, tk, tn), lambda i, j, k: (j // njs, k, j % njs))
    elif b_shard == "k":
        b_spec = pl.BlockSpec((None, tn, tk), lambda i, j, k: (k // nks, j, k % nks))
    elif mode == "nt":
        b_spec = pl.BlockSpec((tn, tk), lambda i, j, k: (j, k))
    else:
        b_spec = pl.BlockSpec((tk, tn), lambda i, j, k: (k, j))
    mn_spec = pl.BlockSpec((tm, tn), lambda i, j, k: (i, j))
    if out_shard:
        out_spec = pl.BlockSpec((None, tm, tn), lambda i, j, k: (j // njs, i, j % njs))
        out_shape = [jax.ShapeDtypeStruct((N_CHIPS, M, N // N_CHIPS), dt) for dt in out_dtypes]
    else:
        out_spec = mn_spec
        out_shape = [jax.ShapeDtypeStruct((M, N), dt) for dt in out_dtypes]
    n_extra, n_out = len(extras), len(out_dtypes)

    def body(*refs):
        a_ref, b_ref = refs[0], refs[1]
        extra_refs = refs[2:2 + n_extra]
        out_refs = refs[2 + n_extra:2 + n_extra + n_out]
        acc_ref = refs[-1]
        k = pl.program_id(2)

        @pl.when(k == 0)
        def _():
            acc_ref[...] = jnp.zeros_like(acc_ref)

        acc_ref[...] += lax.dot_general(a_ref[...], b_ref[...], dims, preferred_element_type=F32)

        @pl.when(k == gk - 1)
        def _():
            acc = acc_ref[...]
            if epilogue is None:
                for o in out_refs:
                    o[...] = acc.astype(o.dtype)
            else:
                epilogue(acc, extra_refs, out_refs)

    outs = pl.pallas_call(
        body,
        name=name,
        grid=(gm, gn, gk),
        in_specs=[a_spec, b_spec] + [mn_spec] * n_extra,
        out_specs=[out_spec] * n_out,
        out_shape=out_shape,
        scratch_shapes=[pltpu.VMEM((tm, tn), F32)],
        compiler_params=_params(("parallel", "parallel", "arbitrary")),
    )(a, b, *extras)
    return outs[0] if n_out == 1 else outs


def _ep_residual(acc, extra_refs, out_refs):
    out_refs[0][...] = extra_refs[0][...] + acc


def _ep_relu2(acc, extra_refs, out_refs):
    r = jnp.maximum(acc, 0.0)
    out_refs[0][...] = r.astype(BF16)
    out_refs[1][...] = (r * r).astype(BF16)


def _ep_relu2_bwd(acc, extra_refs, out_refs):
    out_refs[0][...] = (acc * (2.0 * extra_refs[0][...].astype(F32))).astype(BF16)


def _row_inv(x):
    return lax.rsqrt(jnp.mean(x * x, axis=-1, keepdims=True) + NORM_EPS)


def _rmsnorm_fwd(x, g, name):
    T, D = x.shape
    tt = _tile(T, 256, SUBLANES)

    def body(x_ref, g_ref, o_ref):
        xv = x_ref[...]
        o_ref[...] = (xv * _row_inv(xv) * g_ref[...]).astype(BF16)

    return pl.pallas_call(
        body, name=name, grid=(T // tt,),
        in_specs=[pl.BlockSpec((tt, D), lambda i: (i, 0)), pl.BlockSpec((1, D), lambda i: (0, 0))],
        out_specs=pl.BlockSpec((tt, D), lambda i: (i, 0)),
        out_shape=jax.ShapeDtypeStruct((T, D), BF16),
        compiler_params=_params(("parallel",)),
    )(x, g)


def _rmsnorm_bwd(dxn, h, g, dres, name):
    T, D = h.shape
    tt = _tile(T, 128, SUBLANES)

    def body(dxn_ref, h_ref, g_ref, dres_ref, dh_ref, dhb_ref, dg_ref):
        @pl.when(pl.program_id(0) == 0)
        def _():
            dg_ref[...] = jnp.zeros_like(dg_ref)

        hv = h_ref[...]
        inv = _row_inv(hv)
        n = hv * inv
        d = dxn_ref[...]
        dg_ref[...] += jnp.sum(d * n, axis=0, keepdims=True)
        dn = d * g_ref[...]
        dh = dres_ref[...] + inv * (dn - n * jnp.mean(dn * n, axis=-1, keepdims=True))
        dh_ref[...] = dh
        dhb_ref[...] = dh.astype(BF16)

    row = pl.BlockSpec((tt, D), lambda i: (i, 0))
    vec = pl.BlockSpec((1, D), lambda i: (0, 0))
    return pl.pallas_call(
        body, name=name, grid=(T // tt,),
        in_specs=[row, row, vec, row],
        out_specs=[row, row, vec],
        out_shape=[jax.ShapeDtypeStruct((T, D), F32), jax.ShapeDtypeStruct((T, D), BF16),
                   jax.ShapeDtypeStruct((1, D), F32)],
        compiler_params=_params(("arbitrary",)),
    )(dxn, h, g, dres)


def _loss_and_final_norm_bwd(h, tgt, g, name):
    T, D = h.shape
    tt = _tile(T, 128, SUBLANES)

    def body(h_ref, t_ref, g_ref, dh_ref, dhb_ref, dg_ref, loss_ref):
        @pl.when(pl.program_id(0) == 0)
        def _():
            dg_ref[...] = jnp.zeros_like(dg_ref)
            loss_ref[...] = jnp.zeros_like(loss_ref)

        hv = h_ref[...]
        gv = g_ref[...]
        inv = _row_inv(hv)
        n = hv * inv
        err = n * gv - t_ref[...]
        loss_ref[...] += 0.5 * jnp.sum(jnp.mean(err * err, axis=-1, keepdims=True))
        dy = err * (1.0 / D)
        dg_ref[...] += jnp.sum(dy * n, axis=0, keepdims=True)
        dn = dy * gv
        dh = inv * (dn - n * jnp.mean(dn * n, axis=-1, keepdims=True))
        dh_ref[...] = dh
        dhb_ref[...] = dh.astype(BF16)

    row = pl.BlockSpec((tt, D), lambda i: (i, 0))
    vec = pl.BlockSpec((1, D), lambda i: (0, 0))
    one = pl.BlockSpec((1, LANES), lambda i: (0, 0))
    return pl.pallas_call(
        body, name=name, grid=(T // tt,),
        in_specs=[row, row, vec],
        out_specs=[row, row, vec, one],
        out_shape=[jax.ShapeDtypeStruct((T, D), F32), jax.ShapeDtypeStruct((T, D), BF16),
                   jax.ShapeDtypeStruct((1, D), F32), jax.ShapeDtypeStruct((1, LANES), F32)],
        compiler_params=_params(("arbitrary",)),
    )(h, tgt, g)


def _gelu(x):
    th = jnp.tanh(GELU_K * (x + GELU_A * (x * x * x)))
    return 0.5 * x * (1.0 + th), th


def _gelu_grad(x, th):
    return 0.5 * (1.0 + th) + 0.5 * x * (1.0 - th * th) * (GELU_K * (1.0 + 3.0 * GELU_A * (x * x)))


def _shift_rows(cur, prev_rows, k):
    rolled = pltpu.roll(cur, k, 0)
    row = lax.broadcasted_iota(jnp.int32, cur.shape, 0)
    out = rolled
    for r in range(k):
        out = jnp.where(row == r, prev_rows[SUBLANES - k + r:SUBLANES - k + r + 1, :], out)
    return out


def _unshift_rows(cur, next_rows, k):
    n = cur.shape[0]
    rolled = pltpu.roll(cur, n - k, 0)
    row = lax.broadcasted_iota(jnp.int32, cur.shape, 0)
    out = rolled
    for r in range(k):
        out = jnp.where(row == n - k + r, next_rows[r:r + 1, :], out)
    return out


def _mixer_specs(W, blk, halo):
    cols = [pl.BlockSpec((CHUNK, W), functools.partial(lambda i, col: (blk(i), col), col=col)) for col in range(5)]
    halos = [pl.BlockSpec((SUBLANES, W), functools.partial(lambda i, col: (halo(i), col), col=col)) for col in (1, 2)]
    return cols, halos


def _mixers_fwd(proj, conv_w, wm, bias_e, g_a, g_b, seq_len, name):
    T, W5 = proj.shape
    W = W5 // 5
    H = W // HEAD_DIM
    per_seq = seq_len // CHUNK
    rb = CHUNK // SUBLANES
    cols, halos = _mixer_specs(W, lambda i: i, lambda i: jnp.maximum(i * rb - 1, 0))

    def body(b_ref, c_ref, hin_ref, u_ref, v_ref, ch_ref, hh_ref, cw_ref, wm_ref, be_ref, ga_ref, gb_ref, y_ref, s_ref):
        first = (pl.program_id(0) % per_seq) == 0
        hc = c_ref[...] * hin_ref[...]
        hc_prev = jnp.where(first, 0.0, ch_ref[...] * hh_ref[...])
        cw = cw_ref[...]
        ya = b_ref[...] * (cw[0:1, :] * _shift_rows(hc, hc_prev, 2) + cw[1:2, :] * _shift_rows(hc, hc_prev, 1)
                           + cw[2:3, :] * hc)
        y_ref[:, 0:W] = (ya * _row_inv(ya) * ga_ref[...]).astype(BF16)
        gu, _ = _gelu(u_ref[...])
        gv, _ = _gelu(v_ref[...])
        gvb = gv.astype(BF16)
        for hd in range(H):
            sl = slice(hd * HEAD_DIM, (hd + 1) * HEAD_DIM)
            s_ref[:, sl] = jnp.dot(wm_ref[hd], gvb[:, sl], preferred_element_type=F32)
        yb = gu * (s_ref[...] + be_ref[...])
        y_ref[:, W:2 * W] = (yb * _row_inv(yb) * gb_ref[...]).astype(BF16)

    full = lambda shape: pl.BlockSpec(shape, lambda i: (0,) * len(shape))
    return pl.pallas_call(
        body, name=name, grid=(T // CHUNK,),
        in_specs=cols + halos + [full((CONV_K, W)), full((H, CHUNK, CHUNK)), full((CHUNK, W)), full((1, W)), full((1, W))],
        out_specs=pl.BlockSpec((CHUNK, 2 * W), lambda i: (i, 0)),
        out_shape=jax.ShapeDtypeStruct((T, 2 * W), BF16),
        scratch_shapes=[pltpu.VMEM((CHUNK, W), F32)],
        compiler_params=_params(("parallel",)),
    )(proj, proj, proj, proj, proj, proj, proj, conv_w, wm, bias_e, g_a, g_b)


def _mixers_bwd(dy, proj, conv_w, wm, wmt, bias_e, g_a, g_b, head_onehot, seq_len, name):
    T, W5 = proj.shape
    W = W5 // 5
    H = W // HEAD_DIM
    nb = T // CHUNK
    per_seq = seq_len // CHUNK
    rb = CHUNK // SUBLANES
    blk = lambda i: nb - 1 - i
    cols, halos = _mixer_specs(W, blk, lambda i: jnp.maximum(blk(i) * rb - 1, 0))

    def body(dy_ref, b_ref, c_ref, hin_ref, u_ref, v_ref, ch_ref, hh_ref, cw_ref, wm_ref, wmt_ref, be_ref, ga_ref,
             gb_ref, oh_ref, dp_ref, dcw_ref, dga_ref, dgb_ref, dws_ref, dbt_ref, carry_ref, s_ref, dgv_ref):
        i = pl.program_id(0)
        j = nb - 1 - i

        @pl.when(i == 0)
        def _():
            for r in (dcw_ref, dga_ref, dgb_ref, dws_ref, dbt_ref, carry_ref):
                r[...] = jnp.zeros_like(r)

        first = (j % per_seq) == 0
        last = (j % per_seq) == per_seq - 1
        b, c, hin = b_ref[...], c_ref[...], hin_ref[...]
        cw = cw_ref[...]
        hc = c * hin
        hc_prev = jnp.where(first, 0.0, ch_ref[...] * hh_ref[...])
        hc1 = _shift_rows(hc, hc_prev, 1)
        hc2 = _shift_rows(hc, hc_prev, 2)
        conv = cw[0:1, :] * hc2 + cw[1:2, :] * hc1 + cw[2:3, :] * hc
        ya = b * conv
        inv_a = _row_inv(ya)
        na = ya * inv_a
        do_a = dy_ref[:, 0:W]
        dga_ref[...] += jnp.sum(do_a * na, axis=0, keepdims=True)
        dna = do_a * ga_ref[...]
        dya = inv_a * (dna - na * jnp.mean(dna * na, axis=-1, keepdims=True))
        dp_ref[:, 0:W] = (dya * conv).astype(BF16)
        dconv = dya * b
        dcw_ref[0:1, :] += jnp.sum(dconv * hc2, axis=0, keepdims=True)
        dcw_ref[1:2, :] += jnp.sum(dconv * hc1, axis=0, keepdims=True)
        dcw_ref[2:3, :] += jnp.sum(dconv * hc, axis=0, keepdims=True)
        nxt = jnp.where(last, 0.0, carry_ref[...])
        dhc = cw[2:3, :] * dconv + cw[1:2, :] * _unshift_rows(dconv, nxt, 1) + cw[0:1, :] * _unshift_rows(dconv, nxt, 2)
        carry_ref[...] = dconv[0:SUBLANES, :]
        dp_ref[:, W:2 * W] = (dhc * hin).astype(BF16)
        dp_ref[:, 2 * W:3 * W] = (dhc * c).astype(BF16)
        u, v = u_ref[...], v_ref[...]
        gu, thu = _gelu(u)
        gv, thv = _gelu(v)
        gvb = gv.astype(BF16)
        for hd in range(H):
            sl = slice(hd * HEAD_DIM, (hd + 1) * HEAD_DIM)
            s_ref[:, sl] = jnp.dot(wm_ref[hd], gvb[:, sl], preferred_element_type=F32)
        s = s_ref[...] + be_ref[...]
        yb = gu * s
        inv_b = _row_inv(yb)
        nbv = yb * inv_b
        do_b = dy_ref[:, W:2 * W]
        dgb_ref[...] += jnp.sum(do_b * nbv, axis=0, keepdims=True)
        dnb = do_b * gb_ref[...]
        dyb = inv_b * (dnb - nbv * jnp.mean(dnb * nbv, axis=-1, keepdims=True))
        dp_ref[:, 3 * W:4 * W] = (dyb * s * _gelu_grad(u, thu)).astype(BF16)
        dsb = (dyb * gu).astype(BF16)
        dbt_ref[...] += jnp.dot(dsb, oh_ref[...], preferred_element_type=F32)
        for hd in range(H):
            sl = slice(hd * HEAD_DIM, (hd + 1) * HEAD_DIM)
            dws_ref[hd] += lax.dot_general(dsb[:, sl], gvb[:, sl], (((1,), (1,)), ((), ())), preferred_element_type=F32)
            dgv_ref[:, sl] = jnp.dot(wmt_ref[hd], dsb[:, sl], preferred_element_type=F32)
        dp_ref[:, 4 * W:5 * W] = (dgv_ref[...] * _gelu_grad(v, thv)).astype(BF16)

    full = lambda shape: pl.BlockSpec(shape, lambda i: (0,) * len(shape))
    return pl.pallas_call(
        body, name=name, grid=(nb,),
        in_specs=[pl.BlockSpec((CHUNK, 2 * W), lambda i: (blk(i), 0))] + cols + halos
        + [full((CONV_K, W)), full((H, CHUNK, CHUNK)), full((H, CHUNK, CHUNK)), full((CHUNK, W)), full((1, W)),
           full((1, W)), full((W, LANES))],
        out_specs=[pl.BlockSpec((CHUNK, 5 * W), lambda i: (blk(i), 0)), full((SUBLANES, W)), full((1, W)), full((1, W)),
                   full((H, CHUNK, CHUNK)), full((CHUNK, LANES))],
        out_shape=[jax.ShapeDtypeStruct((T, 5 * W), BF16), jax.ShapeDtypeStruct((SUBLANES, W), F32),
                   jax.ShapeDtypeStruct((1, W), F32), jax.ShapeDtypeStruct((1, W), F32),
                   jax.ShapeDtypeStruct((H, CHUNK, CHUNK), F32), jax.ShapeDtypeStruct((CHUNK, LANES), F32)],
        scratch_shapes=[pltpu.VMEM((SUBLANES, W), F32), pltpu.VMEM((CHUNK, W), F32), pltpu.VMEM((CHUNK, W), F32)],
        compiler_params=_params(("arbitrary",)),
    )(dy, proj, proj, proj, proj, proj, proj, proj, conv_w, wm, wmt, bias_e, g_a, g_b, head_onehot)


def _cast_into_slot(w, chip, name):
    R, C = w.shape
    tr = _tile(R, 256, 16)

    def body(chip_ref, w_ref, o_ref):
        o_ref[...] = w_ref[...].astype(BF16)

    return pl.pallas_call(
        body, name=name,
        grid_spec=pltpu.PrefetchScalarGridSpec(
            num_scalar_prefetch=1, grid=(R // tr,),
            in_specs=[pl.BlockSpec((tr, C), lambda i, chip_ref: (i, 0))],
            out_specs=pl.BlockSpec((None, tr, C), lambda i, chip_ref: (chip_ref[0], i, 0))),
        out_shape=jax.ShapeDtypeStruct((N_CHIPS, R, C), BF16),
        compiler_params=_params(("parallel",)),
    )(chip, w)


def _position():
    x, y, c = lax.axis_index("x"), lax.axis_index("y"), lax.axis_index("c")
    return x, y, c


def _other_chips(x, y):
    return [(1 - x, y), (x, 1 - y), (1 - x, 1 - y)]


def _all_gather_weight(g, name):
    _, R, C = g.shape
    hr = R // 2

    def body(g_in, g_ref, send_sems, recv_sems):
        del g_in
        x, y, c = _position()
        sibling = (x, y, 1 - c)
        chips = _other_chips(x, y)

        def part(chip_xy, half):
            return g_ref.at[2 * chip_xy[0] + chip_xy[1], pl.ds(half * hr, hr), :]

        def copy(k, chip_xy, half, to):
            blk = part(chip_xy, half)
            return pltpu.make_async_remote_copy(src_ref=blk, dst_ref=blk, send_sem=send_sems.at[k],
                                                recv_sem=recv_sems.at[k], device_id=to, device_id_type=MESH)

        first = [copy(j, (x, y), c, (*chip, c)) for j, chip in enumerate(chips)]
        for cp in first:
            cp.start()
        passed = [copy(3 + j, chip, c, sibling) for j, chip in enumerate(chips)]
        for j, chip in enumerate(chips):
            copy(j, chip, c, (x, y, c)).wait_recv()
            passed[j].start()
        for j, chip in enumerate(chips):
            copy(3 + j, chip, 1 - c, (x, y, c)).wait_recv()
        for cp in first + passed:
            cp.wait_send()

    return pl.pallas_call(
        body, name=name,
        in_specs=[pl.BlockSpec(memory_space=pl.ANY)],
        out_specs=pl.BlockSpec(memory_space=pl.ANY),
        out_shape=jax.ShapeDtypeStruct(g.shape, g.dtype),
        scratch_shapes=[pltpu.SemaphoreType.DMA((6,)), pltpu.SemaphoreType.DMA((6,))],
        input_output_aliases={0: 0},
        compiler_params=pltpu.CompilerParams(has_side_effects=True),
    )(g)


def _rs_pair_exchange(dw, name):
    S, R, C = dw.shape
    hr = R // 2

    def body(dw_ref, got_ref, send_sem, recv_sem):
        x, y, c = _position()
        cp = pltpu.make_async_remote_copy(src_ref=dw_ref.at[:, pl.ds((1 - c) * hr, hr), :], dst_ref=got_ref,
                                          send_sem=send_sem, recv_sem=recv_sem, device_id=(x, y, 1 - c),
                                          device_id_type=MESH)
        cp.start()
        cp.wait_recv()
        cp.wait_send()

    return pl.pallas_call(
        body, name=name,
        in_specs=[pl.BlockSpec(memory_space=pl.ANY)],
        out_specs=pl.BlockSpec(memory_space=pl.ANY),
        out_shape=jax.ShapeDtypeStruct((S, hr, C), BF16),
        scratch_shapes=[pltpu.SemaphoreType.DMA, pltpu.SemaphoreType.DMA],
        compiler_params=pltpu.CompilerParams(has_side_effects=True),
    )(dw)


def _rs_pair_add(dw, got, chip_core, name):
    S, R, C = dw.shape
    hr = R // 2
    tr = _tile(hr, 256, 16)
    nrb = hr // tr

    def body(cc_ref, dw_ref, got_ref, send_ref, own_ref):
        s = dw_ref[...].astype(F32) + got_ref[...].astype(F32)
        send_ref[...] = s.astype(BF16)

        @pl.when(pl.program_id(1) == cc_ref[0])
        def _():
            own_ref[...] = s

    return pl.pallas_call(
        body, name=name,
        grid_spec=pltpu.PrefetchScalarGridSpec(
            num_scalar_prefetch=1, grid=(nrb, S),
            in_specs=[pl.BlockSpec((None, tr, C), lambda i, q, cc: (q, cc[1] * nrb + i, 0)),
                      pl.BlockSpec((None, tr, C), lambda i, q, cc: (q, i, 0))],
            out_specs=[pl.BlockSpec((None, tr, C), lambda i, q, cc: (q, i, 0)),
                       pl.BlockSpec((tr, C), lambda i, q, cc: (i, 0))]),
        out_shape=[jax.ShapeDtypeStruct((S, hr, C), BF16), jax.ShapeDtypeStruct((hr, C), F32)],
        compiler_params=_params(("parallel", "arbitrary")),
    )(chip_core, dw, got)


def _rs_chip_exchange(part, name):
    S, hr, C = part.shape

    def body(part_ref, got_ref, send_sems, recv_sems):
        x, y, c = _position()
        copies = []
        for j, chip in enumerate(_other_chips(x, y)):
            copies.append(pltpu.make_async_remote_copy(
                src_ref=part_ref.at[2 * chip[0] + chip[1]], dst_ref=got_ref.at[j], send_sem=send_sems.at[j],
                recv_sem=recv_sems.at[j], device_id=(*chip, c), device_id_type=MESH))
        for cp in copies:
            cp.start()
        for cp in copies:
            cp.wait_recv()
        for cp in copies:
            cp.wait_send()

    return pl.pallas_call(
        body, name=name,
        in_specs=[pl.BlockSpec(memory_space=pl.ANY)],
        out_specs=pl.BlockSpec(memory_space=pl.ANY),
        out_shape=jax.ShapeDtypeStruct((3, hr, C), BF16),
        scratch_shapes=[pltpu.SemaphoreType.DMA((3,)), pltpu.SemaphoreType.DMA((3,))],
        compiler_params=pltpu.CompilerParams(has_side_effects=True),
    )(part)


def _rs_final_add(own, got, chip_core, name):
    hr, C = own.shape
    tr = _tile(hr, 256, 16)
    nrb = hr // tr

    def body(cc_ref, own_ref, got_ref, o_ref):
        o_ref[...] = ((own_ref[...] + got_ref[0].astype(F32)) + got_ref[1].astype(F32)) + got_ref[2].astype(F32)

    return pl.pallas_call(
        body, name=name,
        grid_spec=pltpu.PrefetchScalarGridSpec(
            num_scalar_prefetch=1, grid=(nrb,),
            in_specs=[pl.BlockSpec((tr, C), lambda i, cc: (i, 0)), pl.BlockSpec((3, tr, C), lambda i, cc: (0, i, 0))],
            out_specs=pl.BlockSpec((tr, C), lambda i, cc: (cc[1] * nrb + i, 0))),
        out_shape=jax.ShapeDtypeStruct((2 * hr, C), F32),
        compiler_params=_params(("parallel",)),
    )(chip_core, own, got)


def _rs_half_exchange(grad, name):
    R, C = grad.shape
    hr = R // 2

    def body(g_in, g_ref, send_sem, recv_sem):
        del g_in
        x, y, c = _position()
        mine = g_ref.at[pl.ds(c * hr, hr), :]
        theirs = g_ref.at[pl.ds((1 - c) * hr, hr), :]
        cp = pltpu.make_async_remote_copy(src_ref=mine, dst_ref=mine, send_sem=send_sem, recv_sem=recv_sem,
                                          device_id=(x, y, 1 - c), device_id_type=MESH)
        cp.start()
        pltpu.make_async_remote_copy(src_ref=theirs, dst_ref=theirs, send_sem=send_sem, recv_sem=recv_sem,
                                     device_id=(x, y, 1 - c), device_id_type=MESH).wait_recv()
        cp.wait_send()

    return pl.pallas_call(
        body, name=name,
        in_specs=[pl.BlockSpec(memory_space=pl.ANY)],
        out_specs=pl.BlockSpec(memory_space=pl.ANY),
        out_shape=jax.ShapeDtypeStruct(grad.shape, grad.dtype),
        scratch_shapes=[pltpu.SemaphoreType.DMA, pltpu.SemaphoreType.DMA],
        input_output_aliases={0: 0},
        compiler_params=pltpu.CompilerParams(has_side_effects=True),
    )(grad)


def _adamw_math(w, g, m, v):
    m2 = ADAM_B1 * m + (1.0 - ADAM_B1) * g
    v2 = ADAM_B2 * v + (1.0 - ADAM_B2) * (g * g)
    delta = -ADAM_LR * ((m2 / ADAM_C1) / (jnp.sqrt(v2 / ADAM_C2) + ADAM_EPS) + ADAM_WD * w)
    return delta, m2, v2


def _adamw(w, g, m, v, name):
    R, C = w.shape
    tr = _tile(R, max(SUBLANES, (256 * 1024) // C), SUBLANES)

    def body(w_ref, g_ref, m_ref, v_ref, d_ref, m2_ref, v2_ref):
        d_ref[...], m2_ref[...], v2_ref[...] = _adamw_math(w_ref[...], g_ref[...], m_ref[...], v_ref[...])

    blk = pl.BlockSpec((tr, C), lambda i: (i, 0))
    return pl.pallas_call(
        body, name=name, grid=(R // tr,),
        in_specs=[blk] * 4, out_specs=[blk] * 3,
        out_shape=[jax.ShapeDtypeStruct((R, C), F32)] * 3,
        compiler_params=_params(("parallel",)),
    )(w, g, m, v)


def _reduce_scatter_and_update(dw, w, m, v, chip_core, tag):
    got = _rs_pair_exchange(dw, f"rs_pair_exchange_{tag}")
    part, own = _rs_pair_add(dw, got, chip_core, f"rs_pair_add_{tag}")
    landed = _rs_chip_exchange(part, f"rs_chip_exchange_{tag}")
    half = _rs_final_add(own, landed, chip_core, f"rs_final_add_{tag}")
    grad = _rs_half_exchange(half, f"rs_half_exchange_{tag}")
    delta, m2, v2 = _adamw(w, grad, m, v, f"adamw_{tag}")
    return grad, delta, m2, v2


def _all_gather_small(block, name):
    m_per, n = block.shape

    def body(x_ref, out_ref, send_sems, recv_sems, local_sem):
        x, y, c = _position()
        me, sibling = (x, y, c), (x, y, 1 - c)
        chips = _other_chips(x, y)

        def rows(px, py, pc):
            return out_ref.at[pl.ds((4 * px + 2 * py + pc) * m_per, m_per), :]

        def copy(k, blk, to, src=None):
            return pltpu.make_async_remote_copy(src_ref=rows(*blk) if src is None else src, dst_ref=rows(*blk),
                                                send_sem=send_sems.at[k], recv_sem=recv_sems.at[k], device_id=to,
                                                device_id_type=MESH)

        mine = pltpu.make_async_copy(x_ref, rows(*me), local_sem)
        mine.start()
        first = [copy(0, me, sibling, src=x_ref)]
        first += [copy(1 + j, me, (*chip, c), src=x_ref) for j, chip in enumerate(chips)]
        for cp in first:
            cp.start()
        passed = [copy(4 + j, (*chip, c), sibling) for j, chip in enumerate(chips)]
        for j, chip in enumerate(chips):
            copy(1 + j, (*chip, c), me).wait_recv()
            passed[j].start()
        copy(0, sibling, me).wait_recv()
        for j, chip in enumerate(chips):
            copy(4 + j, (*chip, 1 - c), me).wait_recv()
        for cp in first + passed:
            cp.wait_send()
        mine.wait()

    return pl.pallas_call(
        body, name=name,
        in_specs=[pl.BlockSpec(memory_space=pltpu.VMEM)],
        out_specs=pl.BlockSpec(memory_space=pltpu.VMEM),
        out_shape=jax.ShapeDtypeStruct((N_DEV * m_per, n), block.dtype),
        scratch_shapes=[pltpu.SemaphoreType.DMA((7,)), pltpu.SemaphoreType.DMA((7,)), pltpu.SemaphoreType.DMA],
        compiler_params=pltpu.CompilerParams(vmem_limit_bytes=VMEM_LIMIT_V7X, has_side_effects=True),
    )(block)


def _sum_and_adamw_small(gathered, w, m, v, name):
    rows, n = w.shape
    tr = _tile(rows, 32, SUBLANES)

    def body(p_ref, w_ref, m_ref, v_ref, g_ref, d_ref, m2_ref, v2_ref):
        g = p_ref[0]
        for d in range(1, N_DEV):
            g = g + p_ref[d]
        g_ref[...] = g
        d_ref[...], m2_ref[...], v2_ref[...] = _adamw_math(w_ref[...], g, m_ref[...], v_ref[...])

    blk = pl.BlockSpec((tr, n), lambda i: (i, 0))
    return pl.pallas_call(
        body, name=name, grid=(rows // tr,),
        in_specs=[pl.BlockSpec((N_DEV, tr, n), lambda i: (0, i, 0))] + [blk] * 3,
        out_specs=[blk] * 4,
        out_shape=[jax.ShapeDtypeStruct((rows, n), F32)] * 4,
        compiler_params=_params(("parallel",)),
    )(gathered.reshape(N_DEV, rows, n), w, m, v)


def _pad_rows(a):
    pad = (-a.shape[0]) % SUBLANES
    return jnp.pad(a, ((0, pad), (0, 0))) if pad else a


class _SmallPack:
    def __init__(self, W, D, H, chip):
        self.W, self.D, self.H, self.chip = W, D, H, chip
        self.offsets = {}
        self.rows = 0

    def pack(self, pieces):
        out = []
        self.offsets, self.rows = {}, 0
        for name, a in pieces:
            a = _pad_rows(a.astype(F32))
            self.offsets[name] = (self.rows, a.shape[0])
            self.rows += a.shape[0]
            out.append(a)
        return jnp.concatenate(out, axis=0)

    def piece(self, packed, name):
        start, n = self.offsets[name]
        return packed[start:start + n]


def _bias_rows(b, W):
    bt = jnp.pad(b.T, ((0, 0), (0, LANES - b.shape[0])))
    return bt.reshape(-1, W)


def _bias_from_rows(rows, H):
    return rows.reshape(-1)[:CHUNK * LANES].reshape(CHUNK, LANES)[:, :H].T


def kernel(x, mix_norm_g, w_in, conv_w, spatial_w, spatial_b, conv_out_norm_g, gmlp_out_norm_g, w_out, mlp_norm_g, w_up, w_down, final_norm_g, loss_target, m_mix_norm_g, m_w_in, m_conv_w, m_spatial_w, m_spatial_b, m_conv_out_norm_g, m_gmlp_out_norm_g, m_w_out, m_mlp_norm_g, m_w_up, m_w_down, m_final_norm_g, v_mix_norm_g, v_w_in, v_conv_w, v_spatial_w, v_spatial_b, v_conv_out_norm_g, v_gmlp_out_norm_g, v_w_out, v_mlp_norm_g, v_w_up, v_w_down, v_final_norm_g):
    Bl, S, D = x.shape
    T = Bl * S
    W = conv_out_norm_g.shape[-1]
    H = W // HEAD_DIM
    Wl = conv_w.shape[-1]
    xi, yi, ci = _position()
    chip = (2 * xi + yi).astype(jnp.int32)
    chip_arr = chip.reshape(1)
    chip_core = jnp.stack([chip, ci.astype(jnp.int32)])

    x2 = x.reshape(T, D)
    tgt2 = loss_target.reshape(T, D)

    gathered = {}
    for tag, w in (("w_in", w_in), ("w_out", w_out), ("w_up", w_up), ("w_down", w_down)):
        slot = _cast_into_slot(w[0], chip_arr, f"cast_{tag}")
        gathered[tag] = _all_gather_weight(slot, f"all_gather_{tag}")
    g_in, g_up = gathered["w_in"], gathered["w_up"]
    g_out = gathered["w_out"].reshape(-1, D)
    g_down = gathered["w_down"].reshape(-1, D)

    causal = jnp.tril(jnp.ones((CHUNK, CHUNK), dtype=bool))
    wm = jnp.where(causal[None], spatial_w[0], 0.0).astype(BF16)
    wmt = jnp.swapaxes(wm, 1, 2)
    bias_e = jnp.repeat(spatial_b[0].T, HEAD_DIM, axis=1)
    conv_full = lax.dynamic_update_slice(jnp.zeros((CONV_K, W), F32), conv_w[0], (0, chip * Wl))
    conv_gathered = _all_gather_small(_pad_rows(conv_full), "all_gather_conv_w")
    conv_w_all = conv_gathered.reshape(N_DEV, SUBLANES, W)[:, :CONV_K]
    conv_w_all = conv_w_all[0] + conv_w_all[2] + conv_w_all[4] + conv_w_all[6]
    head_onehot = (jnp.arange(W)[:, None] // HEAD_DIM == jnp.arange(LANES)[None, :]).astype(BF16)
    g_a, g_b = conv_out_norm_g, gmlp_out_norm_g

    xn = _rmsnorm_fwd(x2, mix_norm_g, "mix_norm_fwd")
    proj = _matmul(xn, g_in, mode="nn", name="proj_fwd", tm=1024, tn=1280, tk=1024, out_dtypes=[F32], b_shard="n")
    y = _mixers_fwd(proj, conv_w_all, wm, bias_e, g_a, g_b, S, "mixers_fwd")
    h1 = _matmul(y, g_out, mode="nn", name="out_proj_fwd", tm=1024, tn=1024, tk=1024, out_dtypes=[F32],
                 epilogue=_ep_residual, extras=(x2,))
    xn2 = _rmsnorm_fwd(h1, mlp_norm_g, "mlp_norm_fwd")
    r, a = _matmul(xn2, g_up, mode="nn", name="up_fwd", tm=1024, tn=1024, tk=1024, out_dtypes=[BF16, BF16],
                   epilogue=_ep_relu2, b_shard="n")
    h2 = _matmul(a, g_down, mode="nn", name="down_fwd", tm=1024, tn=1024, tk=1024, out_dtypes=[F32],
                 epilogue=_ep_residual, extras=(h1,))
    dh2, dh2b, d_final_g, loss_part = _loss_and_final_norm_bwd(h2, tgt2, final_norm_g.reshape(1, D), "loss_final_norm")

    dw_down = _matmul(a, dh2b, mode="tn", name="down_dw", tm=2048, tn=1024, tk=512, out_dtypes=[BF16])
    dpre = _matmul(dh2b, g_down, mode="nt", name="down_dx", tm=1024, tn=1024, tk=1024, out_dtypes=[BF16],
                   epilogue=_ep_relu2_bwd, extras=(r,))
    dw_up = _matmul(xn2, dpre, mode="tn", name="up_dw", tm=2048, tn=1024, tk=512, out_dtypes=[BF16], out_shard=True)
    dxn2 = _matmul(dpre, g_up, mode="nt", name="up_dx", tm=1024, tn=1024, tk=1024, out_dtypes=[F32], b_shard="k")
    dh1, dh1b, d_mlp_g = _rmsnorm_bwd(dxn2, h1, mlp_norm_g, dh2, "mlp_norm_bwd")
    dw_out = _matmul(y, dh1b, mode="tn", name="out_proj_dw", tm=2048, tn=1024, tk=512, out_dtypes=[BF16])
    dy = _matmul(dh1b, g_out, mode="nt", name="out_proj_dx", tm=1024, tn=1024, tk=1024, out_dtypes=[F32])
    dproj, d_conv, d_ga, d_gb, d_ws, d_bt = _mixers_bwd(dy, proj, conv_w_all, wm, wmt, bias_e, g_a, g_b, head_onehot,
                                                          S, "mixers_bwd")
    dw_in = _matmul(xn, dproj, mode="tn", name="proj_dw", tm=2048, tn=1280, tk=512, out_dtypes=[BF16], out_shard=True)
    dxn = _matmul(dproj, g_in, mode="nt", name="proj_dx", tm=1024, tn=1024, tk=1280, out_dtypes=[F32], b_shard="k")
    grad_x, _, d_mix_g = _rmsnorm_bwd(dxn, x2, mix_norm_g, dh1, "mix_norm_bwd")

    big = {}
    for tag, dw, w, m, v in (("w_down", dw_down.reshape(N_CHIPS, -1, D), w_down, m_w_down, v_w_down),
                             ("w_up", dw_up, w_up, m_w_up, v_w_up),
                             ("w_out", dw_out.reshape(N_CHIPS, -1, D), w_out, m_w_out, v_w_out),
                             ("w_in", dw_in, w_in, m_w_in, v_w_in)):
        big[tag] = [t[None] for t in _reduce_scatter_and_update(dw, w[0], m[0], v[0], chip_core, tag)]

    pack = _SmallPack(W, D, H, chip)
    causal_f = causal.astype(F32)
    loss_row = jnp.pad(loss_part[:, :1], ((0, 0), (0, W - 1)))

    def small(mix, conv, sw, sb, ga, gb, mlp, fin, extra):
        return pack.pack([("spatial_w", sw.reshape(-1, W)), ("conv_w", conv), ("mix_norm_g", mix.reshape(-1, W)),
                          ("mlp_norm_g", mlp.reshape(-1, W)), ("final_norm_g", fin.reshape(-1, W)),
                          ("conv_out_norm_g", ga.reshape(-1, W)), ("gmlp_out_norm_g", gb.reshape(-1, W)),
                          ("spatial_b", _bias_rows(sb, W)), ("loss", extra)])

    def full_conv(cw):
        return lax.dynamic_update_slice(jnp.zeros((CONV_K, W), F32), cw[0], (0, chip * Wl))

    zero_row = jnp.zeros((1, W), F32)
    g_part = pack.pack([("spatial_w", (d_ws * causal_f[None]).reshape(-1, W)), ("conv_w", d_conv),
                        ("mix_norm_g", d_mix_g.reshape(-1, W)), ("mlp_norm_g", d_mlp_g.reshape(-1, W)),
                        ("final_norm_g", d_final_g.reshape(-1, W)), ("conv_out_norm_g", d_ga), ("gmlp_out_norm_g", d_gb),
                        ("spatial_b", d_bt.reshape(-1, W)), ("loss", loss_row)])
    w_s = small(mix_norm_g, full_conv(conv_w), spatial_w, spatial_b[0], conv_out_norm_g, gmlp_out_norm_g, mlp_norm_g,
                final_norm_g, zero_row)
    m_s = small(m_mix_norm_g, full_conv(m_conv_w), m_spatial_w, m_spatial_b[0], m_conv_out_norm_g, m_gmlp_out_norm_g,
                m_mlp_norm_g, m_final_norm_g, zero_row)
    v_s = small(v_mix_norm_g, full_conv(v_conv_w), v_spatial_w, v_spatial_b[0], v_conv_out_norm_g, v_gmlp_out_norm_g,
                v_mlp_norm_g, v_final_norm_g, zero_row)
    g_all = _all_gather_small(g_part, "all_gather_small_grads")
    small_outs = _sum_and_adamw_small(g_all, w_s, m_s, v_s, "sum_adamw_small")

    def unpack(packed, name):
        rows = pack.piece(packed, name)
        if name == "spatial_w":
            return rows.reshape(1, H, CHUNK, CHUNK)
        if name == "conv_w":
            return lax.dynamic_slice(rows[:CONV_K], (0, chip * Wl), (CONV_K, Wl))[None]
        if name == "spatial_b":
            return _bias_from_rows(rows, H)[None]
        if name == "final_norm_g":
            return rows.reshape(-1)[:D]
        n = D if name in ("mix_norm_g", "mlp_norm_g") else W
        return rows.reshape(-1)[:n].reshape(1, n)

    loss = pack.piece(small_outs[0], "loss")[0, 0]
    order = ["mix_norm_g", "w_in", "conv_w", "spatial_w", "spatial_b", "conv_out_norm_g", "gmlp_out_norm_g", "w_out",
             "mlp_norm_g", "w_up", "w_down", "final_norm_g"]
    outs = [loss, grad_x.reshape(Bl, S, D)]
    for kind in range(4):
        for name in order:
            outs.append(big[name][kind] if name in big else unpack(small_outs[kind], name))
    return tuple(outs)
```

```python
import functools
import math

import jax
import jax.numpy as jnp
from jax import lax
from jax.experimental import pallas as pl
from jax.experimental.pallas import tpu as pltpu

F32 = jnp.float32
BF16 = jnp.bfloat16
MESH = pl.DeviceIdType.MESH

NORM_EPS = 1e-5
HEAD_DIM = 128
CHUNK = 128
CONV_K = 3
N_CHIPS = 4
N_DEV = 8

ADAM_LR = 0.001
ADAM_B1 = 0.9
ADAM_B2 = 0.999
ADAM_EPS = 1e-08
ADAM_WD = 0.01
ADAM_STEP = 10
ADAM_C1 = 1.0 - ADAM_B1 ** ADAM_STEP
ADAM_C2 = 1.0 - ADAM_B2 ** ADAM_STEP

GELU_K = math.sqrt(2.0 / math.pi)
GELU_A = 0.044715

VMEM_LIMIT_V7X = 56 * 1024 * 1024
SUBLANES = 8
LANES = 128


def _tile(dim, target, mult=LANES):
    if dim <= target:
        return dim
    t = (target // mult) * mult
    while t > mult and dim % t:
        t -= mult
    assert dim % t == 0, (dim, target, mult)
    return t


def _params(sem=None):
    return pltpu.CompilerParams(dimension_semantics=sem, vmem_limit_bytes=VMEM_LIMIT_V7X)


class _Stage:
    bufs = ()
    n_sems = 0

    base = 0

    def start(self, refs, send, recv):
        raise NotImplementedError

    def finish(self, refs, send, recv):
        raise NotImplementedError


def _position():
    return lax.axis_index("x"), lax.axis_index("y"), lax.axis_index("c")


def _other_chips(x, y):
    return [(1 - x, y), (x, 1 - y), (1 - x, 1 - y)]


def _remote(src, dst, send, recv, k, to):
    return pltpu.make_async_remote_copy(src_ref=src, dst_ref=dst, send_sem=send.at[k], recv_sem=recv.at[k],
                                        device_id=to, device_id_type=MESH)


def _call(body, *, name, args, in_specs, out_specs, out_shape, grid=(), scratch_shapes=(), semantics=None, stages=()):
    n_in, n_out, n_scratch = len(args), len(out_shape), len(scratch_shapes)
    any_spec = pl.BlockSpec(memory_space=pl.ANY)
    extra_args, extra_out, aliases, layout = [], [], {}, []
    for st in stages:
        where = []
        for kind, buf in st.bufs:
            if kind in ("in", "alias"):
                extra_args.append(buf)
                pos_in = n_in + len(extra_args) - 1
            if kind in ("out", "alias"):
                extra_out.append(jax.ShapeDtypeStruct(buf.shape, buf.dtype))
                pos_out = n_out + len(extra_out) - 1
            if kind == "alias":
                aliases[pos_in] = pos_out
            where.append(("in", pos_in) if kind == "in" else ("out", pos_out))
        layout.append(where)
    n_sems = sum(st.n_sems for st in stages)
    n_xin, n_xout = len(extra_args), len(extra_out)

    def wrapped(*refs):
        ins = refs[:n_in + n_xin]
        outs = refs[n_in + n_xin:n_in + n_xin + n_out + n_xout]
        scratch = refs[n_in + n_xin + n_out + n_xout:]
        main = ins[:n_in] + outs[:n_out] + scratch[:n_scratch]
        if not stages:
            body(*main)
            return
        send, recv = scratch[n_scratch], scratch[n_scratch + 1]
        first = last = None
        for d, g in enumerate(grid):
            pid = pl.program_id(d)
            f, l = pid == 0, pid == g - 1
            first = f if first is None else first & f
            last = l if last is None else last & l
        base, views = 0, []
        for st, where in zip(stages, layout):
            st_refs = [ins[p] if side == "in" else outs[p] for side, p in where]
            st.base = base
            views.append((st, st_refs, send, recv))
            base += st.n_sems

        def starts():
            for st, r, s, v in views:
                st.start(r, s, v)

        def finishes():
            for st, r, s, v in views:
                st.finish(r, s, v)

        if first is None:
            starts()
            body(*main)
            finishes()
        else:
            pl.when(first)(starts)
            body(*main)
            pl.when(last)(finishes)

    sems = [pltpu.SemaphoreType.DMA((n_sems,)), pltpu.SemaphoreType.DMA((n_sems,))] if stages else []
    kw = dict(grid=grid) if grid else {}
    res = pl.pallas_call(
        wrapped, name=name,
        in_specs=list(in_specs) + [any_spec] * n_xin,
        out_specs=list(out_specs) + [any_spec] * n_xout,
        out_shape=list(out_shape) + extra_out,
        scratch_shapes=list(scratch_shapes) + sems,
        input_output_aliases=aliases,
        compiler_params=pltpu.CompilerParams(
            dimension_semantics=("arbitrary",) * len(grid) if stages and grid else semantics,
            vmem_limit_bytes=VMEM_LIMIT_V7X, has_side_effects=bool(stages)),
        **kw,
    )(*args, *extra_args)
    main_res, stage_res, pos = list(res[:n_out]), [], n_out
    for st in stages:
        k = sum(kind in ("out", "alias") for kind, _ in st.bufs)
        stage_res.append(list(res[pos:pos + k]))
        pos += k
    return main_res, stage_res


class _GatherRows(_Stage):
    n_sems = 6

    def __init__(self, g, lo=0, n=None):
        self.hr = g.shape[1] // 2
        self.lo, self.n = lo, (self.hr if n is None else n)
        self.bufs = [("alias", g)]

    def _part(self, g_ref, chip_xy, half):
        return g_ref.at[2 * chip_xy[0] + chip_xy[1], pl.ds(half * self.hr + self.lo, self.n), :]

    def _copy(self, g_ref, send, recv, k, chip_xy, half, to):
        blk = self._part(g_ref, chip_xy, half)
        return _remote(blk, blk, send, recv, self.base + k, to)

    def start(self, refs, send, recv):
        x, y, c = _position()
        for j, chip in enumerate(_other_chips(x, y)):
            self._copy(refs[0], send, recv, j, (x, y), c, (*chip, c)).start()

    def finish(self, refs, send, recv):
        x, y, c = _position()
        g, me, chips = refs[0], (x, y, c), _other_chips(x, y)
        passed = [self._copy(g, send, recv, 3 + j, chip, c, (x, y, 1 - c)) for j, chip in enumerate(chips)]
        for j, chip in enumerate(chips):
            self._copy(g, send, recv, j, chip, c, me).wait_recv()
            passed[j].start()
        for j, chip in enumerate(chips):
            self._copy(g, send, recv, 3 + j, chip, 1 - c, me).wait_recv()
        for j, chip in enumerate(chips):
            self._copy(g, send, recv, j, (x, y), c, (*chip, c)).wait_send()
            passed[j].wait_send()


class _PairExchange(_Stage):
    n_sems = 1

    def __init__(self, dw):
        S, R, C = dw.shape
        self.hr = R // 2
        self.bufs = [("in", dw), ("out", jax.ShapeDtypeStruct((S, self.hr, C), dw.dtype))]

    def _copy(self, refs, send, recv):
        x, y, c = _position()
        return _remote(refs[0].at[:, pl.ds((1 - c) * self.hr, self.hr), :], refs[1], send, recv, self.base,
                       (x, y, 1 - c))

    def start(self, refs, send, recv):
        self._copy(refs, send, recv).start()

    def finish(self, refs, send, recv):
        cp = self._copy(refs, send, recv)
        cp.wait_recv()
        cp.wait_send()


class _ChipExchange(_Stage):
    n_sems = 3

    def __init__(self, part, landed=None, lo=0, n=None):
        S, hr, C = part.shape
        self.lo, self.n = lo, (hr if n is None else n)
        self.bufs = [("in", part), ("out", jax.ShapeDtypeStruct((3, hr, C), part.dtype)) if landed is None
                     else ("alias", landed)]

    def _copies(self, refs, send, recv):
        x, y, c = _position()
        rows = pl.ds(self.lo, self.n)
        return [_remote(refs[0].at[2 * chip[0] + chip[1], rows, :], refs[1].at[j, rows, :], send, recv,
                        self.base + j, (*chip, c))
                for j, chip in enumerate(_other_chips(x, y))]

    def start(self, refs, send, recv):
        for cp in self._copies(refs, send, recv):
            cp.start()

    def finish(self, refs, send, recv):
        copies = self._copies(refs, send, recv)
        for cp in copies:
            cp.wait_recv()
        for cp in copies:
            cp.wait_send()


class _HalfExchange(_Stage):
    n_sems = 1

    def __init__(self, grad):
        self.hr = grad.shape[0] // 2
        self.bufs = [("alias", grad)]

    def start(self, refs, send, recv):
        x, y, c = _position()
        mine = refs[0].at[pl.ds(c * self.hr, self.hr), :]
        _remote(mine, mine, send, recv, self.base, (x, y, 1 - c)).start()

    def finish(self, refs, send, recv):
        x, y, c = _position()
        mine = refs[0].at[pl.ds(c * self.hr, self.hr), :]
        theirs = refs[0].at[pl.ds((1 - c) * self.hr, self.hr), :]
        _remote(theirs, theirs, send, recv, self.base, (x, y, 1 - c)).wait_recv()
        _remote(mine, mine, send, recv, self.base, (x, y, 1 - c)).wait_send()


def _matmul(a, b, *, mode, name, tm, tn, tk, out_dtypes, epilogue=None, extras=(), b_shard=None, out_shard=False,
            stages=()):
    if mode == "tn":
        K, M = a.shape
    else:
        M, K = a.shape
    if b_shard == "n":
        S, Kb, Ns = b.shape
        N = S * Ns
    elif b_shard == "k":
        S, N, Ks = b.shape
        Kb = S * Ks
    elif mode == "nt":
        N, Kb = b.shape
    else:
        Kb, N = b.shape
    assert Kb == K, (name, a.shape, b.shape)
    tm, tn, tk = _tile(M, tm), _tile(N, tn), _tile(K, tk)
    if b_shard == "n" or out_shard:
        n_per = N // N_CHIPS
        tn = _tile(n_per, tn)
        njs = n_per // tn
    if b_shard == "k":
        tk = _tile(K // N_CHIPS, tk)
        nks = (K // N_CHIPS) // tk
    gm, gn, gk = M // tm, N // tn, K // tk

    if mode == "tn":
        a_spec = pl.BlockSpec((tk, tm), lambda i, j, k: (k, i))
        dims = (((0,), (0,)), ((), ()))
    else:
        a_spec = pl.BlockSpec((tm, tk), lambda i, j, k: (i, k))
        dims = (((1,), (1,)), ((), ())) if mode == "nt" else (((1,), (0,)), ((), ()))
    if b_shard == "n":
        b_spec = pl.BlockSpec((None, tk, tn), lambda i, j, k: (j // njs, k, j % njs))
    elif b_shard == "k":
        b_spec = pl.BlockSpec((None, tn, tk), lambda i, j, k: (k // nks, j, k % nks))
    elif mode == "nt":
        b_spec = pl.BlockSpec((tn, tk), lambda i, j, k: (j, k))
    else:
        b_spec = pl.BlockSpec((tk, tn), lambda i, j, k: (k, j))
    mn_spec = pl.BlockSpec((tm, tn), lambda i, j, k: (i, j))
    if out_shard:
        out_spec = pl.BlockSpec((None, tm, tn), lambda i, j, k: (j // njs, i, j % njs))
        out_shape = [jax.ShapeDtypeStruct((N_CHIPS, M, N // N_CHIPS), dt) for dt in out_dtypes]
    else:
        out_spec = mn_spec
        out_shape = [jax.ShapeDtypeStruct((M, N), dt) for dt in out_dtypes]
    n_extra, n_out = len(extras), len(out_dtypes)

    def finish_tile(acc, extra_refs, out_refs):
        if epilogue is None:
            for o in out_refs:
                o[...] = acc.astype(o.dtype)
        else:
            epilogue(acc, extra_refs, out_refs)

    def body(*refs):
        a_ref, b_ref = refs[0], refs[1]
        extra_refs = refs[2:2 + n_extra]
        out_refs = refs[2 + n_extra:2 + n_extra + n_out]

        def product():
            return lax.dot_general(a_ref[...], b_ref[...], dims, preferred_element_type=F32)

        if gk == 1:
            finish_tile(product(), extra_refs, out_refs)
            return
        acc_ref = refs[-1]
        k = pl.program_id(2)

        @pl.when(k == 0)
        def _():
            acc_ref[...] = product()

        @pl.when((k > 0) & (k < gk - 1))
        def _():
            acc_ref[...] += product()

        @pl.when(k == gk - 1)
        def _():
            finish_tile(acc_ref[...] + product(), extra_refs, out_refs)

    outs, carried = _call(
        body, name=name, args=[a, b, *extras], grid=(gm, gn, gk),
        in_specs=[a_spec, b_spec] + [mn_spec] * n_extra, out_specs=[out_spec] * n_out, out_shape=out_shape,
        scratch_shapes=[pltpu.VMEM((tm, tn), F32)] if gk > 1 else [],
        semantics=("parallel", "parallel", "arbitrary"), stages=stages)
    return (outs[0] if n_out == 1 else outs), carried


def _ep_residual(acc, extra_refs, out_refs):
    out_refs[0][...] = extra_refs[0][...] + acc


def _ep_relu2(acc, extra_refs, out_refs):
    r = jnp.maximum(acc, 0.0)
    out_refs[0][...] = r.astype(BF16)
    out_refs[1][...] = (r * r).astype(BF16)


def _ep_relu2_bwd(acc, extra_refs, out_refs):
    out_refs[0][...] = (acc * (2.0 * extra_refs[0][...].astype(F32))).astype(BF16)


def _row_inv(x):
    return lax.rsqrt(jnp.mean(x * x, axis=-1, keepdims=True) + NORM_EPS)


def _rmsnorm_fwd(x, g, name, stages=()):
    T, D = x.shape
    tt = _tile(T, 256, SUBLANES)

    def body(x_ref, g_ref, o_ref):
        xv = x_ref[...]
        o_ref[...] = (xv * _row_inv(xv) * g_ref[...]).astype(BF16)

    outs, carried = _call(
        body, name=name, args=[x, g], grid=(T // tt,),
        in_specs=[pl.BlockSpec((tt, D), lambda i: (i, 0)), pl.BlockSpec((1, D), lambda i: (0, 0))],
        out_specs=[pl.BlockSpec((tt, D), lambda i: (i, 0))], out_shape=[jax.ShapeDtypeStruct((T, D), BF16)],
        semantics=("parallel",), stages=stages)
    return outs[0], carried


def _rmsnorm_bwd(dxn, h, g, dres, name, stages=()):
    T, D = h.shape
    tt = _tile(T, 128, SUBLANES)

    def body(dxn_ref, h_ref, g_ref, dres_ref, dh_ref, dhb_ref, dg_ref):
        @pl.when(pl.program_id(0) == 0)
        def _():
            dg_ref[...] = jnp.zeros_like(dg_ref)

        hv = h_ref[...]
        inv = _row_inv(hv)
        n = hv * inv
        d = dxn_ref[...]
        dg_ref[...] += jnp.sum(d * n, axis=0, keepdims=True)
        dn = d * g_ref[...]
        dh = dres_ref[...] + inv * (dn - n * jnp.mean(dn * n, axis=-1, keepdims=True))
        dh_ref[...] = dh
        dhb_ref[...] = dh.astype(BF16)

    row = pl.BlockSpec((tt, D), lambda i: (i, 0))
    vec = pl.BlockSpec((1, D), lambda i: (0, 0))
    return _call(
        body, name=name, args=[dxn, h, g, dres], grid=(T // tt,), in_specs=[row, row, vec, row],
        out_specs=[row, row, vec],
        out_shape=[jax.ShapeDtypeStruct((T, D), F32), jax.ShapeDtypeStruct((T, D), BF16),
                   jax.ShapeDtypeStruct((1, D), F32)],
        semantics=("arbitrary",), stages=stages)


def _loss_and_final_norm_bwd(h1, d2, tgt, g, name):
    T, D = h1.shape
    tt = _tile(T, 128, SUBLANES)

    def body(h1_ref, d2_ref, t_ref, g_ref, dh_ref, dhb_ref, dg_ref, loss_ref):
        @pl.when(pl.program_id(0) == 0)
        def _():
            dg_ref[...] = jnp.zeros_like(dg_ref)
            loss_ref[...] = jnp.zeros_like(loss_ref)

        hv = h1_ref[...] + d2_ref[...]
        gv = g_ref[...]
        inv = _row_inv(hv)
        n = hv * inv
        err = n * gv - t_ref[...]
        loss_ref[...] += 0.5 * jnp.sum(jnp.mean(err * err, axis=-1, keepdims=True))
        dy = err * (1.0 / D)
        dg_ref[...] += jnp.sum(dy * n, axis=0, keepdims=True)
        dn = dy * gv
        dh = inv * (dn - n * jnp.mean(dn * n, axis=-1, keepdims=True))
        dh_ref[...] = dh
        dhb_ref[...] = dh.astype(BF16)

    row = pl.BlockSpec((tt, D), lambda i: (i, 0))
    vec = pl.BlockSpec((1, D), lambda i: (0, 0))
    one = pl.BlockSpec((1, LANES), lambda i: (0, 0))
    return _call(
        body, name=name, args=[h1, d2, tgt, g], grid=(T // tt,), in_specs=[row, row, row, vec],
        out_specs=[row, row, vec, one],
        out_shape=[jax.ShapeDtypeStruct((T, D), F32), jax.ShapeDtypeStruct((T, D), BF16),
                   jax.ShapeDtypeStruct((1, D), F32), jax.ShapeDtypeStruct((1, LANES), F32)],
        semantics=("arbitrary",))[0]


def _gelu(x):
    th = jnp.tanh(GELU_K * (x + GELU_A * (x * x * x)))
    return 0.5 * x * (1.0 + th), th


def _gelu_grad(x, th):
    return 0.5 * (1.0 + th) + 0.5 * x * (1.0 - th * th) * (GELU_K * (1.0 + 3.0 * GELU_A * (x * x)))


def _shift_rows(cur, prev_rows, k):
    rolled = pltpu.roll(cur, k, 0)
    row = lax.broadcasted_iota(jnp.int32, cur.shape, 0)
    out = rolled
    for r in range(k):
        out = jnp.where(row == r, prev_rows[SUBLANES - k + r:SUBLANES - k + r + 1, :], out)
    return out


def _unshift_rows(cur, next_rows, k):
    n = cur.shape[0]
    rolled = pltpu.roll(cur, n - k, 0)
    row = lax.broadcasted_iota(jnp.int32, cur.shape, 0)
    out = rolled
    for r in range(k):
        out = jnp.where(row == n - k + r, next_rows[r:r + 1, :], out)
    return out


def _mixer_specs(W, blk, halo):
    cols = [pl.BlockSpec((CHUNK, W), functools.partial(lambda i, col: (blk(i), col), col=col)) for col in range(5)]
    halos = [pl.BlockSpec((SUBLANES, W), functools.partial(lambda i, col: (halo(i), col), col=col)) for col in (1, 2)]
    return cols, halos


def _mixers_fwd(proj, conv_w, wm, bias_e, g_a, g_b, seq_len, name, stages=()):
    T, W5 = proj.shape
    W = W5 // 5
    H = W // HEAD_DIM
    per_seq = seq_len // CHUNK
    rb = CHUNK // SUBLANES
    cols, halos = _mixer_specs(W, lambda i: i, lambda i: jnp.maximum(i * rb - 1, 0))

    def body(b_ref, c_ref, hin_ref, u_ref, v_ref, ch_ref, hh_ref, cw_ref, wm_ref, be_ref, ga_ref, gb_ref, y_ref, s_ref):
        first = (pl.program_id(0) % per_seq) == 0
        hc = c_ref[...] * hin_ref[...]
        hc_prev = jnp.where(first, 0.0, ch_ref[...] * hh_ref[...])
        cw = cw_ref[...]
        ya = b_ref[...] * (cw[0:1, :] * _shift_rows(hc, hc_prev, 2) + cw[1:2, :] * _shift_rows(hc, hc_prev, 1)
                           + cw[2:3, :] * hc)
        y_ref[:, 0:W] = (ya * _row_inv(ya) * ga_ref[...]).astype(BF16)
        gu, _ = _gelu(u_ref[...])
        gv, _ = _gelu(v_ref[...])
        gvb = gv.astype(BF16)
        for hd in range(H):
            sl = slice(hd * HEAD_DIM, (hd + 1) * HEAD_DIM)
            s_ref[:, sl] = jnp.dot(wm_ref[hd], gvb[:, sl], preferred_element_type=F32)
        yb = gu * (s_ref[...] + be_ref[...])
        y_ref[:, W:2 * W] = (yb * _row_inv(yb) * gb_ref[...]).astype(BF16)

    full = lambda shape: pl.BlockSpec(shape, lambda i: (0,) * len(shape))
    outs, carried = _call(
        body, name=name, args=[proj, proj, proj, proj, proj, proj, proj, conv_w, wm, bias_e, g_a, g_b],
        grid=(T // CHUNK,),
        in_specs=cols + halos + [full((CONV_K, W)), full((H, CHUNK, CHUNK)), full((CHUNK, W)), full((1, W)), full((1, W))],
        out_specs=[pl.BlockSpec((CHUNK, 2 * W), lambda i: (i, 0))], out_shape=[jax.ShapeDtypeStruct((T, 2 * W), BF16)],
        scratch_shapes=[pltpu.VMEM((CHUNK, W), F32)], semantics=("parallel",), stages=stages)
    return outs[0], carried


def _mixers_bwd(dy, proj, conv_w, wm, wmt, bias_e, g_a, g_b, head_onehot, seq_len, name, stages=()):
    T, W5 = proj.shape
    W = W5 // 5
    H = W // HEAD_DIM
    nb = T // CHUNK
    per_seq = seq_len // CHUNK
    rb = CHUNK // SUBLANES
    blk = lambda i: nb - 1 - i
    cols, halos = _mixer_specs(W, blk, lambda i: jnp.maximum(blk(i) * rb - 1, 0))

    def body(dy_ref, b_ref, c_ref, hin_ref, u_ref, v_ref, ch_ref, hh_ref, cw_ref, wm_ref, wmt_ref, be_ref, ga_ref,
             gb_ref, oh_ref, dp_ref, dcw_ref, dga_ref, dgb_ref, dws_ref, dbt_ref, carry_ref, s_ref, dgv_ref):
        i = pl.program_id(0)
        j = nb - 1 - i

        @pl.when(i == 0)
        def _():
            for r in (dcw_ref, dga_ref, dgb_ref, dws_ref, dbt_ref, carry_ref):
                r[...] = jnp.zeros_like(r)

        first = (j % per_seq) == 0
        last = (j % per_seq) == per_seq - 1
        b, c, hin = b_ref[...], c_ref[...], hin_ref[...]
        cw = cw_ref[...]
        hc = c * hin
        hc_prev = jnp.where(first, 0.0, ch_ref[...] * hh_ref[...])
        hc1 = _shift_rows(hc, hc_prev, 1)
        hc2 = _shift_rows(hc, hc_prev, 2)
        conv = cw[0:1, :] * hc2 + cw[1:2, :] * hc1 + cw[2:3, :] * hc
        ya = b * conv
        inv_a = _row_inv(ya)
        na = ya * inv_a
        do_a = dy_ref[:, 0:W]
        dga_ref[...] += jnp.sum(do_a * na, axis=0, keepdims=True)
        dna = do_a * ga_ref[...]
        dya = inv_a * (dna - na * jnp.mean(dna * na, axis=-1, keepdims=True))
        dp_ref[:, 0:W] = (dya * conv).astype(BF16)
        dconv = dya * b
        dcw_ref[0:1, :] += jnp.sum(dconv * hc2, axis=0, keepdims=True)
        dcw_ref[1:2, :] += jnp.sum(dconv * hc1, axis=0, keepdims=True)
        dcw_ref[2:3, :] += jnp.sum(dconv * hc, axis=0, keepdims=True)
        nxt = jnp.where(last, 0.0, carry_ref[...])
        dhc = cw[2:3, :] * dconv + cw[1:2, :] * _unshift_rows(dconv, nxt, 1) + cw[0:1, :] * _unshift_rows(dconv, nxt, 2)
        carry_ref[...] = dconv[0:SUBLANES, :]
        dp_ref[:, W:2 * W] = (dhc * hin).astype(BF16)
        dp_ref[:, 2 * W:3 * W] = (dhc * c).astype(BF16)
        u, v = u_ref[...], v_ref[...]
        gu, thu = _gelu(u)
        gv, thv = _gelu(v)
        gvb = gv.astype(BF16)
        for hd in range(H):
            sl = slice(hd * HEAD_DIM, (hd + 1) * HEAD_DIM)
            s_ref[:, sl] = jnp.dot(wm_ref[hd], gvb[:, sl], preferred_element_type=F32)
        s = s_ref[...] + be_ref[...]
        yb = gu * s
        inv_b = _row_inv(yb)
        nbv = yb * inv_b
        do_b = dy_ref[:, W:2 * W]
        dgb_ref[...] += jnp.sum(do_b * nbv, axis=0, keepdims=True)
        dnb = do_b * gb_ref[...]
        dyb = inv_b * (dnb - nbv * jnp.mean(dnb * nbv, axis=-1, keepdims=True))
        dp_ref[:, 3 * W:4 * W] = (dyb * s * _gelu_grad(u, thu)).astype(BF16)
        dsb = (dyb * gu).astype(BF16)
        dbt_ref[...] += jnp.dot(dsb, oh_ref[...], preferred_element_type=F32)
        for hd in range(H):
            sl = slice(hd * HEAD_DIM, (hd + 1) * HEAD_DIM)
            dws_ref[hd] += lax.dot_general(dsb[:, sl], gvb[:, sl], (((1,), (1,)), ((), ())), preferred_element_type=F32)
            dgv_ref[:, sl] = jnp.dot(wmt_ref[hd], dsb[:, sl], preferred_element_type=F32)
        dp_ref[:, 4 * W:5 * W] = (dgv_ref[...] * _gelu_grad(v, thv)).astype(BF16)

    full = lambda shape: pl.BlockSpec(shape, lambda i: (0,) * len(shape))
    return _call(
        body, name=name, grid=(nb,),
        args=[dy, proj, proj, proj, proj, proj, proj, proj, conv_w, wm, wmt, bias_e, g_a, g_b, head_onehot],
        in_specs=[pl.BlockSpec((CHUNK, 2 * W), lambda i: (blk(i), 0))] + cols + halos
        + [full((CONV_K, W)), full((H, CHUNK, CHUNK)), full((H, CHUNK, CHUNK)), full((CHUNK, W)), full((1, W)),
           full((1, W)), full((W, LANES))],
        out_specs=[pl.BlockSpec((CHUNK, 5 * W), lambda i: (blk(i), 0)), full((SUBLANES, W)), full((1, W)), full((1, W)),
                   full((H, CHUNK, CHUNK)), full((CHUNK, LANES))],
        out_shape=[jax.ShapeDtypeStruct((T, 5 * W), BF16), jax.ShapeDtypeStruct((SUBLANES, W), F32),
                   jax.ShapeDtypeStruct((1, W), F32), jax.ShapeDtypeStruct((1, W), F32),
                   jax.ShapeDtypeStruct((H, CHUNK, CHUNK), F32), jax.ShapeDtypeStruct((CHUNK, LANES), F32)],
        scratch_shapes=[pltpu.VMEM((SUBLANES, W), F32), pltpu.VMEM((CHUNK, W), F32), pltpu.VMEM((CHUNK, W), F32)],
        semantics=("arbitrary",), stages=stages)


def _cast_into_slot(w, chip, name):
    R, C = w.shape
    tr = _tile(R, 256, 16)

    def body(chip_ref, w_ref, o_ref):
        o_ref[...] = w_ref[...].astype(BF16)

    return pl.pallas_call(
        body, name=name,
        grid_spec=pltpu.PrefetchScalarGridSpec(
            num_scalar_prefetch=1, grid=(R // tr,),
            in_specs=[pl.BlockSpec((tr, C), lambda i, chip_ref: (i, 0))],
            out_specs=pl.BlockSpec((None, tr, C), lambda i, chip_ref: (chip_ref[0], i, 0))),
        out_shape=jax.ShapeDtypeStruct((N_CHIPS, R, C), BF16),
        compiler_params=_params(("parallel",)),
    )(chip, w)


def _rs_pair_add(dw, got, chip_core, name):
    S, R, C = dw.shape
    hr = R // 2
    tr = _tile(hr, 256, 16)
    nrb = hr // tr

    def body(cc_ref, dw_ref, got_ref, send_ref, own_ref):
        s = dw_ref[...].astype(F32) + got_ref[...].astype(F32)
        send_ref[...] = s.astype(BF16)

        @pl.when(pl.program_id(1) == cc_ref[0])
        def _():
            own_ref[...] = s

    return pl.pallas_call(
        body, name=name,
        grid_spec=pltpu.PrefetchScalarGridSpec(
            num_scalar_prefetch=1, grid=(nrb, S),
            in_specs=[pl.BlockSpec((None, tr, C), lambda i, q, cc: (q, cc[1] * nrb + i, 0)),
                      pl.BlockSpec((None, tr, C), lambda i, q, cc: (q, i, 0))],
            out_specs=[pl.BlockSpec((None, tr, C), lambda i, q, cc: (q, i, 0)),
                       pl.BlockSpec((tr, C), lambda i, q, cc: (i, 0))]),
        out_shape=[jax.ShapeDtypeStruct((S, hr, C), BF16), jax.ShapeDtypeStruct((hr, C), F32)],
        compiler_params=_params(("parallel", "arbitrary")),
    )(chip_core, dw, got)


def _rs_final_add(own, got, chip_core, name):
    hr, C = own.shape
    tr = _tile(hr, 256, 16)
    nrb = hr // tr

    def body(cc_ref, own_ref, got_ref, o_ref):
        o_ref[...] = ((own_ref[...] + got_ref[0].astype(F32)) + got_ref[1].astype(F32)) + got_ref[2].astype(F32)

    return pl.pallas_call(
        body, name=name,
        grid_spec=pltpu.PrefetchScalarGridSpec(
            num_scalar_prefetch=1, grid=(nrb,),
            in_specs=[pl.BlockSpec((tr, C), lambda i, cc: (i, 0)), pl.BlockSpec((3, tr, C), lambda i, cc: (0, i, 0))],
            out_specs=pl.BlockSpec((tr, C), lambda i, cc: (cc[1] * nrb + i, 0))),
        out_shape=jax.ShapeDtypeStruct((2 * hr, C), F32),
        compiler_params=_params(("parallel",)),
    )(chip_core, own, got)


def _adamw_math(w, g, m, v):
    m2 = ADAM_B1 * m + (1.0 - ADAM_B1) * g
    v2 = ADAM_B2 * v + (1.0 - ADAM_B2) * (g * g)
    delta = -ADAM_LR * ((m2 / ADAM_C1) / (jnp.sqrt(v2 / ADAM_C2) + ADAM_EPS) + ADAM_WD * w)
    return delta, m2, v2


def _adamw(w, g, m, v, name, stages=()):
    R, C = w.shape
    tr = _tile(R, max(SUBLANES, (256 * 1024) // C), SUBLANES)

    def body(w_ref, g_ref, m_ref, v_ref, d_ref, m2_ref, v2_ref):
        d_ref[...], m2_ref[...], v2_ref[...] = _adamw_math(w_ref[...], g_ref[...], m_ref[...], v_ref[...])

    blk = pl.BlockSpec((tr, C), lambda i: (i, 0))
    return _call(body, name=name, args=[w, g, m, v], grid=(R // tr,), in_specs=[blk] * 4, out_specs=[blk] * 3,
                 out_shape=[jax.ShapeDtypeStruct((R, C), F32)] * 3, semantics=("parallel",), stages=stages)


def _all_gather_small(block, name):
    m_per, n = block.shape

    def body(x_ref, out_ref, send_sems, recv_sems, local_sem):
        x, y, c = _position()
        me, sibling = (x, y, c), (x, y, 1 - c)
        chips = _other_chips(x, y)

        def rows(px, py, pc):
            return out_ref.at[pl.ds((4 * px + 2 * py + pc) * m_per, m_per), :]

        def copy(k, blk, to, src=None):
            return pltpu.make_async_remote_copy(src_ref=rows(*blk) if src is None else src, dst_ref=rows(*blk),
                                                send_sem=send_sems.at[k], recv_sem=recv_sems.at[k], device_id=to,
                                                device_id_type=MESH)

        mine = pltpu.make_async_copy(x_ref, rows(*me), local_sem)
        mine.start()
        first = [copy(0, me, sibling, src=x_ref)]
        first += [copy(1 + j, me, (*chip, c), src=x_ref) for j, chip in enumerate(chips)]
        for cp in first:
            cp.start()
        passed = [copy(4 + j, (*chip, c), sibling) for j, chip in enumerate(chips)]
        for j, chip in enumerate(chips):
            copy(1 + j, (*chip, c), me).wait_recv()
            passed[j].start()
        copy(0, sibling, me).wait_recv()
        for j, chip in enumerate(chips):
            copy(4 + j, (*chip, 1 - c), me).wait_recv()
        for cp in first + passed:
            cp.wait_send()
        mine.wait()

    return pl.pallas_call(
        body, name=name,
        in_specs=[pl.BlockSpec(memory_space=pltpu.VMEM)],
        out_specs=pl.BlockSpec(memory_space=pltpu.VMEM),
        out_shape=jax.ShapeDtypeStruct((N_DEV * m_per, n), block.dtype),
        scratch_shapes=[pltpu.SemaphoreType.DMA((7,)), pltpu.SemaphoreType.DMA((7,)), pltpu.SemaphoreType.DMA],
        compiler_params=pltpu.CompilerParams(vmem_limit_bytes=VMEM_LIMIT_V7X, has_side_effects=True),
    )(block)


def _sum_and_adamw_small(gathered, w, m, v, name):
    rows, n = w.shape
    tr = _tile(rows, 32, SUBLANES)

    def body(p_ref, w_ref, m_ref, v_ref, g_ref, d_ref, m2_ref, v2_ref):
        g = p_ref[0]
        for d in range(1, N_DEV):
            g = g + p_ref[d]
        g_ref[...] = g
        d_ref[...], m2_ref[...], v2_ref[...] = _adamw_math(w_ref[...], g, m_ref[...], v_ref[...])

    blk = pl.BlockSpec((tr, n), lambda i: (i, 0))
    return pl.pallas_call(
        body, name=name, grid=(rows // tr,),
        in_specs=[pl.BlockSpec((N_DEV, tr, n), lambda i: (0, i, 0))] + [blk] * 3,
        out_specs=[blk] * 4,
        out_shape=[jax.ShapeDtypeStruct((rows, n), F32)] * 4,
        compiler_params=_params(("parallel",)),
    )(gathered.reshape(N_DEV, rows, n), w, m, v)


def _pad_rows(a):
    pad = (-a.shape[0]) % SUBLANES
    return jnp.pad(a, ((0, pad), (0, 0))) if pad else a


class _SmallPack:
    def __init__(self, W, D, H, chip):
        self.W, self.D, self.H, self.chip = W, D, H, chip
        self.offsets = {}
        self.rows = 0

    def pack(self, pieces):
        out = []
        self.offsets, self.rows = {}, 0
        for name, a in pieces:
            a = _pad_rows(a.astype(F32))
            self.offsets[name] = (self.rows, a.shape[0])
            self.rows += a.shape[0]
            out.append(a)
        return jnp.concatenate(out, axis=0)

    def piece(self, packed, name):
        start, n = self.offsets[name]
        return packed[start:start + n]


def _bias_rows(b, W):
    bt = jnp.pad(b.T, ((0, 0), (0, LANES - b.shape[0])))
    return bt.reshape(-1, W)


def _bias_from_rows(rows, H):
    return rows.reshape(-1)[:CHUNK * LANES].reshape(CHUNK, LANES)[:, :H].T


def kernel(x, mix_norm_g, w_in, conv_w, spatial_w, spatial_b, conv_out_norm_g, gmlp_out_norm_g, w_out, mlp_norm_g, w_up, w_down, final_norm_g, loss_target, m_mix_norm_g, m_w_in, m_conv_w, m_spatial_w, m_spatial_b, m_conv_out_norm_g, m_gmlp_out_norm_g, m_w_out, m_mlp_norm_g, m_w_up, m_w_down, m_final_norm_g, v_mix_norm_g, v_w_in, v_conv_w, v_spatial_w, v_spatial_b, v_conv_out_norm_g, v_gmlp_out_norm_g, v_w_out, v_mlp_norm_g, v_w_up, v_w_down, v_final_norm_g):
    Bl, S, D = x.shape
    T = Bl * S
    W = conv_out_norm_g.shape[-1]
    H = W // HEAD_DIM
    Wl = conv_w.shape[-1]
    xi, yi, ci = _position()
    chip = (2 * xi + yi).astype(jnp.int32)
    chip_arr = chip.reshape(1)
    chip_core = jnp.stack([chip, ci.astype(jnp.int32)])

    x2 = x.reshape(T, D)
    tgt2 = loss_target.reshape(T, D)

    s_in = _cast_into_slot(w_in[0], chip_arr, "cast_w_in")
    s_out = _cast_into_slot(w_out[0], chip_arr, "cast_w_out")
    s_up = _cast_into_slot(w_up[0], chip_arr, "cast_w_up")
    s_down = _cast_into_slot(w_down[0], chip_arr, "cast_w_down")
    up_third = (s_up.shape[1] // 2) // 3 // 16 * 16

    causal = jnp.tril(jnp.ones((CHUNK, CHUNK), dtype=bool))
    wm = jnp.where(causal[None], spatial_w[0], 0.0).astype(BF16)
    wmt = jnp.swapaxes(wm, 1, 2)
    bias_e = jnp.repeat(spatial_b[0].T, HEAD_DIM, axis=1)
    conv_full = lax.dynamic_update_slice(jnp.zeros((CONV_K, W), F32), conv_w[0], (0, chip * Wl))
    conv_gathered = _all_gather_small(_pad_rows(conv_full), "all_gather_conv_w")
    conv_w_all = conv_gathered.reshape(N_DEV, SUBLANES, W)[:, :CONV_K]
    conv_w_all = conv_w_all[0] + conv_w_all[2] + conv_w_all[4] + conv_w_all[6]
    head_onehot = (jnp.arange(W)[:, None] // HEAD_DIM == jnp.arange(LANES)[None, :]).astype(BF16)
    g_a, g_b = conv_out_norm_g, gmlp_out_norm_g

    xn, ((g_in,),) = _rmsnorm_fwd(x2, mix_norm_g, "mix_norm_fwd", stages=[_GatherRows(s_in)])
    proj, ((g_out,), (g_up,)) = _matmul(xn, g_in, mode="nn", name="proj_fwd", tm=1024, tn=640, tk=4096,
                                        out_dtypes=[F32], b_shard="n",
                                        stages=[_GatherRows(s_out), _GatherRows(s_up, 0, up_third)])
    g_out = g_out.reshape(-1, D)
    y, _ = _mixers_fwd(proj, conv_w_all, wm, bias_e, g_a, g_b, S, "mixers_fwd")
    h1, ((g_up,),) = _matmul(y, g_out, mode="nn", name="out_proj_fwd", tm=1024, tn=512, tk=4096, out_dtypes=[F32],
                             epilogue=_ep_residual, extras=(x2,), stages=[_GatherRows(g_up, up_third, up_third)])
    xn2, ((g_up,),) = _rmsnorm_fwd(h1, mlp_norm_g, "mlp_norm_fwd",
                                   stages=[_GatherRows(g_up, 2 * up_third, s_up.shape[1] // 2 - 2 * up_third)])
    (r, a), ((g_down,),) = _matmul(xn2, g_up, mode="nn", name="up_fwd", tm=1024, tn=512, tk=4096,
                                   out_dtypes=[BF16, BF16], epilogue=_ep_relu2, b_shard="n",
                                   stages=[_GatherRows(s_down)])
    g_down = g_down.reshape(-1, D)
    d2, _ = _matmul(a, g_down, mode="nn", name="down_fwd", tm=2048, tn=1024, tk=1024, out_dtypes=[F32])
    dh2, dh2b, d_final_g, loss_part = _loss_and_final_norm_bwd(h1, d2, tgt2, final_norm_g.reshape(1, D),
                                                               "loss_final_norm")

    def rs_adds(dw, got, tag):
        return _rs_pair_add(dw, got, chip_core, f"rs_pair_add_{tag}")

    dw_down, _ = _matmul(a, dh2b, mode="tn", name="down_dw", tm=1024, tn=512, tk=4096, out_dtypes=[BF16])
    dw_down = dw_down.reshape(N_CHIPS, -1, D)
    dpre, ((got_down,),) = _matmul(dh2b, g_down, mode="nt", name="down_dx", tm=1024, tn=512, tk=4096,
                                   out_dtypes=[BF16], epilogue=_ep_relu2_bwd, extras=(r,),
                                   stages=[_PairExchange(dw_down)])
    part_down, own_down = rs_adds(dw_down, got_down, "w_down")
    dw_up, ((landed_down,),) = _matmul(xn2, dpre, mode="tn", name="up_dw", tm=1024, tn=512, tk=4096,
                                       out_dtypes=[BF16], out_shard=True, stages=[_ChipExchange(part_down)])
    half_down = _rs_final_add(own_down, landed_down, chip_core, "rs_final_add_w_down")
    dxn2, ((got_up,), (grad_down,)) = _matmul(dpre, g_up, mode="nt", name="up_dx", tm=2048, tn=1024, tk=1024,
                                              out_dtypes=[F32], b_shard="k",
                                              stages=[_PairExchange(dw_up), _HalfExchange(half_down)])
    part_up, own_up = rs_adds(dw_up, got_up, "w_up")
    (dh1, dh1b, d_mlp_g), _ = _rmsnorm_bwd(dxn2, h1, mlp_norm_g, dh2, "mlp_norm_bwd")
    q_up = part_up.shape[1] // 4 // 16 * 16
    dw_out, ((landed_up,),) = _matmul(y, dh1b, mode="tn", name="out_proj_dw", tm=1024, tn=512, tk=4096,
                                      out_dtypes=[BF16], stages=[_ChipExchange(part_up, None, 0, q_up)])
    dw_out = dw_out.reshape(N_CHIPS, -1, D)
    dy, ((landed_up,), (got_out,)) = _matmul(dh1b, g_out, mode="nt", name="out_proj_dx", tm=1024, tn=512, tk=4096,
                                             out_dtypes=[F32],
                                             stages=[_ChipExchange(part_up, landed_up, q_up, q_up),
                                                     _PairExchange(dw_out)])
    part_out, own_out = rs_adds(dw_out, got_out, "w_out")
    (dproj, d_conv, d_ga, d_gb, d_ws, d_bt), ((landed_up,),) = _mixers_bwd(
        dy, proj, conv_w_all, wm, wmt, bias_e, g_a, g_b, head_onehot, S, "mixers_bwd",
        stages=[_ChipExchange(part_up, landed_up, 2 * q_up, q_up)])
    dw_in, ((landed_up,), (landed_out,)) = _matmul(
        xn, dproj, mode="tn", name="proj_dw", tm=1024, tn=640, tk=4096, out_dtypes=[BF16], out_shard=True,
        stages=[_ChipExchange(part_up, landed_up, 3 * q_up, part_up.shape[1] - 3 * q_up), _ChipExchange(part_out)])
    half_up = _rs_final_add(own_up, landed_up, chip_core, "rs_final_add_w_up")
    half_out = _rs_final_add(own_out, landed_out, chip_core, "rs_final_add_w_out")
    dxn, ((got_in,), (grad_up,), (grad_out,)) = _matmul(
        dproj, g_in, mode="nt", name="proj_dx", tm=2048, tn=1024, tk=1280, out_dtypes=[F32], b_shard="k",
        stages=[_PairExchange(dw_in), _HalfExchange(half_up), _HalfExchange(half_out)])
    part_in, own_in = rs_adds(dw_in, got_in, "w_in")
    h_in = part_in.shape[1] // 4 // 16 * 16
    (grad_x, _unused, d_mix_g), ((landed_in,),) = _rmsnorm_bwd(dxn, x2, mix_norm_g, dh1, "mix_norm_bwd",
                                                        stages=[_ChipExchange(part_in, None, 0, h_in)])
    upd_down, ((landed_in,),) = _adamw(w_down[0], grad_down, m_w_down[0], v_w_down[0], "adamw_w_down",
                                       stages=[_ChipExchange(part_in, landed_in, h_in, 2 * h_in)])
    upd_up, ((landed_in,),) = _adamw(w_up[0], grad_up, m_w_up[0], v_w_up[0], "adamw_w_up",
                                     stages=[_ChipExchange(part_in, landed_in, 3 * h_in, part_in.shape[1] - 3 * h_in)])
    half_in = _rs_final_add(own_in, landed_in, chip_core, "rs_final_add_w_in")
    upd_out, ((grad_in,),) = _adamw(w_out[0], grad_out, m_w_out[0], v_w_out[0], "adamw_w_out",
                                    stages=[_HalfExchange(half_in)])
    upd_in, _ = _adamw(w_in[0], grad_in, m_w_in[0], v_w_in[0], "adamw_w_in")
    big = {"w_down": [grad_down, *upd_down], "w_up": [grad_up, *upd_up], "w_out": [grad_out, *upd_out],
           "w_in": [grad_in, *upd_in]}
    big = {k: [t[None] for t in v] for k, v in big.items()}

    pack = _SmallPack(W, D, H, chip)
    causal_f = causal.astype(F32)
    loss_row = jnp.pad(loss_part[:, :1], ((0, 0), (0, W - 1)))

    def small(mix, conv, sw, sb, ga, gb, mlp, fin, extra):
        return pack.pack([("spatial_w", sw.reshape(-1, W)), ("conv_w", conv), ("mix_norm_g", mix.reshape(-1, W)),
                          ("mlp_norm_g", mlp.reshape(-1, W)), ("final_norm_g", fin.reshape(-1, W)),
                          ("conv_out_norm_g", ga.reshape(-1, W)), ("gmlp_out_norm_g", gb.reshape(-1, W)),
                          ("spatial_b", _bias_rows(sb, W)), ("loss", extra)])

    def full_conv(cw):
        return lax.dynamic_update_slice(jnp.zeros((CONV_K, W), F32), cw[0], (0, chip * Wl))

    zero_row = jnp.zeros((1, W), F32)
    g_part = pack.pack([("spatial_w", (d_ws * causal_f[None]).reshape(-1, W)), ("conv_w", d_conv),
                        ("mix_norm_g", d_mix_g.reshape(-1, W)), ("mlp_norm_g", d_mlp_g.reshape(-1, W)),
                        ("final_norm_g", d_final_g.reshape(-1, W)), ("conv_out_norm_g", d_ga), ("gmlp_out_norm_g", d_gb),
                        ("spatial_b", d_bt.reshape(-1, W)), ("loss", loss_row)])
    w_s = small(mix_norm_g, full_conv(conv_w), spatial_w, spatial_b[0], conv_out_norm_g, gmlp_out_norm_g, mlp_norm_g,
                final_norm_g, zero_row)
    m_s = small(m_mix_norm_g, full_conv(m_conv_w), m_spatial_w, m_spatial_b[0], m_conv_out_norm_g, m_gmlp_out_norm_g,
                m_mlp_norm_g, m_final_norm_g, zero_row)
    v_s = small(v_mix_norm_g, full_conv(v_conv_w), v_spatial_w, v_spatial_b[0], v_conv_out_norm_g, v_gmlp_out_norm_g,
                v_mlp_norm_g, v_final_norm_g, zero_row)
    g_all = _all_gather_small(g_part, "all_gather_small_grads")
    small_outs = _sum_and_adamw_small(g_all, w_s, m_s, v_s, "sum_adamw_small")

    def unpack(packed, name):
        rows = pack.piece(packed, name)
        if name == "spatial_w":
            return rows.reshape(1, H, CHUNK, CHUNK)
        if name == "conv_w":
            return lax.dynamic_slice(rows[:CONV_K], (0, chip * Wl), (CONV_K, Wl))[None]
        if name == "spatial_b":
            return _bias_from_rows(rows, H)[None]
        if name == "final_norm_g":
            return rows.reshape(-1)[:D]
        n = D if name in ("mix_norm_g", "mlp_norm_g") else W
        return rows.reshape(-1)[:n].reshape(1, n)

    loss = pack.piece(small_outs[0], "loss")[0, 0]
    order = ["mix_norm_g", "w_in", "conv_w", "spatial_w", "spatial_b", "conv_out_norm_g", "gmlp_out_norm_g", "w_out",
             "mlp_norm_g", "w_up", "w_down", "final_norm_g"]
    outs = [loss, grad_x.reshape(Bl, S, D)]
    for kind in range(4):
        for name in order:
            outs.append(big[name][kind] if name in big else unpack(small_outs[kind], name))
    return tuple(outs)
```

```python
import functools
import math

import jax
import jax.numpy as jnp
from jax import lax
from jax.experimental import pallas as pl
from jax.experimental.pallas import tpu as pltpu

F32 = jnp.float32
BF16 = jnp.bfloat16
MESH = pl.DeviceIdType.MESH

NORM_EPS = 1e-5
HEAD_DIM = 128
CHUNK = 128
CONV_K = 3
N_CHIPS = 4
N_DEV = 8

ADAM_LR = 0.001
ADAM_B1 = 0.9
ADAM_B2 = 0.999
ADAM_EPS = 1e-08
ADAM_WD = 0.01
ADAM_STEP = 10
ADAM_C1 = 1.0 - ADAM_B1 ** ADAM_STEP
ADAM_C2 = 1.0 - ADAM_B2 ** ADAM_STEP

GELU_K = math.sqrt(2.0 / math.pi)
GELU_A = 0.044715

VMEM_LIMIT_V7X = 56 * 1024 * 1024
SUBLANES = 8
LANES = 128


def _tile(dim, target, mult=LANES):
    if dim <= target:
        return dim
    t = (target // mult) * mult
    while t > mult and dim % t:
        t -= mult
    assert dim % t == 0, (dim, target, mult)
    return t


def _params(sem=None):
    return pltpu.CompilerParams(dimension_semantics=sem, vmem_limit_bytes=VMEM_LIMIT_V7X)


class _Stage:
    bufs = ()
    n_sems = 0
    MIDDLE_AT = 0.6
    base = 0

    def start(self, refs, send, recv):
        raise NotImplementedError

    def middle(self, refs, send, recv):
        pass

    def finish(self, refs, send, recv):
        raise NotImplementedError


def _position():
    return lax.axis_index("x"), lax.axis_index("y"), lax.axis_index("c")


def _other_chips(x, y):
    return [(1 - x, y), (x, 1 - y), (1 - x, 1 - y)]


def _remote(src, dst, send, recv, k, to):
    return pltpu.make_async_remote_copy(src_ref=src, dst_ref=dst, send_sem=send.at[k], recv_sem=recv.at[k],
                                        device_id=to, device_id_type=MESH)


def _call(body, *, name, args, in_specs, out_specs, out_shape, grid=(), scratch_shapes=(), semantics=None, stages=()):
    n_in, n_out, n_scratch = len(args), len(out_shape), len(scratch_shapes)
    any_spec = pl.BlockSpec(memory_space=pl.ANY)
    extra_args, extra_out, aliases, layout = [], [], {}, []
    for st in stages:
        where = []
        for kind, buf in st.bufs:
            if kind in ("in", "alias"):
                extra_args.append(buf)
                pos_in = n_in + len(extra_args) - 1
            if kind in ("out", "alias"):
                extra_out.append(jax.ShapeDtypeStruct(buf.shape, buf.dtype))
                pos_out = n_out + len(extra_out) - 1
            if kind == "alias":
                aliases[pos_in] = pos_out
            where.append(("in", pos_in) if kind == "in" else ("out", pos_out))
        layout.append(where)
    n_sems = sum(st.n_sems for st in stages)
    n_xin, n_xout = len(extra_args), len(extra_out)

    def wrapped(*refs):
        ins = refs[:n_in + n_xin]
        outs = refs[n_in + n_xin:n_in + n_xin + n_out + n_xout]
        scratch = refs[n_in + n_xin + n_out + n_xout:]
        main = ins[:n_in] + outs[:n_out] + scratch[:n_scratch]
        if not stages:
            body(*main)
            return
        send, recv = scratch[n_scratch], scratch[n_scratch + 1]
        step, n_steps = 0, 1
        for d, g in enumerate(grid):
            step = step * g + pl.program_id(d)
            n_steps *= g
        base, views = 0, []
        for st, where in zip(stages, layout):
            st_refs = [ins[p] if side == "in" else outs[p] for side, p in where]
            st.base = base
            views.append((st, st_refs, send, recv))
            base += st.n_sems

        def starts():
            for st, r, s, v in views:
                st.start(r, s, v)

        def middles():
            for st, r, s, v in views:
                st.middle(r, s, v)

        def finishes():
            for st, r, s, v in views:
                st.finish(r, s, v)

        if not grid:
            starts()
            body(*main)
            middles()
            finishes()
        else:
            pl.when(step == 0)(starts)
            body(*main)
            pl.when(step == min(int(n_steps * _Stage.MIDDLE_AT), n_steps - 1))(middles)
            pl.when(step == n_steps - 1)(finishes)

    sems = [pltpu.SemaphoreType.DMA((n_sems,)), pltpu.SemaphoreType.DMA((n_sems,))] if stages else []
    kw = dict(grid=grid) if grid else {}
    res = pl.pallas_call(
        wrapped, name=name,
        in_specs=list(in_specs) + [any_spec] * n_xin,
        out_specs=list(out_specs) + [any_spec] * n_xout,
        out_shape=list(out_shape) + extra_out,
        scratch_shapes=list(scratch_shapes) + sems,
        input_output_aliases=aliases,
        compiler_params=pltpu.CompilerParams(
            dimension_semantics=("arbitrary",) * len(grid) if stages and grid else semantics,
            vmem_limit_bytes=VMEM_LIMIT_V7X, has_side_effects=bool(stages)),
        **kw,
    )(*args, *extra_args)
    main_res, stage_res, pos = list(res[:n_out]), [], n_out
    for st in stages:
        k = sum(kind in ("out", "alias") for kind, _ in st.bufs)
        stage_res.append(list(res[pos:pos + k]))
        pos += k
    return main_res, stage_res


def _run_stages(stages, name):
    return _call(lambda: None, name=name, args=[], in_specs=[], out_specs=[], out_shape=[], stages=stages)[1]


class _GatherRows(_Stage):
    n_sems = 7

    def __init__(self, g, lo=0, n=None):
        self.hr = g.shape[1] // 2
        self.lo, self.n = lo, (self.hr if n is None else n)
        self.n0 = self.n // 2 // 16 * 16
        self.bufs = [("alias", g)]

    def _copy(self, g_ref, send, recv, k, chip_xy, half, to, lo=0, n=None):
        n = self.n if n is None else n
        blk = g_ref.at[2 * chip_xy[0] + chip_xy[1], pl.ds(half * self.hr + self.lo + lo, n), :]
        return _remote(blk, blk, send, recv, self.base + k, to)

    def _plan(self, g, send, recv):
        x, y, c = _position()
        me, sib = (x, y, c), (x, y, 1 - c)
        cx, cy, cd = _other_chips(x, y)
        n0, n1 = self.n0, self.n - self.n0
        cp = functools.partial(self._copy, g, send, recv)
        mine = [cp(0, (x, y), c, (*cx, c)), cp(1, (x, y), c, (*cy, c)), cp(2, cx, c, sib), cp(3, cy, c, sib),
                cp(4, cx, c, (*cy, c), 0, n0), cp(5, cy, c, (*cx, c), n0, n1), cp(6, cd, c, sib)]
        theirs = [cp(0, cx, c, me), cp(1, cy, c, me), cp(2, cx, 1 - c, me), cp(3, cy, 1 - c, me),
                  cp(4, cd, c, me, 0, n0), cp(5, cd, c, me, n0, n1), cp(6, cd, 1 - c, me)]
        return mine, theirs

    def start(self, refs, send, recv):
        mine, _ = self._plan(refs[0], send, recv)
        mine[0].start()
        mine[1].start()

    def middle(self, refs, send, recv):
        mine, theirs = self._plan(refs[0], send, recv)
        theirs[0].wait_recv()
        mine[4].start()
        mine[2].start()
        theirs[1].wait_recv()
        mine[5].start()
        mine[3].start()

    def finish(self, refs, send, recv):
        mine, theirs = self._plan(refs[0], send, recv)
        theirs[4].wait_recv()
        theirs[5].wait_recv()
        mine[6].start()
        for k in (2, 3, 6):
            theirs[k].wait_recv()
        for cp in mine:
            cp.wait_send()


class _PairExchange(_Stage):
    n_sems = 1

    def __init__(self, dw):
        S, R, C = dw.shape
        self.hr = R // 2
        self.bufs = [("in", dw), ("out", jax.ShapeDtypeStruct((S, self.hr, C), dw.dtype))]

    def _copy(self, refs, send, recv):
        x, y, c = _position()
        return _remote(refs[0].at[:, pl.ds((1 - c) * self.hr, self.hr), :], refs[1], send, recv, self.base,
                       (x, y, 1 - c))

    def start(self, refs, send, recv):
        self._copy(refs, send, recv).start()

    def finish(self, refs, send, recv):
        cp = self._copy(refs, send, recv)
        cp.wait_recv()
        cp.wait_send()


class _ChipExchange(_Stage):
    n_sems = 3

    def __init__(self, part, landed=None, lo=0, n=None):
        S, hr, C = part.shape
        self.lo, self.n = lo, (hr if n is None else n)
        self.bufs = [("in", part), ("out", jax.ShapeDtypeStruct((3, hr, C), part.dtype)) if landed is None
                     else ("alias", landed)]

    def _copies(self, refs, send, recv):
        x, y, c = _position()
        rows = pl.ds(self.lo, self.n)
        return [_remote(refs[0].at[2 * chip[0] + chip[1], rows, :], refs[1].at[j, rows, :], send, recv,
                        self.base + j, (*chip, c))
                for j, chip in enumerate(_other_chips(x, y))]

    def start(self, refs, send, recv):
        for cp in self._copies(refs, send, recv):
            cp.start()

    def finish(self, refs, send, recv):
        copies = self._copies(refs, send, recv)
        for cp in copies:
            cp.wait_recv()
        for cp in copies:
            cp.wait_send()


class _HalfExchange(_Stage):
    n_sems = 1

    def __init__(self, grad):
        self.hr = grad.shape[0] // 2
        self.bufs = [("alias", grad)]

    def start(self, refs, send, recv):
        x, y, c = _position()
        mine = refs[0].at[pl.ds(c * self.hr, self.hr), :]
        _remote(mine, mine, send, recv, self.base, (x, y, 1 - c)).start()

    def finish(self, refs, send, recv):
        x, y, c = _position()
        mine = refs[0].at[pl.ds(c * self.hr, self.hr), :]
        theirs = refs[0].at[pl.ds((1 - c) * self.hr, self.hr), :]
        _remote(theirs, theirs, send, recv, self.base, (x, y, 1 - c)).wait_recv()
        _remote(mine, mine, send, recv, self.base, (x, y, 1 - c)).wait_send()


def _matmul(a, b, *, mode, name, tm, tn, tk, out_dtypes, epilogue=None, extras=(), b_shard=None, out_shard=False,
            stages=()):
    if mode == "tn":
        K, M = a.shape
    else:
        M, K = a.shape
    if b_shard == "n":
        S, Kb, Ns = b.shape
        N = S * Ns
    elif b_shard == "k":
        S, N, Ks = b.shape
        Kb = S * Ks
    elif mode == "nt":
        N, Kb = b.shape
    else:
        Kb, N = b.shape
    assert Kb == K, (name, a.shape, b.shape)
    tm, tn, tk = _tile(M, tm), _tile(N, tn), _tile(K, tk)
    if b_shard == "n" or out_shard:
        n_per = N // N_CHIPS
        tn = _tile(n_per, tn)
        njs = n_per // tn
    if b_shard == "k":
        tk = _tile(K // N_CHIPS, tk)
        nks = (K // N_CHIPS) // tk
    gm, gn, gk = M // tm, N // tn, K // tk

    if mode == "tn":
        a_spec = pl.BlockSpec((tk, tm), lambda i, j, k: (k, i))
        dims = (((0,), (0,)), ((), ()))
    else:
        a_spec = pl.BlockSpec((tm, tk), lambda i, j, k: (i, k))
        dims = (((1,), (1,)), ((), ())) if mode == "nt" else (((1,), (0,)), ((), ()))
    if b_shard == "n":
        b_spec = pl.BlockSpec((None, tk, tn), lambda i, j, k: (j // njs, k, j % njs))
    elif b_shard == "k":
        b_spec = pl.BlockSpec((None, tn, tk), lambda i, j, k: (k // nks, j, k % nks))
    elif mode == "nt":
        b_spec = pl.BlockSpec((tn, tk), lambda i, j, k: (j, k))
    else:
        b_spec = pl.BlockSpec((tk, tn), lambda i, j, k: (k, j))
    mn_spec = pl.BlockSpec((tm, tn), lambda i, j, k: (i, j))
    if out_shard:
        out_spec = pl.BlockSpec((None, tm, tn), lambda i, j, k: (j // njs, i, j % njs))
        out_shape = [jax.ShapeDtypeStruct((N_CHIPS, M, N // N_CHIPS), dt) for dt in out_dtypes]
    else:
        out_spec = mn_spec
        out_shape = [jax.ShapeDtypeStruct((M, N), dt) for dt in out_dtypes]
    n_extra, n_out = len(extras), len(out_dtypes)

    def finish_tile(acc, extra_refs, out_refs):
        if epilogue is None:
            for o in out_refs:
                o[...] = acc.astype(o.dtype)
        else:
            epilogue(acc, extra_refs, out_refs)

    def body(*refs):
        a_ref, b_ref = refs[0], refs[1]
        extra_refs = refs[2:2 + n_extra]
        out_refs = refs[2 + n_extra:2 + n_extra + n_out]

        def product():
            return lax.dot_general(a_ref[...], b_ref[...], dims, preferred_element_type=F32)

        if gk == 1:
            finish_tile(product(), extra_refs, out_refs)
            return
        acc_ref = refs[-1]
        k = pl.program_id(2)

        @pl.when(k == 0)
        def _():
            acc_ref[...] = product()

        @pl.when((k > 0) & (k < gk - 1))
        def _():
            acc_ref[...] += product()

        @pl.when(k == gk - 1)
        def _():
            finish_tile(acc_ref[...] + product(), extra_refs, out_refs)

    outs, carried = _call(
        body, name=name, args=[a, b, *extras], grid=(gm, gn, gk),
        in_specs=[a_spec, b_spec] + [mn_spec] * n_extra, out_specs=[out_spec] * n_out, out_shape=out_shape,
        scratch_shapes=[pltpu.VMEM((tm, tn), F32)] if gk > 1 else [],
        semantics=("parallel", "parallel", "arbitrary"), stages=stages)
    return (outs[0] if n_out == 1 else outs), carried


def _ep_residual(acc, extra_refs, out_refs):
    out_refs[0][...] = extra_refs[0][...] + acc


def _ep_relu2(acc, extra_refs, out_refs):
    r = jnp.maximum(acc, 0.0)
    out_refs[0][...] = r.astype(BF16)
    out_refs[1][...] = (r * r).astype(BF16)


def _ep_relu2_bwd(acc, extra_refs, out_refs):
    out_refs[0][...] = (acc * (2.0 * extra_refs[0][...].astype(F32))).astype(BF16)


def _row_inv(x):
    return lax.rsqrt(jnp.mean(x * x, axis=-1, keepdims=True) + NORM_EPS)


def _rmsnorm_fwd(x, g, name, stages=()):
    T, D = x.shape
    tt = _tile(T, 256, SUBLANES)

    def body(x_ref, g_ref, o_ref):
        xv = x_ref[...]
        o_ref[...] = (xv * _row_inv(xv) * g_ref[...]).astype(BF16)

    outs, carried = _call(
        body, name=name, args=[x, g], grid=(T // tt,),
        in_specs=[pl.BlockSpec((tt, D), lambda i: (i, 0)), pl.BlockSpec((1, D), lambda i: (0, 0))],
        out_specs=[pl.BlockSpec((tt, D), lambda i: (i, 0))], out_shape=[jax.ShapeDtypeStruct((T, D), BF16)],
        semantics=("parallel",), stages=stages)
    return outs[0], carried


def _rmsnorm_bwd(dxn, h, g, dres, name, stages=()):
    T, D = h.shape
    tt = _tile(T, 128, SUBLANES)

    def body(dxn_ref, h_ref, g_ref, dres_ref, dh_ref, dhb_ref, dg_ref):
        @pl.when(pl.program_id(0) == 0)
        def _():
            dg_ref[...] = jnp.zeros_like(dg_ref)

        hv = h_ref[...]
        inv = _row_inv(hv)
        n = hv * inv
        d = dxn_ref[...]
        dg_ref[...] += jnp.sum(d * n, axis=0, keepdims=True)
        dn = d * g_ref[...]
        dh = dres_ref[...] + inv * (dn - n * jnp.mean(dn * n, axis=-1, keepdims=True))
        dh_ref[...] = dh
        dhb_ref[...] = dh.astype(BF16)

    row = pl.BlockSpec((tt, D), lambda i: (i, 0))
    vec = pl.BlockSpec((1, D), lambda i: (0, 0))
    return _call(
        body, name=name, args=[dxn, h, g, dres], grid=(T // tt,), in_specs=[row, row, vec, row],
        out_specs=[row, row, vec],
        out_shape=[jax.ShapeDtypeStruct((T, D), F32), jax.ShapeDtypeStruct((T, D), BF16),
                   jax.ShapeDtypeStruct((1, D), F32)],
        semantics=("arbitrary",), stages=stages)


def _loss_and_final_norm_bwd(h1, d2, tgt, g, name):
    T, D = h1.shape
    tt = _tile(T, 128, SUBLANES)

    def body(h1_ref, d2_ref, t_ref, g_ref, dh_ref, dhb_ref, dg_ref, loss_ref):
        @pl.when(pl.program_id(0) == 0)
        def _():
            dg_ref[...] = jnp.zeros_like(dg_ref)
            loss_ref[...] = jnp.zeros_like(loss_ref)

        hv = h1_ref[...] + d2_ref[...]
        gv = g_ref[...]
        inv = _row_inv(hv)
        n = hv * inv
        err = n * gv - t_ref[...]
        loss_ref[...] += 0.5 * jnp.sum(jnp.mean(err * err, axis=-1, keepdims=True))
        dy = err * (1.0 / D)
        dg_ref[...] += jnp.sum(dy * n, axis=0, keepdims=True)
        dn = dy * gv
        dh = inv * (dn - n * jnp.mean(dn * n, axis=-1, keepdims=True))
        dh_ref[...] = dh
        dhb_ref[...] = dh.astype(BF16)

    row = pl.BlockSpec((tt, D), lambda i: (i, 0))
    vec = pl.BlockSpec((1, D), lambda i: (0, 0))
    one = pl.BlockSpec((1, LANES), lambda i: (0, 0))
    return _call(
        body, name=name, args=[h1, d2, tgt, g], grid=(T // tt,), in_specs=[row, row, row, vec],
        out_specs=[row, row, vec, one],
        out_shape=[jax.ShapeDtypeStruct((T, D), F32), jax.ShapeDtypeStruct((T, D), BF16),
                   jax.ShapeDtypeStruct((1, D), F32), jax.ShapeDtypeStruct((1, LANES), F32)],
        semantics=("arbitrary",))[0]


def _gelu(x):
    th = jnp.tanh(GELU_K * (x + GELU_A * (x * x * x)))
    return 0.5 * x * (1.0 + th), th


def _gelu_grad(x, th):
    return 0.5 * (1.0 + th) + 0.5 * x * (1.0 - th * th) * (GELU_K * (1.0 + 3.0 * GELU_A * (x * x)))


def _shift_rows(cur, prev_rows, k):
    rolled = pltpu.roll(cur, k, 0)
    row = lax.broadcasted_iota(jnp.int32, cur.shape, 0)
    out = rolled
    for r in range(k):
        out = jnp.where(row == r, prev_rows[SUBLANES - k + r:SUBLANES - k + r + 1, :], out)
    return out


def _unshift_rows(cur, next_rows, k):
    n = cur.shape[0]
    rolled = pltpu.roll(cur, n - k, 0)
    row = lax.broadcasted_iota(jnp.int32, cur.shape, 0)
    out = rolled
    for r in range(k):
        out = jnp.where(row == n - k + r, next_rows[r:r + 1, :], out)
    return out


def _mixer_specs(W, blk, halo):
    cols = [pl.BlockSpec((CHUNK, W), functools.partial(lambda i, col: (blk(i), col), col=col)) for col in range(5)]
    halos = [pl.BlockSpec((SUBLANES, W), functools.partial(lambda i, col: (halo(i), col), col=col)) for col in (1, 2)]
    return cols, halos


def _mixers_fwd(proj, conv_w, wm, bias_e, g_a, g_b, seq_len, name, stages=()):
    T, W5 = proj.shape
    W = W5 // 5
    H = W // HEAD_DIM
    per_seq = seq_len // CHUNK
    rb = CHUNK // SUBLANES
    cols, halos = _mixer_specs(W, lambda i: i, lambda i: jnp.maximum(i * rb - 1, 0))

    def body(b_ref, c_ref, hin_ref, u_ref, v_ref, ch_ref, hh_ref, cw_ref, wm_ref, be_ref, ga_ref, gb_ref, y_ref, s_ref):
        first = (pl.program_id(0) % per_seq) == 0
        hc = c_ref[...] * hin_ref[...]
        hc_prev = jnp.where(first, 0.0, ch_ref[...] * hh_ref[...])
        cw = cw_ref[...]
        ya = b_ref[...] * (cw[0:1, :] * _shift_rows(hc, hc_prev, 2) + cw[1:2, :] * _shift_rows(hc, hc_prev, 1)
                           + cw[2:3, :] * hc)
        y_ref[:, 0:W] = (ya * _row_inv(ya) * ga_ref[...]).astype(BF16)
        gu, _ = _gelu(u_ref[...])
        gv, _ = _gelu(v_ref[...])
        gvb = gv.astype(BF16)
        for hd in range(H):
            sl = slice(hd * HEAD_DIM, (hd + 1) * HEAD_DIM)
            s_ref[:, sl] = jnp.dot(wm_ref[hd], gvb[:, sl], preferred_element_type=F32)
        yb = gu * (s_ref[...] + be_ref[...])
        y_ref[:, W:2 * W] = (yb * _row_inv(yb) * gb_ref[...]).astype(BF16)

    full = lambda shape: pl.BlockSpec(shape, lambda i: (0,) * len(shape))
    outs, carried = _call(
        body, name=name, args=[proj, proj, proj, proj, proj, proj, proj, conv_w, wm, bias_e, g_a, g_b],
        grid=(T // CHUNK,),
        in_specs=cols + halos + [full((CONV_K, W)), full((H, CHUNK, CHUNK)), full((CHUNK, W)), full((1, W)), full((1, W))],
        out_specs=[pl.BlockSpec((CHUNK, 2 * W), lambda i: (i, 0))], out_shape=[jax.ShapeDtypeStruct((T, 2 * W), BF16)],
        scratch_shapes=[pltpu.VMEM((CHUNK, W), F32)], semantics=("parallel",), stages=stages)
    return outs[0], carried


def _mixers_bwd(dy, proj, conv_w, wm, wmt, bias_e, g_a, g_b, head_onehot, seq_len, name, stages=()):
    T, W5 = proj.shape
    W = W5 // 5
    H = W // HEAD_DIM
    nb = T // CHUNK
    per_seq = seq_len // CHUNK
    rb = CHUNK // SUBLANES
    blk = lambda i: nb - 1 - i
    cols, halos = _mixer_specs(W, blk, lambda i: jnp.maximum(blk(i) * rb - 1, 0))

    def body(dy_ref, b_ref, c_ref, hin_ref, u_ref, v_ref, ch_ref, hh_ref, cw_ref, wm_ref, wmt_ref, be_ref, ga_ref,
             gb_ref, oh_ref, dp_ref, dcw_ref, dga_ref, dgb_ref, dws_ref, dbt_ref, carry_ref, s_ref, dgv_ref):
        i = pl.program_id(0)
        j = nb - 1 - i

        @pl.when(i == 0)
        def _():
            for r in (dcw_ref, dga_ref, dgb_ref, dws_ref, dbt_ref, carry_ref):
                r[...] = jnp.zeros_like(r)

        first = (j % per_seq) == 0
        last = (j % per_seq) == per_seq - 1
        b, c, hin = b_ref[...], c_ref[...], hin_ref[...]
        cw = cw_ref[...]
        hc = c * hin
        hc_prev = jnp.where(first, 0.0, ch_ref[...] * hh_ref[...])
        hc1 = _shift_rows(hc, hc_prev, 1)
        hc2 = _shift_rows(hc, hc_prev, 2)
        conv = cw[0:1, :] * hc2 + cw[1:2, :] * hc1 + cw[2:3, :] * hc
        ya = b * conv
        inv_a = _row_inv(ya)
        na = ya * inv_a
        do_a = dy_ref[:, 0:W]
        dga_ref[...] += jnp.sum(do_a * na, axis=0, keepdims=True)
        dna = do_a * ga_ref[...]
        dya = inv_a * (dna - na * jnp.mean(dna * na, axis=-1, keepdims=True))
        dp_ref[:, 0:W] = (dya * conv).astype(BF16)
        dconv = dya * b
        dcw_ref[0:1, :] += jnp.sum(dconv * hc2, axis=0, keepdims=True)
        dcw_ref[1:2, :] += jnp.sum(dconv * hc1, axis=0, keepdims=True)
        dcw_ref[2:3, :] += jnp.sum(dconv * hc, axis=0, keepdims=True)
        nxt = jnp.where(last, 0.0, carry_ref[...])
        dhc = cw[2:3, :] * dconv + cw[1:2, :] * _unshift_rows(dconv, nxt, 1) + cw[0:1, :] * _unshift_rows(dconv, nxt, 2)
        carry_ref[...] = dconv[0:SUBLANES, :]
        dp_ref[:, W:2 * W] = (dhc * hin).astype(BF16)
        dp_ref[:, 2 * W:3 * W] = (dhc * c).astype(BF16)
        u, v = u_ref[...], v_ref[...]
        gu, thu = _gelu(u)
        gv, thv = _gelu(v)
        gvb = gv.astype(BF16)
        for hd in range(H):
            sl = slice(hd * HEAD_DIM, (hd + 1) * HEAD_DIM)
            s_ref[:, sl] = jnp.dot(wm_ref[hd], gvb[:, sl], preferred_element_type=F32)
        s = s_ref[...] + be_ref[...]
        yb = gu * s
        inv_b = _row_inv(yb)
        nbv = yb * inv_b
        do_b = dy_ref[:, W:2 * W]
        dgb_ref[...] += jnp.sum(do_b * nbv, axis=0, keepdims=True)
        dnb = do_b * gb_ref[...]
        dyb = inv_b * (dnb - nbv * jnp.mean(dnb * nbv, axis=-1, keepdims=True))
        dp_ref[:, 3 * W:4 * W] = (dyb * s * _gelu_grad(u, thu)).astype(BF16)
        dsb = (dyb * gu).astype(BF16)
        dbt_ref[...] += jnp.dot(dsb, oh_ref[...], preferred_element_type=F32)
        for hd in range(H):
            sl = slice(hd * HEAD_DIM, (hd + 1) * HEAD_DIM)
            dws_ref[hd] += lax.dot_general(dsb[:, sl], gvb[:, sl], (((1,), (1,)), ((), ())), preferred_element_type=F32)
            dgv_ref[:, sl] = jnp.dot(wmt_ref[hd], dsb[:, sl], preferred_element_type=F32)
        dp_ref[:, 4 * W:5 * W] = (dgv_ref[...] * _gelu_grad(v, thv)).astype(BF16)

    full = lambda shape: pl.BlockSpec(shape, lambda i: (0,) * len(shape))
    return _call(
        body, name=name, grid=(nb,),
        args=[dy, proj, proj, proj, proj, proj, proj, proj, conv_w, wm, wmt, bias_e, g_a, g_b, head_onehot],
        in_specs=[pl.BlockSpec((CHUNK, 2 * W), lambda i: (blk(i), 0))] + cols + halos
        + [full((CONV_K, W)), full((H, CHUNK, CHUNK)), full((H, CHUNK, CHUNK)), full((CHUNK, W)), full((1, W)),
           full((1, W)), full((W, LANES))],
        out_specs=[pl.BlockSpec((CHUNK, 5 * W), lambda i: (blk(i), 0)), full((SUBLANES, W)), full((1, W)), full((1, W)),
                   full((H, CHUNK, CHUNK)), full((CHUNK, LANES))],
        out_shape=[jax.ShapeDtypeStruct((T, 5 * W), BF16), jax.ShapeDtypeStruct((SUBLANES, W), F32),
                   jax.ShapeDtypeStruct((1, W), F32), jax.ShapeDtypeStruct((1, W), F32),
                   jax.ShapeDtypeStruct((H, CHUNK, CHUNK), F32), jax.ShapeDtypeStruct((CHUNK, LANES), F32)],
        scratch_shapes=[pltpu.VMEM((SUBLANES, W), F32), pltpu.VMEM((CHUNK, W), F32), pltpu.VMEM((CHUNK, W), F32)],
        semantics=("arbitrary",), stages=stages)


def _cast_into_slot(w, chip, name):
    R, C = w.shape
    tr = _tile(R, 256, 16)

    def body(chip_ref, w_ref, o_ref):
        o_ref[...] = w_ref[...].astype(BF16)

    return pl.pallas_call(
        body, name=name,
        grid_spec=pltpu.PrefetchScalarGridSpec(
            num_scalar_prefetch=1, grid=(R // tr,),
            in_specs=[pl.BlockSpec((tr, C), lambda i, chip_ref: (i, 0))],
            out_specs=pl.BlockSpec((None, tr, C), lambda i, chip_ref: (chip_ref[0], i, 0))),
        out_shape=jax.ShapeDtypeStruct((N_CHIPS, R, C), BF16),
        compiler_params=_params(("parallel",)),
    )(chip, w)


def _rs_pair_add(dw, got, chip_core, name):
    S, R, C = dw.shape
    hr = R // 2
    tr = _tile(hr, 256, 16)
    nrb = hr // tr

    def body(cc_ref, dw_ref, got_ref, send_ref, own_ref):
        s = dw_ref[...].astype(F32) + got_ref[...].astype(F32)
        send_ref[...] = s.astype(BF16)

        @pl.when(pl.program_id(1) == cc_ref[0])
        def _():
            own_ref[...] = s

    return pl.pallas_call(
        body, name=name,
        grid_spec=pltpu.PrefetchScalarGridSpec(
            num_scalar_prefetch=1, grid=(nrb, S),
            in_specs=[pl.BlockSpec((None, tr, C), lambda i, q, cc: (q, cc[1] * nrb + i, 0)),
                      pl.BlockSpec((None, tr, C), lambda i, q, cc: (q, i, 0))],
            out_specs=[pl.BlockSpec((None, tr, C), lambda i, q, cc: (q, i, 0)),
                       pl.BlockSpec((tr, C), lambda i, q, cc: (i, 0))]),
        out_shape=[jax.ShapeDtypeStruct((S, hr, C), BF16), jax.ShapeDtypeStruct((hr, C), F32)],
        compiler_params=_params(("parallel", "arbitrary")),
    )(chip_core, dw, got)


def _rs_final_add(own, got, chip_core, name):
    hr, C = own.shape
    tr = _tile(hr, 256, 16)
    nrb = hr // tr

    def body(cc_ref, own_ref, got_ref, o_ref):
        o_ref[...] = ((own_ref[...] + got_ref[0].astype(F32)) + got_ref[1].astype(F32)) + got_ref[2].astype(F32)

    return pl.pallas_call(
        body, name=name,
        grid_spec=pltpu.PrefetchScalarGridSpec(
            num_scalar_prefetch=1, grid=(nrb,),
            in_specs=[pl.BlockSpec((tr, C), lambda i, cc: (i, 0)), pl.BlockSpec((3, tr, C), lambda i, cc: (0, i, 0))],
            out_specs=pl.BlockSpec((tr, C), lambda i, cc: (cc[1] * nrb + i, 0))),
        out_shape=jax.ShapeDtypeStruct((2 * hr, C), F32),
        compiler_params=_params(("parallel",)),
    )(chip_core, own, got)


def _adamw_math(w, g, m, v):
    m2 = ADAM_B1 * m + (1.0 - ADAM_B1) * g
    v2 = ADAM_B2 * v + (1.0 - ADAM_B2) * (g * g)
    delta = -ADAM_LR * ((m2 / ADAM_C1) / (jnp.sqrt(v2 / ADAM_C2) + ADAM_EPS) + ADAM_WD * w)
    return delta, m2, v2


def _adamw(w, g, m, v, name):
    R, C = w.shape
    tr = _tile(R, max(SUBLANES, (256 * 1024) // C), SUBLANES)

    def body(w_ref, g_ref, m_ref, v_ref, g2_ref, d_ref, m2_ref, v2_ref):
        g = g_ref[...]
        g2_ref[...] = g
        d_ref[...], m2_ref[...], v2_ref[...] = _adamw_math(w_ref[...], g, m_ref[...], v_ref[...])

    blk = pl.BlockSpec((tr, C), lambda i: (i, 0))
    return _call(body, name=name, args=[w, g, m, v], grid=(R // tr,), in_specs=[blk] * 4, out_specs=[blk] * 4,
                 out_shape=[jax.ShapeDtypeStruct((R, C), F32)] * 4, semantics=("parallel",))[0]


def _all_gather_small(block, name):
    m_per, n = block.shape

    def body(x_ref, out_ref, send_sems, recv_sems, local_sem):
        x, y, c = _position()
        me, sibling = (x, y, c), (x, y, 1 - c)
        chips = _other_chips(x, y)

        def rows(px, py, pc):
            return out_ref.at[pl.ds((4 * px + 2 * py + pc) * m_per, m_per), :]

        def copy(k, blk, to, src=None):
            return pltpu.make_async_remote_copy(src_ref=rows(*blk) if src is None else src, dst_ref=rows(*blk),
                                                send_sem=send_sems.at[k], recv_sem=recv_sems.at[k], device_id=to,
                                                device_id_type=MESH)

        mine = pltpu.make_async_copy(x_ref, rows(*me), local_sem)
        mine.start()
        first = [copy(0, me, sibling, src=x_ref)]
        first += [copy(1 + j, me, (*chip, c), src=x_ref) for j, chip in enumerate(chips)]
        for cp in first:
            cp.start()
        passed = [copy(4 + j, (*chip, c), sibling) for j, chip in enumerate(chips)]
        for j, chip in enumerate(chips):
            copy(1 + j, (*chip, c), me).wait_recv()
            passed[j].start()
        copy(0, sibling, me).wait_recv()
        for j, chip in enumerate(chips):
            copy(4 + j, (*chip, 1 - c), me).wait_recv()
        for cp in first + passed:
            cp.wait_send()
        mine.wait()

    return pl.pallas_call(
        body, name=name,
        in_specs=[pl.BlockSpec(memory_space=pltpu.VMEM)],
        out_specs=pl.BlockSpec(memory_space=pltpu.VMEM),
        out_shape=jax.ShapeDtypeStruct((N_DEV * m_per, n), block.dtype),
        scratch_shapes=[pltpu.SemaphoreType.DMA((7,)), pltpu.SemaphoreType.DMA((7,)), pltpu.SemaphoreType.DMA],
        compiler_params=pltpu.CompilerParams(vmem_limit_bytes=VMEM_LIMIT_V7X, has_side_effects=True),
    )(block)


def _sum_and_adamw_small(gathered, w, m, v, name):
    rows, n = w.shape
    tr = _tile(rows, 32, SUBLANES)

    def body(p_ref, w_ref, m_ref, v_ref, g_ref, d_ref, m2_ref, v2_ref):
        g = p_ref[0]
        for d in range(1, N_DEV):
            g = g + p_ref[d]
        g_ref[...] = g
        d_ref[...], m2_ref[...], v2_ref[...] = _adamw_math(w_ref[...], g, m_ref[...], v_ref[...])

    blk = pl.BlockSpec((tr, n), lambda i: (i, 0))
    return pl.pallas_call(
        body, name=name, grid=(rows // tr,),
        in_specs=[pl.BlockSpec((N_DEV, tr, n), lambda i: (0, i, 0))] + [blk] * 3,
        out_specs=[blk] * 4,
        out_shape=[jax.ShapeDtypeStruct((rows, n), F32)] * 4,
        compiler_params=_params(("parallel",)),
    )(gathered.reshape(N_DEV, rows, n), w, m, v)


def _pad_rows(a):
    pad = (-a.shape[0]) % SUBLANES
    return jnp.pad(a, ((0, pad), (0, 0))) if pad else a


class _SmallPack:
    def __init__(self, W, D, H, chip):
        self.W, self.D, self.H, self.chip = W, D, H, chip
        self.offsets = {}
        self.rows = 0

    def pack(self, pieces):
        out = []
        self.offsets, self.rows = {}, 0
        for name, a in pieces:
            a = _pad_rows(a.astype(F32))
            self.offsets[name] = (self.rows, a.shape[0])
            self.rows += a.shape[0]
            out.append(a)
        return jnp.concatenate(out, axis=0)

    def piece(self, packed, name):
        start, n = self.offsets[name]
        return packed[start:start + n]


def _bias_rows(b, W):
    bt = jnp.pad(b.T, ((0, 0), (0, LANES - b.shape[0])))
    return bt.reshape(-1, W)


def _bias_from_rows(rows, H):
    return rows.reshape(-1)[:CHUNK * LANES].reshape(CHUNK, LANES)[:, :H].T


def kernel(x, mix_norm_g, w_in, conv_w, spatial_w, spatial_b, conv_out_norm_g, gmlp_out_norm_g, w_out, mlp_norm_g, w_up, w_down, final_norm_g, loss_target, m_mix_norm_g, m_w_in, m_conv_w, m_spatial_w, m_spatial_b, m_conv_out_norm_g, m_gmlp_out_norm_g, m_w_out, m_mlp_norm_g, m_w_up, m_w_down, m_final_norm_g, v_mix_norm_g, v_w_in, v_conv_w, v_spatial_w, v_spatial_b, v_conv_out_norm_g, v_gmlp_out_norm_g, v_w_out, v_mlp_norm_g, v_w_up, v_w_down, v_final_norm_g):
    Bl, S, D = x.shape
    T = Bl * S
    W = conv_out_norm_g.shape[-1]
    H = W // HEAD_DIM
    Wl = conv_w.shape[-1]
    xi, yi, ci = _position()
    chip = (2 * xi + yi).astype(jnp.int32)
    chip_arr = chip.reshape(1)
    chip_core = jnp.stack([chip, ci.astype(jnp.int32)])

    x2 = x.reshape(T, D)
    tgt2 = loss_target.reshape(T, D)

    s_in = _cast_into_slot(w_in[0], chip_arr, "cast_w_in")
    s_out = _cast_into_slot(w_out[0], chip_arr, "cast_w_out")
    s_up = _cast_into_slot(w_up[0], chip_arr, "cast_w_up")
    s_down = _cast_into_slot(w_down[0], chip_arr, "cast_w_down")
    up_rows = s_up.shape[1] // 2
    up_a, up_b = up_rows // 2 // 16 * 16, up_rows * 7 // 20 // 16 * 16

    causal = jnp.tril(jnp.ones((CHUNK, CHUNK), dtype=bool))
    wm = jnp.where(causal[None], spatial_w[0], 0.0).astype(BF16)
    wmt = jnp.swapaxes(wm, 1, 2)
    bias_e = jnp.repeat(spatial_b[0].T, HEAD_DIM, axis=1)
    conv_full = lax.dynamic_update_slice(jnp.zeros((CONV_K, W), F32), conv_w[0], (0, chip * Wl))
    conv_gathered = _all_gather_small(_pad_rows(conv_full), "all_gather_conv_w")
    conv_w_all = conv_gathered.reshape(N_DEV, SUBLANES, W)[:, :CONV_K]
    conv_w_all = conv_w_all[0] + conv_w_all[2] + conv_w_all[4] + conv_w_all[6]
    head_onehot = (jnp.arange(W)[:, None] // HEAD_DIM == jnp.arange(LANES)[None, :]).astype(BF16)
    g_a, g_b = conv_out_norm_g, gmlp_out_norm_g

    xn, ((g_in,),) = _rmsnorm_fwd(x2, mix_norm_g, "mix_norm_fwd", stages=[_GatherRows(s_in)])
    proj, ((g_out,), (g_up,)) = _matmul(xn, g_in, mode="nn", name="proj_fwd", tm=1024, tn=640, tk=4096,
                                        out_dtypes=[F32], b_shard="n",
                                        stages=[_GatherRows(s_out), _GatherRows(s_up, 0, up_a)])
    g_out = g_out.reshape(-1, D)
    y, _ = _mixers_fwd(proj, conv_w_all, wm, bias_e, g_a, g_b, S, "mixers_fwd")
    h1, ((g_up,),) = _matmul(y, g_out, mode="nn", name="out_proj_fwd", tm=1024, tn=512, tk=4096, out_dtypes=[F32],
                             epilogue=_ep_residual, extras=(x2,), stages=[_GatherRows(g_up, up_a, up_b)])
    xn2, ((g_up,),) = _rmsnorm_fwd(h1, mlp_norm_g, "mlp_norm_fwd",
                                   stages=[_GatherRows(g_up, up_a + up_b, up_rows - up_a - up_b)])
    (r, a), ((g_down,),) = _matmul(xn2, g_up, mode="nn", name="up_fwd", tm=1024, tn=512, tk=4096,
                                   out_dtypes=[BF16, BF16], epilogue=_ep_relu2, b_shard="n",
                                   stages=[_GatherRows(s_down)])
    g_down = g_down.reshape(-1, D)
    d2, _ = _matmul(a, g_down, mode="nn", name="down_fwd", tm=2048, tn=1024, tk=1024, out_dtypes=[F32])
    dh2, dh2b, d_final_g, loss_part = _loss_and_final_norm_bwd(h1, d2, tgt2, final_norm_g.reshape(1, D),
                                                               "loss_final_norm")

    def rs_adds(dw, got, tag):
        return _rs_pair_add(dw, got, chip_core, f"rs_pair_add_{tag}")

    dw_down, _ = _matmul(a, dh2b, mode="tn", name="down_dw", tm=1024, tn=512, tk=4096, out_dtypes=[BF16])
    dw_down = dw_down.reshape(N_CHIPS, -1, D)
    dpre, ((got_down,),) = _matmul(dh2b, g_down, mode="nt", name="down_dx", tm=1024, tn=512, tk=4096,
                                   out_dtypes=[BF16], epilogue=_ep_relu2_bwd, extras=(r,),
                                   stages=[_PairExchange(dw_down)])
    part_down, own_down = rs_adds(dw_down, got_down, "w_down")
    rows_down = part_down.shape[1]
    down_a = rows_down * 3 // 4 // 16 * 16
    dw_up, ((landed_down,),) = _matmul(xn2, dpre, mode="tn", name="up_dw", tm=1024, tn=512, tk=4096,
                                       out_dtypes=[BF16], out_shard=True,
                                       stages=[_ChipExchange(part_down, None, 0, down_a)])
    dxn2, ((got_up,), (landed_down,)) = _matmul(
        dpre, g_up, mode="nt", name="up_dx", tm=2048, tn=1024, tk=1024, out_dtypes=[F32], b_shard="k",
        stages=[_PairExchange(dw_up), _ChipExchange(part_down, landed_down, down_a, rows_down - down_a)])
    half_down = _rs_final_add(own_down, landed_down, chip_core, "rs_final_add_w_down")
    part_up, own_up = rs_adds(dw_up, got_up, "w_up")
    (dh1, dh1b, d_mlp_g), ((grad_down,),) = _rmsnorm_bwd(dxn2, h1, mlp_norm_g, dh2, "mlp_norm_bwd",
                                                        stages=[_HalfExchange(half_down)])
    rows_up = part_up.shape[1]
    q_up = rows_up // 4 // 16 * 16
    dy, ((landed_up,),) = _matmul(dh1b, g_out, mode="nt", name="out_proj_dx", tm=1024, tn=512, tk=4096,
                                  out_dtypes=[F32], stages=[_ChipExchange(part_up, None, 0, q_up)])
    (dproj, d_conv, d_ga, d_gb, d_ws, d_bt), ((landed_up,),) = _mixers_bwd(
        dy, proj, conv_w_all, wm, wmt, bias_e, g_a, g_b, head_onehot, S, "mixers_bwd",
        stages=[_ChipExchange(part_up, landed_up, q_up, q_up)])
    dw_in, ((landed_up,),) = _matmul(
        xn, dproj, mode="tn", name="proj_dw", tm=1024, tn=640, tk=4096, out_dtypes=[BF16], out_shard=True,
        stages=[_ChipExchange(part_up, landed_up, 2 * q_up, rows_up - 2 * q_up)])
    half_up = _rs_final_add(own_up, landed_up, chip_core, "rs_final_add_w_up")
    dw_out, ((got_in,), (grad_up,)) = _matmul(y, dh1b, mode="tn", name="out_proj_dw", tm=1024, tn=512, tk=4096,
                                              out_dtypes=[BF16],
                                              stages=[_PairExchange(dw_in), _HalfExchange(half_up)])
    dw_out = dw_out.reshape(N_CHIPS, -1, D)
    part_in, own_in = rs_adds(dw_in, got_in, "w_in")
    dxn, ((landed_in,), (got_out,)) = _matmul(
        dproj, g_in, mode="nt", name="proj_dx", tm=2048, tn=1024, tk=1280, out_dtypes=[F32], b_shard="k",
        stages=[_ChipExchange(part_in), _PairExchange(dw_out)])
    part_out, own_out = rs_adds(dw_out, got_out, "w_out")
    half_in = _rs_final_add(own_in, landed_in, chip_core, "rs_final_add_w_in")
    (grad_x, _unused, d_mix_g), ((landed_out,), (grad_in,)) = _rmsnorm_bwd(
        dxn, x2, mix_norm_g, dh1, "mix_norm_bwd", stages=[_ChipExchange(part_out), _HalfExchange(half_in)])
    half_out = _rs_final_add(own_out, landed_out, chip_core, "rs_final_add_w_out")
    ((grad_out,),) = _run_stages([_HalfExchange(half_out)], "rs_half_exchange_w_out")
    big = {"w_down": _adamw(w_down[0], grad_down, m_w_down[0], v_w_down[0], "adamw_w_down"),
           "w_up": _adamw(w_up[0], grad_up, m_w_up[0], v_w_up[0], "adamw_w_up"),
           "w_in": _adamw(w_in[0], grad_in, m_w_in[0], v_w_in[0], "adamw_w_in"),
           "w_out": _adamw(w_out[0], grad_out, m_w_out[0], v_w_out[0], "adamw_w_out")}
    big = {k: [t[None] for t in v] for k, v in big.items()}

    pack = _SmallPack(W, D, H, chip)
    causal_f = causal.astype(F32)
    loss_row = jnp.pad(loss_part[:, :1], ((0, 0), (0, W - 1)))

    def small(mix, conv, sw, sb, ga, gb, mlp, fin, extra):
        return pack.pack([("spatial_w", sw.reshape(-1, W)), ("conv_w", conv), ("mix_norm_g", mix.reshape(-1, W)),
                          ("mlp_norm_g", mlp.reshape(-1, W)), ("final_norm_g", fin.reshape(-1, W)),
                          ("conv_out_norm_g", ga.reshape(-1, W)), ("gmlp_out_norm_g", gb.reshape(-1, W)),
                          ("spatial_b", _bias_rows(sb, W)), ("loss", extra)])

    def full_conv(cw):
        return lax.dynamic_update_slice(jnp.zeros((CONV_K, W), F32), cw[0], (0, chip * Wl))

    zero_row = jnp.zeros((1, W), F32)
    g_part = pack.pack([("spatial_w", (d_ws * causal_f[None]).reshape(-1, W)), ("conv_w", d_conv),
                        ("mix_norm_g", d_mix_g.reshape(-1, W)), ("mlp_norm_g", d_mlp_g.reshape(-1, W)),
                        ("final_norm_g", d_final_g.reshape(-1, W)), ("conv_out_norm_g", d_ga), ("gmlp_out_norm_g", d_gb),
                        ("spatial_b", d_bt.reshape(-1, W)), ("loss", loss_row)])
    w_s = small(mix_norm_g, full_conv(conv_w), spatial_w, spatial_b[0], conv_out_norm_g, gmlp_out_norm_g, mlp_norm_g,
                final_norm_g, zero_row)
    m_s = small(m_mix_norm_g, full_conv(m_conv_w), m_spatial_w, m_spatial_b[0], m_conv_out_norm_g, m_gmlp_out_norm_g,
                m_mlp_norm_g, m_final_norm_g, zero_row)
    v_s = small(v_mix_norm_g, full_conv(v_conv_w), v_spatial_w, v_spatial_b[0], v_conv_out_norm_g, v_gmlp_out_norm_g,
                v_mlp_norm_g, v_final_norm_g, zero_row)
    g_all = _all_gather_small(g_part, "all_gather_small_grads")
    small_outs = _sum_and_adamw_small(g_all, w_s, m_s, v_s, "sum_adamw_small")

    def unpack(packed, name):
        rows = pack.piece(packed, name)
        if name == "spatial_w":
            return rows.reshape(1, H, CHUNK, CHUNK)
        if name == "conv_w":
            return lax.dynamic_slice(rows[:CONV_K], (0, chip * Wl), (CONV_K, Wl))[None]
        if name == "spatial_b":
            return _bias_from_rows(rows, H)[None]
        if name == "final_norm_g":
            return rows.reshape(-1)[:D]
        n = D if name in ("mix_norm_g", "mlp_norm_g") else W
        return rows.reshape(-1)[:n].reshape(1, n)

    loss = pack.piece(small_outs[0], "loss")[0, 0]
    order = ["mix_norm_g", "w_in", "conv_w", "spatial_w", "spatial_b", "conv_out_norm_g", "gmlp_out_norm_g", "w_out",
             "mlp_norm_g", "w_up", "w_down", "final_norm_g"]
    outs = [loss, grad_x.reshape(Bl, S, D)]
    for kind in range(4):
        for name in order:
            outs.append(big[name][kind] if name in big else unpack(small_outs[kind], name))
    return tuple(outs)
```

```python
import functools
import math

import jax
import jax.numpy as jnp
from jax import lax
from jax.experimental import pallas as pl
from jax.experimental.pallas import tpu as pltpu

F32 = jnp.float32
BF16 = jnp.bfloat16
MESH = pl.DeviceIdType.MESH

NORM_EPS = 1e-5
HEAD_DIM = 128
CHUNK = 128
CONV_K = 3
N_CHIPS = 4
N_DEV = 8

ADAM_LR = 0.001
ADAM_B1 = 0.9
ADAM_B2 = 0.999
ADAM_EPS = 1e-08
ADAM_WD = 0.01
ADAM_STEP = 10
ADAM_C1 = 1.0 - ADAM_B1 ** ADAM_STEP
ADAM_C2 = 1.0 - ADAM_B2 ** ADAM_STEP

GELU_K = math.sqrt(2.0 / math.pi)
GELU_A = 0.044715

VMEM_LIMIT_V7X = 56 * 1024 * 1024
SUBLANES = 8
LANES = 128


def _tile(dim, target, mult=LANES):
    if dim <= target:
        return dim
    t = (target // mult) * mult
    while t > mult and dim % t:
        t -= mult
    assert dim % t == 0, (dim, target, mult)
    return t


def _params(sem=None):
    return pltpu.CompilerParams(dimension_semantics=sem, vmem_limit_bytes=VMEM_LIMIT_V7X)


class _Stage:
    bufs = ()
    n_sems = 0
    MIDDLE_AT = 0.6
    base = 0

    def start(self, refs, send, recv):
        raise NotImplementedError

    def middle(self, refs, send, recv):
        pass

    def finish(self, refs, send, recv):
        raise NotImplementedError


def _position():
    return lax.axis_index("x"), lax.axis_index("y"), lax.axis_index("c")


def _other_chips(x, y):
    return [(1 - x, y), (x, 1 - y), (1 - x, 1 - y)]


def _remote(src, dst, send, recv, k, to):
    return pltpu.make_async_remote_copy(src_ref=src, dst_ref=dst, send_sem=send.at[k], recv_sem=recv.at[k],
                                        device_id=to, device_id_type=MESH)


def _call(body, *, name, args, in_specs, out_specs, out_shape, grid=(), scratch_shapes=(), semantics=None, stages=()):
    n_in, n_out, n_scratch = len(args), len(out_shape), len(scratch_shapes)
    any_spec = pl.BlockSpec(memory_space=pl.ANY)
    extra_args, extra_out, aliases, layout = [], [], {}, []
    for st in stages:
        where = []
        for kind, buf in st.bufs:
            if kind in ("in", "alias"):
                extra_args.append(buf)
                pos_in = n_in + len(extra_args) - 1
            if kind in ("out", "alias"):
                extra_out.append(jax.ShapeDtypeStruct(buf.shape, buf.dtype))
                pos_out = n_out + len(extra_out) - 1
            if kind == "alias":
                aliases[pos_in] = pos_out
            where.append(("in", pos_in) if kind == "in" else ("out", pos_out))
        layout.append(where)
    n_sems = sum(st.n_sems for st in stages)
    n_xin, n_xout = len(extra_args), len(extra_out)

    def wrapped(*refs):
        ins = refs[:n_in + n_xin]
        outs = refs[n_in + n_xin:n_in + n_xin + n_out + n_xout]
        scratch = refs[n_in + n_xin + n_out + n_xout:]
        main = ins[:n_in] + outs[:n_out] + scratch[:n_scratch]
        if not stages:
            body(*main)
            return
        send, recv = scratch[n_scratch], scratch[n_scratch + 1]
        step, n_steps = 0, 1
        for d, g in enumerate(grid):
            step = step * g + pl.program_id(d)
            n_steps *= g
        base, views = 0, []
        for st, where in zip(stages, layout):
            st_refs = [ins[p] if side == "in" else outs[p] for side, p in where]
            st.base = base
            views.append((st, st_refs, send, recv))
            base += st.n_sems

        def starts():
            for st, r, s, v in views:
                st.start(r, s, v)

        def middles():
            for st, r, s, v in views:
                st.middle(r, s, v)

        def finishes():
            for st, r, s, v in views:
                st.finish(r, s, v)

        if not grid:
            starts()
            body(*main)
            middles()
            finishes()
        else:
            pl.when(step == 0)(starts)
            body(*main)
            pl.when(step == min(int(n_steps * _Stage.MIDDLE_AT), n_steps - 1))(middles)
            pl.when(step == n_steps - 1)(finishes)

    sems = [pltpu.SemaphoreType.DMA((n_sems,)), pltpu.SemaphoreType.DMA((n_sems,))] if stages else []
    kw = dict(grid=grid) if grid else {}
    res = pl.pallas_call(
        wrapped, name=name,
        in_specs=list(in_specs) + [any_spec] * n_xin,
        out_specs=list(out_specs) + [any_spec] * n_xout,
        out_shape=list(out_shape) + extra_out,
        scratch_shapes=list(scratch_shapes) + sems,
        input_output_aliases=aliases,
        compiler_params=pltpu.CompilerParams(
            dimension_semantics=("arbitrary",) * len(grid) if stages and grid else semantics,
            vmem_limit_bytes=VMEM_LIMIT_V7X, has_side_effects=bool(stages)),
        **kw,
    )(*args, *extra_args)
    main_res, stage_res, pos = list(res[:n_out]), [], n_out
    for st in stages:
        k = sum(kind in ("out", "alias") for kind, _ in st.bufs)
        stage_res.append(list(res[pos:pos + k]))
        pos += k
    return main_res, stage_res


def _run_stages(stages, name):
    return _call(lambda: None, name=name, args=[], in_specs=[], out_specs=[], out_shape=[], stages=stages)[1]


class _GatherRows(_Stage):
    n_sems = 7

    def __init__(self, g, lo=0, n=None):
        self.hr = g.shape[1] // 2
        self.lo, self.n = lo, (self.hr if n is None else n)
        self.n0 = self.n // 2 // 16 * 16
        self.bufs = [("alias", g)]

    def _copy(self, g_ref, send, recv, k, chip_xy, half, to, lo=0, n=None):
        n = self.n if n is None else n
        blk = g_ref.at[2 * chip_xy[0] + chip_xy[1], pl.ds(half * self.hr + self.lo + lo, n), :]
        return _remote(blk, blk, send, recv, self.base + k, to)

    def _plan(self, g, send, recv):
        x, y, c = _position()
        me, sib = (x, y, c), (x, y, 1 - c)
        cx, cy, cd = _other_chips(x, y)
        n0, n1 = self.n0, self.n - self.n0
        mine = [((x, y), c, (*cx, c)), ((x, y), c, (*cy, c)), (cx, c, sib), (cy, c, sib),
                (cx, c, (*cy, c), 0, n0), (cy, c, (*cx, c), n0, n1), (cd, c, sib)]
        theirs = [(cx, c, me), (cy, c, me), (cx, 1 - c, me), (cy, 1 - c, me),
                  (cd, c, me, 0, n0), (cd, c, me, n0, n1), (cd, 1 - c, me)]
        return (lambda k: self._copy(g, send, recv, k, *mine[k])), (lambda k: self._copy(g, send, recv, k, *theirs[k]))

    def start(self, refs, send, recv):
        mine, _ = self._plan(refs[0], send, recv)
        mine(0).start()
        mine(1).start()

    def middle(self, refs, send, recv):
        mine, theirs = self._plan(refs[0], send, recv)
        theirs(0).wait_recv()
        mine(4).start()
        mine(2).start()
        theirs(1).wait_recv()
        mine(5).start()
        mine(3).start()

    def finish(self, refs, send, recv):
        mine, theirs = self._plan(refs[0], send, recv)
        theirs(4).wait_recv()
        theirs(5).wait_recv()
        mine(6).start()
        for k in (2, 3, 6):
            theirs(k).wait_recv()
        for k in range(self.n_sems):
            mine(k).wait_send()


class _PairExchange(_Stage):
    n_sems = 1

    def __init__(self, dw):
        S, R, C = dw.shape
        self.hr = R // 2
        self.bufs = [("in", dw), ("out", jax.ShapeDtypeStruct((S, self.hr, C), dw.dtype))]

    def _copy(self, refs, send, recv):
        x, y, c = _position()
        return _remote(refs[0].at[:, pl.ds((1 - c) * self.hr, self.hr), :], refs[1], send, recv, self.base,
                       (x, y, 1 - c))

    def start(self, refs, send, recv):
        self._copy(refs, send, recv).start()

    def finish(self, refs, send, recv):
        cp = self._copy(refs, send, recv)
        cp.wait_recv()
        cp.wait_send()


class _ChipExchange(_Stage):
    n_sems = 3

    def __init__(self, part, landed=None, lo=0, n=None):
        S, hr, C = part.shape
        self.lo, self.n = lo, (hr if n is None else n)
        self.bufs = [("in", part), ("out", jax.ShapeDtypeStruct((3, hr, C), part.dtype)) if landed is None
                     else ("alias", landed)]

    def _copies(self, refs, send, recv):
        x, y, c = _position()
        rows = pl.ds(self.lo, self.n)
        return [_remote(refs[0].at[2 * chip[0] + chip[1], rows, :], refs[1].at[j, rows, :], send, recv,
                        self.base + j, (*chip, c))
                for j, chip in enumerate(_other_chips(x, y))]

    def start(self, refs, send, recv):
        for cp in self._copies(refs, send, recv):
            cp.start()

    def finish(self, refs, send, recv):
        copies = self._copies(refs, send, recv)
        for cp in copies:
            cp.wait_recv()
        for cp in copies:
            cp.wait_send()


class _HalfExchange(_Stage):
    n_sems = 1

    def __init__(self, grad):
        self.hr = grad.shape[0] // 2
        self.bufs = [("alias", grad)]

    def start(self, refs, send, recv):
        x, y, c = _position()
        mine = refs[0].at[pl.ds(c * self.hr, self.hr), :]
        _remote(mine, mine, send, recv, self.base, (x, y, 1 - c)).start()

    def finish(self, refs, send, recv):
        x, y, c = _position()
        mine = refs[0].at[pl.ds(c * self.hr, self.hr), :]
        theirs = refs[0].at[pl.ds((1 - c) * self.hr, self.hr), :]
        _remote(theirs, theirs, send, recv, self.base, (x, y, 1 - c)).wait_recv()
        _remote(mine, mine, send, recv, self.base, (x, y, 1 - c)).wait_send()


def _matmul(a, b, *, mode, name, tm, tn, tk, out_dtypes, epilogue=None, extras=(), b_shard=None, out_shard=False,
            stages=(), n_sub=1):
    if mode == "tn":
        K, M = a.shape
    else:
        M, K = a.shape
    if b_shard == "n":
        S, Kb, Ns = b.shape
        N = S * Ns
    elif b_shard == "k":
        S, N, Ks = b.shape
        Kb = S * Ks
    elif mode == "nt":
        N, Kb = b.shape
    else:
        Kb, N = b.shape
    assert Kb == K, (name, a.shape, b.shape)
    tm, tn, tk = _tile(M, tm), _tile(N, tn), _tile(K, tk)
    if b_shard == "n" or out_shard:
        n_per = N // N_CHIPS
        tn = _tile(n_per, tn)
        njs = n_per // tn
    if b_shard == "k":
        tk = _tile(K // N_CHIPS, tk)
        nks = (K // N_CHIPS) // tk
    gm, gn, gk = M // tm, N // tn, K // tk
    if gk > 1 or tn % (n_sub * LANES):
        n_sub = 1

    if mode == "tn":
        a_spec = pl.BlockSpec((tk, tm), lambda i, j, k: (k, i))
        dims = (((0,), (0,)), ((), ()))
    else:
        a_spec = pl.BlockSpec((tm, tk), lambda i, j, k: (i, k))
        dims = (((1,), (1,)), ((), ())) if mode == "nt" else (((1,), (0,)), ((), ()))
    if b_shard == "n":
        b_spec = pl.BlockSpec((None, tk, tn), lambda i, j, k: (j // njs, k, j % njs))
    elif b_shard == "k":
        b_spec = pl.BlockSpec((None, tn, tk), lambda i, j, k: (k // nks, j, k % nks))
    elif mode == "nt":
        b_spec = pl.BlockSpec((tn, tk), lambda i, j, k: (j, k))
    else:
        b_spec = pl.BlockSpec((tk, tn), lambda i, j, k: (k, j))
    mn_spec = pl.BlockSpec((tm, tn), lambda i, j, k: (i, j))
    if out_shard:
        out_spec = pl.BlockSpec((None, tm, tn), lambda i, j, k: (j // njs, i, j % njs))
        out_shape = [jax.ShapeDtypeStruct((N_CHIPS, M, N // N_CHIPS), dt) for dt in out_dtypes]
    else:
        out_spec = mn_spec
        out_shape = [jax.ShapeDtypeStruct((M, N), dt) for dt in out_dtypes]
    n_extra, n_out = len(extras), len(out_dtypes)

    def finish_tile(acc, extra_refs, out_refs):
        if epilogue is None:
            for o in out_refs:
                o[...] = acc.astype(o.dtype)
        else:
            epilogue(acc, extra_refs, out_refs)

    def body(*refs):
        a_ref, b_ref = refs[0], refs[1]
        extra_refs = refs[2:2 + n_extra]
        out_refs = refs[2 + n_extra:2 + n_extra + n_out]

        def product():
            return lax.dot_general(a_ref[...], b_ref[...], dims, preferred_element_type=F32)

        if gk == 1:
            sub = tn // n_sub
            for h in range(n_sub):
                cols = slice(h * sub, (h + 1) * sub)
                b_part = b_ref[cols, :] if mode == "nt" else b_ref[:, cols]
                acc = lax.dot_general(a_ref[...], b_part, dims, preferred_element_type=F32)
                finish_tile(acc, [e.at[:, cols] for e in extra_refs], [o.at[:, cols] for o in out_refs])
            return
        acc_ref = refs[-1]
        k = pl.program_id(2)

        @pl.when(k == 0)
        def _():
            acc_ref[...] = product()

        @pl.when((k > 0) & (k < gk - 1))
        def _():
            acc_ref[...] += product()

        @pl.when(k == gk - 1)
        def _():
            finish_tile(acc_ref[...] + product(), extra_refs, out_refs)

    outs, carried = _call(
        body, name=name, args=[a, b, *extras], grid=(gm, gn, gk),
        in_specs=[a_spec, b_spec] + [mn_spec] * n_extra, out_specs=[out_spec] * n_out, out_shape=out_shape,
        scratch_shapes=[pltpu.VMEM((tm, tn), F32)] if gk > 1 else [],
        semantics=("parallel", "parallel", "arbitrary"), stages=stages)
    return (outs[0] if n_out == 1 else outs), carried


def _ep_residual(acc, extra_refs, out_refs):
    out_refs[0][...] = extra_refs[0][...] + acc


def _ep_relu2(acc, extra_refs, out_refs):
    r = jnp.maximum(acc, 0.0)
    out_refs[0][...] = r.astype(BF16)
    out_refs[1][...] = (r * r).astype(BF16)


def _ep_relu2_bwd(acc, extra_refs, out_refs):
    out_refs[0][...] = (acc * (2.0 * extra_refs[0][...].astype(F32))).astype(BF16)


def _row_inv(x):
    return lax.rsqrt(jnp.mean(x * x, axis=-1, keepdims=True) + NORM_EPS)


def _rmsnorm_fwd(x, g, name, stages=()):
    T, D = x.shape
    tt = _tile(T, 256, SUBLANES)

    def body(x_ref, g_ref, o_ref):
        xv = x_ref[...]
        o_ref[...] = (xv * _row_inv(xv) * g_ref[...]).astype(BF16)

    outs, carried = _call(
        body, name=name, args=[x, g], grid=(T // tt,),
        in_specs=[pl.BlockSpec((tt, D), lambda i: (i, 0)), pl.BlockSpec((1, D), lambda i: (0, 0))],
        out_specs=[pl.BlockSpec((tt, D), lambda i: (i, 0))], out_shape=[jax.ShapeDtypeStruct((T, D), BF16)],
        semantics=("parallel",), stages=stages)
    return outs[0], carried


def _rmsnorm_bwd(dxn, h, g, dres, name, stages=()):
    T, D = h.shape
    tt = _tile(T, 128, SUBLANES)

    def body(dxn_ref, h_ref, g_ref, dres_ref, dh_ref, dhb_ref, dg_ref):
        @pl.when(pl.program_id(0) == 0)
        def _():
            dg_ref[...] = jnp.zeros_like(dg_ref)

        hv = h_ref[...]
        inv = _row_inv(hv)
        n = hv * inv
        d = dxn_ref[...]
        dg_ref[...] += jnp.sum(d * n, axis=0, keepdims=True)
        dn = d * g_ref[...]
        dh = dres_ref[...] + inv * (dn - n * jnp.mean(dn * n, axis=-1, keepdims=True))
        dh_ref[...] = dh
        dhb_ref[...] = dh.astype(BF16)

    row = pl.BlockSpec((tt, D), lambda i: (i, 0))
    vec = pl.BlockSpec((1, D), lambda i: (0, 0))
    return _call(
        body, name=name, args=[dxn, h, g, dres], grid=(T // tt,), in_specs=[row, row, vec, row],
        out_specs=[row, row, vec],
        out_shape=[jax.ShapeDtypeStruct((T, D), F32), jax.ShapeDtypeStruct((T, D), BF16),
                   jax.ShapeDtypeStruct((1, D), F32)],
        semantics=("arbitrary",), stages=stages)


def _loss_and_final_norm_bwd(h1, d2, tgt, g, name):
    T, D = h1.shape
    tt = _tile(T, 128, SUBLANES)

    def body(h1_ref, d2_ref, t_ref, g_ref, dh_ref, dhb_ref, dg_ref, loss_ref):
        @pl.when(pl.program_id(0) == 0)
        def _():
            dg_ref[...] = jnp.zeros_like(dg_ref)
            loss_ref[...] = jnp.zeros_like(loss_ref)

        hv = h1_ref[...] + d2_ref[...]
        gv = g_ref[...]
        inv = _row_inv(hv)
        n = hv * inv
        err = n * gv - t_ref[...]
        loss_ref[...] += 0.5 * jnp.sum(jnp.mean(err * err, axis=-1, keepdims=True))
        dy = err * (1.0 / D)
        dg_ref[...] += jnp.sum(dy * n, axis=0, keepdims=True)
        dn = dy * gv
        dh = inv * (dn - n * jnp.mean(dn * n, axis=-1, keepdims=True))
        dh_ref[...] = dh
        dhb_ref[...] = dh.astype(BF16)

    row = pl.BlockSpec((tt, D), lambda i: (i, 0))
    vec = pl.BlockSpec((1, D), lambda i: (0, 0))
    one = pl.BlockSpec((1, LANES), lambda i: (0, 0))
    return _call(
        body, name=name, args=[h1, d2, tgt, g], grid=(T // tt,), in_specs=[row, row, row, vec],
        out_specs=[row, row, vec, one],
        out_shape=[jax.ShapeDtypeStruct((T, D), F32), jax.ShapeDtypeStruct((T, D), BF16),
                   jax.ShapeDtypeStruct((1, D), F32), jax.ShapeDtypeStruct((1, LANES), F32)],
        semantics=("arbitrary",))[0]


def _gelu(x):
    th = jnp.tanh(GELU_K * (x + GELU_A * (x * x * x)))
    return 0.5 * x * (1.0 + th), th


def _gelu_grad(x, th):
    return 0.5 * (1.0 + th) + 0.5 * x * (1.0 - th * th) * (GELU_K * (1.0 + 3.0 * GELU_A * (x * x)))


def _shift_rows(cur, prev_rows, k):
    rolled = pltpu.roll(cur, k, 0)
    row = lax.broadcasted_iota(jnp.int32, cur.shape, 0)
    out = rolled
    for r in range(k):
        out = jnp.where(row == r, prev_rows[SUBLANES - k + r:SUBLANES - k + r + 1, :], out)
    return out


def _unshift_rows(cur, next_rows, k):
    n = cur.shape[0]
    rolled = pltpu.roll(cur, n - k, 0)
    row = lax.broadcasted_iota(jnp.int32, cur.shape, 0)
    out = rolled
    for r in range(k):
        out = jnp.where(row == n - k + r, next_rows[r:r + 1, :], out)
    return out


def _mixer_specs(W, blk, halo):
    cols = [pl.BlockSpec((CHUNK, W), functools.partial(lambda i, col: (blk(i), col), col=col)) for col in range(5)]
    halos = [pl.BlockSpec((SUBLANES, W), functools.partial(lambda i, col: (halo(i), col), col=col)) for col in (1, 2)]
    return cols, halos


def _mixers_fwd(proj, conv_w, wm, bias_e, g_a, g_b, seq_len, name, stages=()):
    T, W5 = proj.shape
    W = W5 // 5
    H = W // HEAD_DIM
    per_seq = seq_len // CHUNK
    rb = CHUNK // SUBLANES
    cols, halos = _mixer_specs(W, lambda i: i, lambda i: jnp.maximum(i * rb - 1, 0))

    def body(b_ref, c_ref, hin_ref, u_ref, v_ref, ch_ref, hh_ref, cw_ref, wm_ref, be_ref, ga_ref, gb_ref, y_ref, s_ref):
        first = (pl.program_id(0) % per_seq) == 0
        hc = c_ref[...] * hin_ref[...]
        hc_prev = jnp.where(first, 0.0, ch_ref[...] * hh_ref[...])
        cw = cw_ref[...]
        ya = b_ref[...] * (cw[0:1, :] * _shift_rows(hc, hc_prev, 2) + cw[1:2, :] * _shift_rows(hc, hc_prev, 1)
                           + cw[2:3, :] * hc)
        y_ref[:, 0:W] = (ya * _row_inv(ya) * ga_ref[...]).astype(BF16)
        gu, _ = _gelu(u_ref[...])
        gv, _ = _gelu(v_ref[...])
        gvb = gv.astype(BF16)
        for hd in range(H):
            sl = slice(hd * HEAD_DIM, (hd + 1) * HEAD_DIM)
            s_ref[:, sl] = jnp.dot(wm_ref[hd], gvb[:, sl], preferred_element_type=F32)
        yb = gu * (s_ref[...] + be_ref[...])
        y_ref[:, W:2 * W] = (yb * _row_inv(yb) * gb_ref[...]).astype(BF16)

    full = lambda shape: pl.BlockSpec(shape, lambda i: (0,) * len(shape))
    outs, carried = _call(
        body, name=name, args=[proj, proj, proj, proj, proj, proj, proj, conv_w, wm, bias_e, g_a, g_b],
        grid=(T // CHUNK,),
        in_specs=cols + halos + [full((CONV_K, W)), full((H, CHUNK, CHUNK)), full((CHUNK, W)), full((1, W)), full((1, W))],
        out_specs=[pl.BlockSpec((CHUNK, 2 * W), lambda i: (i, 0))], out_shape=[jax.ShapeDtypeStruct((T, 2 * W), BF16)],
        scratch_shapes=[pltpu.VMEM((CHUNK, W), F32)], semantics=("parallel",), stages=stages)
    return outs[0], carried


def _mixers_bwd(dy, proj, conv_w, wm, wmt, bias_e, g_a, g_b, head_onehot, seq_len, name, stages=()):
    T, W5 = proj.shape
    W = W5 // 5
    H = W // HEAD_DIM
    nb = T // CHUNK
    per_seq = seq_len // CHUNK
    rb = CHUNK // SUBLANES
    blk = lambda i: nb - 1 - i
    cols, halos = _mixer_specs(W, blk, lambda i: jnp.maximum(blk(i) * rb - 1, 0))

    def body(dy_ref, b_ref, c_ref, hin_ref, u_ref, v_ref, ch_ref, hh_ref, cw_ref, wm_ref, wmt_ref, be_ref, ga_ref,
             gb_ref, oh_ref, dp_ref, dcw_ref, dga_ref, dgb_ref, dws_ref, dbt_ref, carry_ref, s_ref, dgv_ref):
        i = pl.program_id(0)
        j = nb - 1 - i

        @pl.when(i == 0)
        def _():
            for r in (dcw_ref, dga_ref, dgb_ref, dws_ref, dbt_ref, carry_ref):
                r[...] = jnp.zeros_like(r)

        first = (j % per_seq) == 0
        last = (j % per_seq) == per_seq - 1
        b, c, hin = b_ref[...], c_ref[...], hin_ref[...]
        cw = cw_ref[...]
        hc = c * hin
        hc_prev = jnp.where(first, 0.0, ch_ref[...] * hh_ref[...])
        hc1 = _shift_rows(hc, hc_prev, 1)
        hc2 = _shift_rows(hc, hc_prev, 2)
        conv = cw[0:1, :] * hc2 + cw[1:2, :] * hc1 + cw[2:3, :] * hc
        ya = b * conv
        inv_a = _row_inv(ya)
        na = ya * inv_a
        do_a = dy_ref[:, 0:W]
        dga_ref[...] += jnp.sum(do_a * na, axis=0, keepdims=True)
        dna = do_a * ga_ref[...]
        dya = inv_a * (dna - na * jnp.mean(dna * na, axis=-1, keepdims=True))
        dp_ref[:, 0:W] = (dya * conv).astype(BF16)
        dconv = dya * b
        dcw_ref[0:1, :] += jnp.sum(dconv * hc2, axis=0, keepdims=True)
        dcw_ref[1:2, :] += jnp.sum(dconv * hc1, axis=0, keepdims=True)
        dcw_ref[2:3, :] += jnp.sum(dconv * hc, axis=0, keepdims=True)
        nxt = jnp.where(last, 0.0, carry_ref[...])
        dhc = cw[2:3, :] * dconv + cw[1:2, :] * _unshift_rows(dconv, nxt, 1) + cw[0:1, :] * _unshift_rows(dconv, nxt, 2)
        carry_ref[...] = dconv[0:SUBLANES, :]
        dp_ref[:, W:2 * W] = (dhc * hin).astype(BF16)
        dp_ref[:, 2 * W:3 * W] = (dhc * c).astype(BF16)
        u, v = u_ref[...], v_ref[...]
        gu, thu = _gelu(u)
        gv, thv = _gelu(v)
        gvb = gv.astype(BF16)
        for hd in range(H):
            sl = slice(hd * HEAD_DIM, (hd + 1) * HEAD_DIM)
            s_ref[:, sl] = jnp.dot(wm_ref[hd], gvb[:, sl], preferred_element_type=F32)
        s = s_ref[...] + be_ref[...]
        yb = gu * s
        inv_b = _row_inv(yb)
        nbv = yb * inv_b
        do_b = dy_ref[:, W:2 * W]
        dgb_ref[...] += jnp.sum(do_b * nbv, axis=0, keepdims=True)
        dnb = do_b * gb_ref[...]
        dyb = inv_b * (dnb - nbv * jnp.mean(dnb * nbv, axis=-1, keepdims=True))
        dp_ref[:, 3 * W:4 * W] = (dyb * s * _gelu_grad(u, thu)).astype(BF16)
        dsb = (dyb * gu).astype(BF16)
        dbt_ref[...] += jnp.dot(dsb, oh_ref[...], preferred_element_type=F32)
        for hd in range(H):
            sl = slice(hd * HEAD_DIM, (hd + 1) * HEAD_DIM)
            dws_ref[hd] += lax.dot_general(dsb[:, sl], gvb[:, sl], (((1,), (1,)), ((), ())), preferred_element_type=F32)
            dgv_ref[:, sl] = jnp.dot(wmt_ref[hd], dsb[:, sl], preferred_element_type=F32)
        dp_ref[:, 4 * W:5 * W] = (dgv_ref[...] * _gelu_grad(v, thv)).astype(BF16)

    full = lambda shape: pl.BlockSpec(shape, lambda i: (0,) * len(shape))
    return _call(
        body, name=name, grid=(nb,),
        args=[dy, proj, proj, proj, proj, proj, proj, proj, conv_w, wm, wmt, bias_e, g_a, g_b, head_onehot],
        in_specs=[pl.BlockSpec((CHUNK, 2 * W), lambda i: (blk(i), 0))] + cols + halos
        + [full((CONV_K, W)), full((H, CHUNK, CHUNK)), full((H, CHUNK, CHUNK)), full((CHUNK, W)), full((1, W)),
           full((1, W)), full((W, LANES))],
        out_specs=[pl.BlockSpec((CHUNK, 5 * W), lambda i: (blk(i), 0)), full((SUBLANES, W)), full((1, W)), full((1, W)),
                   full((H, CHUNK, CHUNK)), full((CHUNK, LANES))],
        out_shape=[jax.ShapeDtypeStruct((T, 5 * W), BF16), jax.ShapeDtypeStruct((SUBLANES, W), F32),
                   jax.ShapeDtypeStruct((1, W), F32), jax.ShapeDtypeStruct((1, W), F32),
                   jax.ShapeDtypeStruct((H, CHUNK, CHUNK), F32), jax.ShapeDtypeStruct((CHUNK, LANES), F32)],
        scratch_shapes=[pltpu.VMEM((SUBLANES, W), F32), pltpu.VMEM((CHUNK, W), F32), pltpu.VMEM((CHUNK, W), F32)],
        semantics=("arbitrary",), stages=stages)


def _cast_into_slot(w, chip, name):
    R, C = w.shape
    tr = _tile(R, 256, 16)

    def body(chip_ref, w_ref, o_ref):
        o_ref[...] = w_ref[...].astype(BF16)

    return pl.pallas_call(
        body, name=name,
        grid_spec=pltpu.PrefetchScalarGridSpec(
            num_scalar_prefetch=1, grid=(R // tr,),
            in_specs=[pl.BlockSpec((tr, C), lambda i, chip_ref: (i, 0))],
            out_specs=pl.BlockSpec((None, tr, C), lambda i, chip_ref: (chip_ref[0], i, 0))),
        out_shape=jax.ShapeDtypeStruct((N_CHIPS, R, C), BF16),
        compiler_params=_params(("parallel",)),
    )(chip, w)


def _rs_pair_add(dw, got, chip_core, name):
    S, R, C = dw.shape
    hr = R // 2
    tr = _tile(hr, 256, 16)
    nrb = hr // tr

    def body(cc_ref, dw_ref, got_ref, send_ref, own_ref):
        s = dw_ref[...].astype(F32) + got_ref[...].astype(F32)
        send_ref[...] = s.astype(BF16)

        @pl.when(pl.program_id(1) == cc_ref[0])
        def _():
            own_ref[...] = s

    return pl.pallas_call(
        body, name=name,
        grid_spec=pltpu.PrefetchScalarGridSpec(
            num_scalar_prefetch=1, grid=(nrb, S),
            in_specs=[pl.BlockSpec((None, tr, C), lambda i, q, cc: (q, cc[1] * nrb + i, 0)),
                      pl.BlockSpec((None, tr, C), lambda i, q, cc: (q, i, 0))],
            out_specs=[pl.BlockSpec((None, tr, C), lambda i, q, cc: (q, i, 0)),
                       pl.BlockSpec((tr, C), lambda i, q, cc: (i, 0))]),
        out_shape=[jax.ShapeDtypeStruct((S, hr, C), BF16), jax.ShapeDtypeStruct((hr, C), F32)],
        compiler_params=_params(("parallel", "arbitrary")),
    )(chip_core, dw, got)


def _rs_final_add(own, got, chip_core, name):
    hr, C = own.shape
    tr = _tile(hr, 256, 16)
    nrb = hr // tr

    def body(cc_ref, own_ref, got_ref, o_ref):
        o_ref[...] = ((own_ref[...] + got_ref[0].astype(F32)) + got_ref[1].astype(F32)) + got_ref[2].astype(F32)

    return pl.pallas_call(
        body, name=name,
        grid_spec=pltpu.PrefetchScalarGridSpec(
            num_scalar_prefetch=1, grid=(nrb,),
            in_specs=[pl.BlockSpec((tr, C), lambda i, cc: (i, 0)), pl.BlockSpec((3, tr, C), lambda i, cc: (0, i, 0))],
            out_specs=pl.BlockSpec((tr, C), lambda i, cc: (cc[1] * nrb + i, 0))),
        out_shape=jax.ShapeDtypeStruct((2 * hr, C), F32),
        compiler_params=_params(("parallel",)),
    )(chip_core, own, got)


def _adamw_math(w, g, m, v):
    m2 = ADAM_B1 * m + (1.0 - ADAM_B1) * g
    v2 = ADAM_B2 * v + (1.0 - ADAM_B2) * (g * g)
    delta = -ADAM_LR * ((m2 / ADAM_C1) / (jnp.sqrt(v2 / ADAM_C2) + ADAM_EPS) + ADAM_WD * w)
    return delta, m2, v2


def _adamw(w, g, m, v, name):
    R, C = w.shape
    tr = _tile(R, max(SUBLANES, (256 * 1024) // C), SUBLANES)

    def body(w_ref, g_ref, m_ref, v_ref, g2_ref, d_ref, m2_ref, v2_ref):
        g = g_ref[...]
        g2_ref[...] = g
        d_ref[...], m2_ref[...], v2_ref[...] = _adamw_math(w_ref[...], g, m_ref[...], v_ref[...])

    blk = pl.BlockSpec((tr, C), lambda i: (i, 0))
    return _call(body, name=name, args=[w, g, m, v], grid=(R // tr,), in_specs=[blk] * 4, out_specs=[blk] * 4,
                 out_shape=[jax.ShapeDtypeStruct((R, C), F32)] * 4, semantics=("parallel",))[0]


def _all_gather_small(block, name):
    m_per, n = block.shape

    def body(x_ref, out_ref, send_sems, recv_sems, local_sem):
        x, y, c = _position()
        me, sibling = (x, y, c), (x, y, 1 - c)
        chips = _other_chips(x, y)

        def rows(px, py, pc):
            return out_ref.at[pl.ds((4 * px + 2 * py + pc) * m_per, m_per), :]

        def copy(k, blk, to, src=None):
            return pltpu.make_async_remote_copy(src_ref=rows(*blk) if src is None else src, dst_ref=rows(*blk),
                                                send_sem=send_sems.at[k], recv_sem=recv_sems.at[k], device_id=to,
                                                device_id_type=MESH)

        mine = pltpu.make_async_copy(x_ref, rows(*me), local_sem)
        mine.start()
        first = [copy(0, me, sibling, src=x_ref)]
        first += [copy(1 + j, me, (*chip, c), src=x_ref) for j, chip in enumerate(chips)]
        for cp in first:
            cp.start()
        passed = [copy(4 + j, (*chip, c), sibling) for j, chip in enumerate(chips)]
        for j, chip in enumerate(chips):
            copy(1 + j, (*chip, c), me).wait_recv()
            passed[j].start()
        copy(0, sibling, me).wait_recv()
        for j, chip in enumerate(chips):
            copy(4 + j, (*chip, 1 - c), me).wait_recv()
        for cp in first + passed:
            cp.wait_send()
        mine.wait()

    return pl.pallas_call(
        body, name=name,
        in_specs=[pl.BlockSpec(memory_space=pltpu.VMEM)],
        out_specs=pl.BlockSpec(memory_space=pltpu.VMEM),
        out_shape=jax.ShapeDtypeStruct((N_DEV * m_per, n), block.dtype),
        scratch_shapes=[pltpu.SemaphoreType.DMA((7,)), pltpu.SemaphoreType.DMA((7,)), pltpu.SemaphoreType.DMA],
        compiler_params=pltpu.CompilerParams(vmem_limit_bytes=VMEM_LIMIT_V7X, has_side_effects=True),
    )(block)


def _sum_and_adamw_small(gathered, w, m, v, name):
    rows, n = w.shape
    tr = _tile(rows, 32, SUBLANES)

    def body(p_ref, w_ref, m_ref, v_ref, g_ref, d_ref, m2_ref, v2_ref):
        g = p_ref[0]
        for d in range(1, N_DEV):
            g = g + p_ref[d]
        g_ref[...] = g
        d_ref[...], m2_ref[...], v2_ref[...] = _adamw_math(w_ref[...], g, m_ref[...], v_ref[...])

    blk = pl.BlockSpec((tr, n), lambda i: (i, 0))
    return pl.pallas_call(
        body, name=name, grid=(rows // tr,),
        in_specs=[pl.BlockSpec((N_DEV, tr, n), lambda i: (0, i, 0))] + [blk] * 3,
        out_specs=[blk] * 4,
        out_shape=[jax.ShapeDtypeStruct((rows, n), F32)] * 4,
        compiler_params=_params(("parallel",)),
    )(gathered.reshape(N_DEV, rows, n), w, m, v)


def _pad_rows(a):
    pad = (-a.shape[0]) % SUBLANES
    return jnp.pad(a, ((0, pad), (0, 0))) if pad else a


class _SmallPack:
    def __init__(self, W, D, H, chip):
        self.W, self.D, self.H, self.chip = W, D, H, chip
        self.offsets = {}
        self.rows = 0

    def pack(self, pieces):
        out = []
        self.offsets, self.rows = {}, 0
        for name, a in pieces:
            a = _pad_rows(a.astype(F32))
            self.offsets[name] = (self.rows, a.shape[0])
            self.rows += a.shape[0]
            out.append(a)
        return jnp.concatenate(out, axis=0)

    def piece(self, packed, name):
        start, n = self.offsets[name]
        return packed[start:start + n]


def _bias_rows(b, W):
    bt = jnp.pad(b.T, ((0, 0), (0, LANES - b.shape[0])))
    return bt.reshape(-1, W)


def _bias_from_rows(rows, H):
    return rows.reshape(-1)[:CHUNK * LANES].reshape(CHUNK, LANES)[:, :H].T


def kernel(x, mix_norm_g, w_in, conv_w, spatial_w, spatial_b, conv_out_norm_g, gmlp_out_norm_g, w_out, mlp_norm_g, w_up, w_down, final_norm_g, loss_target, m_mix_norm_g, m_w_in, m_conv_w, m_spatial_w, m_spatial_b, m_conv_out_norm_g, m_gmlp_out_norm_g, m_w_out, m_mlp_norm_g, m_w_up, m_w_down, m_final_norm_g, v_mix_norm_g, v_w_in, v_conv_w, v_spatial_w, v_spatial_b, v_conv_out_norm_g, v_gmlp_out_norm_g, v_w_out, v_mlp_norm_g, v_w_up, v_w_down, v_final_norm_g):
    Bl, S, D = x.shape
    T = Bl * S
    W = conv_out_norm_g.shape[-1]
    H = W // HEAD_DIM
    Wl = conv_w.shape[-1]
    xi, yi, ci = _position()
    chip = (2 * xi + yi).astype(jnp.int32)
    chip_arr = chip.reshape(1)
    chip_core = jnp.stack([chip, ci.astype(jnp.int32)])

    x2 = x.reshape(T, D)
    tgt2 = loss_target.reshape(T, D)

    s_in = _cast_into_slot(w_in[0], chip_arr, "cast_w_in")
    s_out = _cast_into_slot(w_out[0], chip_arr, "cast_w_out")
    s_up = _cast_into_slot(w_up[0], chip_arr, "cast_w_up")
    s_down = _cast_into_slot(w_down[0], chip_arr, "cast_w_down")
    up_rows = s_up.shape[1] // 2
    up_cuts = [0] + [up_rows * pct // 100 // 16 * 16 for pct in (43, 57, 82)] + [up_rows]
    up_part = [(lo, hi - lo) for lo, hi in zip(up_cuts[:-1], up_cuts[1:])]

    causal = jnp.tril(jnp.ones((CHUNK, CHUNK), dtype=bool))
    wm = jnp.where(causal[None], spatial_w[0], 0.0).astype(BF16)
    wmt = jnp.swapaxes(wm, 1, 2)
    bias_e = jnp.repeat(spatial_b[0].T, HEAD_DIM, axis=1)
    conv_full = lax.dynamic_update_slice(jnp.zeros((CONV_K, W), F32), conv_w[0], (0, chip * Wl))
    conv_gathered = _all_gather_small(_pad_rows(conv_full), "all_gather_conv_w")
    conv_w_all = conv_gathered.reshape(N_DEV, SUBLANES, W)[:, :CONV_K]
    conv_w_all = conv_w_all[0] + conv_w_all[2] + conv_w_all[4] + conv_w_all[6]
    head_onehot = (jnp.arange(W)[:, None] // HEAD_DIM == jnp.arange(LANES)[None, :]).astype(BF16)
    g_a, g_b = conv_out_norm_g, gmlp_out_norm_g

    xn, ((g_in,),) = _rmsnorm_fwd(x2, mix_norm_g, "mix_norm_fwd", stages=[_GatherRows(s_in)])
    proj, ((g_out,), (g_up,)) = _matmul(xn, g_in, mode="nn", name="proj_fwd", tm=1024, tn=512, tk=4096,
                                        out_dtypes=[F32], b_shard="n",
                                        stages=[_GatherRows(s_out), _GatherRows(s_up, *up_part[0])])
    g_out = g_out.reshape(-1, D)
    y, ((g_up,),) = _mixers_fwd(proj, conv_w_all, wm, bias_e, g_a, g_b, S, "mixers_fwd",
                                stages=[_GatherRows(g_up, *up_part[1])])
    h1, ((g_up,),) = _matmul(y, g_out, mode="nn", name="out_proj_fwd", tm=1024, tn=512, tk=4096, out_dtypes=[F32],
                             epilogue=_ep_residual, extras=(x2,), stages=[_GatherRows(g_up, *up_part[2])])
    xn2, ((g_up,),) = _rmsnorm_fwd(h1, mlp_norm_g, "mlp_norm_fwd", stages=[_GatherRows(g_up, *up_part[3])])
    (r, a), ((g_down,),) = _matmul(xn2, g_up, mode="nn", name="up_fwd", tm=1024, tn=1024, tk=4096, n_sub=2,
                                   out_dtypes=[BF16, BF16], epilogue=_ep_relu2, b_shard="n",
                                   stages=[_GatherRows(s_down)])
    g_down = g_down.reshape(-1, D)
    d2, _ = _matmul(a, g_down, mode="nn", name="down_fwd", tm=2048, tn=1024, tk=1024, out_dtypes=[F32])
    dh2, dh2b, d_final_g, loss_part = _loss_and_final_norm_bwd(h1, d2, tgt2, final_norm_g.reshape(1, D),
                                                               "loss_final_norm")

    def rs_adds(dw, got, tag):
        return _rs_pair_add(dw, got, chip_core, f"rs_pair_add_{tag}")

    dw_down, _ = _matmul(a, dh2b, mode="tn", name="down_dw", tm=1024, tn=1024, tk=4096, n_sub=2, out_dtypes=[BF16])
    dw_down = dw_down.reshape(N_CHIPS, -1, D)
    dpre, ((got_down,),) = _matmul(dh2b, g_down, mode="nt", name="down_dx", tm=1024, tn=1024, tk=4096, n_sub=2,
                                   out_dtypes=[BF16], epilogue=_ep_relu2_bwd, extras=(r,),
                                   stages=[_PairExchange(dw_down)])
    part_down, own_down = rs_adds(dw_down, got_down, "w_down")
    rows_down = part_down.shape[1]
    down_a = rows_down * 3 // 4 // 16 * 16
    dw_up, ((landed_down,),) = _matmul(xn2, dpre, mode="tn", name="up_dw", tm=1024, tn=1024, tk=4096, n_sub=2,
                                       out_dtypes=[BF16], out_shard=True,
                                       stages=[_ChipExchange(part_down, None, 0, down_a)])
    dxn2, ((got_up,), (landed_down,)) = _matmul(
        dpre, g_up, mode="nt", name="up_dx", tm=2048, tn=1024, tk=1024, out_dtypes=[F32], b_shard="k",
        stages=[_PairExchange(dw_up), _ChipExchange(part_down, landed_down, down_a, rows_down - down_a)])
    half_down = _rs_final_add(own_down, landed_down, chip_core, "rs_final_add_w_down")
    part_up, own_up = rs_adds(dw_up, got_up, "w_up")
    (dh1, dh1b, d_mlp_g), ((grad_down,),) = _rmsnorm_bwd(dxn2, h1, mlp_norm_g, dh2, "mlp_norm_bwd",
                                                        stages=[_HalfExchange(half_down)])
    rows_up = part_up.shape[1]
    q_up = rows_up // 4 // 16 * 16
    dy, ((landed_up,),) = _matmul(dh1b, g_out, mode="nt", name="out_proj_dx", tm=1024, tn=512, tk=4096,
                                  out_dtypes=[F32], stages=[_ChipExchange(part_up, None, 0, q_up)])
    (dproj, d_conv, d_ga, d_gb, d_ws, d_bt), ((landed_up,),) = _mixers_bwd(
        dy, proj, conv_w_all, wm, wmt, bias_e, g_a, g_b, head_onehot, S, "mixers_bwd",
        stages=[_ChipExchange(part_up, landed_up, q_up, q_up)])
    dw_in, ((landed_up,),) = _matmul(
        xn, dproj, mode="tn", name="proj_dw", tm=1024, tn=512, tk=4096, out_dtypes=[BF16], out_shard=True,
        stages=[_ChipExchange(part_up, landed_up, 2 * q_up, rows_up - 2 * q_up)])
    half_up = _rs_final_add(own_up, landed_up, chip_core, "rs_final_add_w_up")
    dw_out, ((got_in,), (grad_up,)) = _matmul(y, dh1b, mode="tn", name="out_proj_dw", tm=1024, tn=1024, tk=4096, n_sub=2,
                                              out_dtypes=[BF16],
                                              stages=[_PairExchange(dw_in), _HalfExchange(half_up)])
    dw_out = dw_out.reshape(N_CHIPS, -1, D)
    part_in, own_in = rs_adds(dw_in, got_in, "w_in")
    dxn, ((landed_in,), (got_out,)) = _matmul(
        dproj, g_in, mode="nt", name="proj_dx", tm=2048, tn=1024, tk=1280, out_dtypes=[F32], b_shard="k",
        stages=[_ChipExchange(part_in), _PairExchange(dw_out)])
    part_out, own_out = rs_adds(dw_out, got_out, "w_out")
    half_in = _rs_final_add(own_in, landed_in, chip_core, "rs_final_add_w_in")
    (grad_x, _unused, d_mix_g), ((landed_out,), (grad_in,)) = _rmsnorm_bwd(
        dxn, x2, mix_norm_g, dh1, "mix_norm_bwd", stages=[_ChipExchange(part_out), _HalfExchange(half_in)])
    half_out = _rs_final_add(own_out, landed_out, chip_core, "rs_final_add_w_out")
    ((grad_out,),) = _run_stages([_HalfExchange(half_out)], "rs_half_exchange_w_out")
    big = {"w_down": _adamw(w_down[0], grad_down, m_w_down[0], v_w_down[0], "adamw_w_down"),
           "w_up": _adamw(w_up[0], grad_up, m_w_up[0], v_w_up[0], "adamw_w_up"),
           "w_in": _adamw(w_in[0], grad_in, m_w_in[0], v_w_in[0], "adamw_w_in"),
           "w_out": _adamw(w_out[0], grad_out, m_w_out[0], v_w_out[0], "adamw_w_out")}
    big = {k: [t[None] for t in v] for k, v in big.items()}

    pack = _SmallPack(W, D, H, chip)
    causal_f = causal.astype(F32)
    loss_row = jnp.pad(loss_part[:, :1], ((0, 0), (0, W - 1)))

    def small(mix, conv, sw, sb, ga, gb, mlp, fin, extra):
        return pack.pack([("spatial_w", sw.reshape(-1, W)), ("conv_w", conv), ("mix_norm_g", mix.reshape(-1, W)),
                          ("mlp_norm_g", mlp.reshape(-1, W)), ("final_norm_g", fin.reshape(-1, W)),
                          ("conv_out_norm_g", ga.reshape(-1, W)), ("gmlp_out_norm_g", gb.reshape(-1, W)),
                          ("spatial_b", _bias_rows(sb, W)), ("loss", extra)])

    def full_conv(cw):
        return lax.dynamic_update_slice(jnp.zeros((CONV_K, W), F32), cw[0], (0, chip * Wl))

    zero_row = jnp.zeros((1, W), F32)
    g_part = pack.pack([("spatial_w", (d_ws * causal_f[None]).reshape(-1, W)), ("conv_w", d_conv),
                        ("mix_norm_g", d_mix_g.reshape(-1, W)), ("mlp_norm_g", d_mlp_g.reshape(-1, W)),
                        ("final_norm_g", d_final_g.reshape(-1, W)), ("conv_out_norm_g", d_ga), ("gmlp_out_norm_g", d_gb),
                        ("spatial_b", d_bt.reshape(-1, W)), ("loss", loss_row)])
    w_s = small(mix_norm_g, full_conv(conv_w), spatial_w, spatial_b[0], conv_out_norm_g, gmlp_out_norm_g, mlp_norm_g,
                final_norm_g, zero_row)
    m_s = small(m_mix_norm_g, full_conv(m_conv_w), m_spatial_w, m_spatial_b[0], m_conv_out_norm_g, m_gmlp_out_norm_g,
                m_mlp_norm_g, m_final_norm_g, zero_row)
    v_s = small(v_mix_norm_g, full_conv(v_conv_w), v_spatial_w, v_spatial_b[0], v_conv_out_norm_g, v_gmlp_out_norm_g,
                v_mlp_norm_g, v_final_norm_g, zero_row)
    g_all = _all_gather_small(g_part, "all_gather_small_grads")
    small_outs = _sum_and_adamw_small(g_all, w_s, m_s, v_s, "sum_adamw_small")

    def unpack(packed, name):
        rows = pack.piece(packed, name)
        if name == "spatial_w":
            return rows.reshape(1, H, CHUNK, CHUNK)
        if name == "conv_w":
            return lax.dynamic_slice(rows[:CONV_K], (0, chip * Wl), (CONV_K, Wl))[None]
        if name == "spatial_b":
            return _bias_from_rows(rows, H)[None]
        if name == "final_norm_g":
            return rows.reshape(-1)[:D]
        n = D if name in ("mix_norm_g", "mlp_norm_g") else W
        return rows.reshape(-1)[:n].reshape(1, n)

    loss = pack.piece(small_outs[0], "loss")[0, 0]
    order = ["mix_norm_g", "w_in", "conv_w", "spatial_w", "spatial_b", "conv_out_norm_g", "gmlp_out_norm_g", "w_out",
             "mlp_norm_g", "w_up", "w_down", "final_norm_g"]
    outs = [loss, grad_x.reshape(Bl, S, D)]
    for kind in range(4):
        for name in order:
            outs.append(big[name][kind] if name in big else unpack(small_outs[kind], name))
    return tuple(outs)
```

```python
import functools
import math

import jax
import jax.numpy as jnp
from jax import lax
from jax.experimental import pallas as pl
from jax.experimental.pallas import tpu as pltpu

F32 = jnp.float32
BF16 = jnp.bfloat16
MESH = pl.DeviceIdType.MESH

NORM_EPS = 1e-5
HEAD_DIM = 128
CHUNK = 128
CONV_K = 3
N_CHIPS = 4
N_DEV = 8

ADAM_LR = 0.001
ADAM_B1 = 0.9
ADAM_B2 = 0.999
ADAM_EPS = 1e-08
ADAM_WD = 0.01
ADAM_STEP = 10
ADAM_C1 = 1.0 - ADAM_B1 ** ADAM_STEP
ADAM_C2 = 1.0 - ADAM_B2 ** ADAM_STEP

GELU_K = math.sqrt(2.0 / math.pi)
GELU_A = 0.044715

VMEM_LIMIT_V7X = 56 * 1024 * 1024
SUBLANES = 8
LANES = 128


def _tile(dim, target, mult=LANES):
    if dim <= target:
        return dim
    t = (target // mult) * mult
    while t > mult and dim % t:
        t -= mult
    assert dim % t == 0, (dim, target, mult)
    return t


def _params(sem=None):
    return pltpu.CompilerParams(dimension_semantics=sem, vmem_limit_bytes=VMEM_LIMIT_V7X)


class _Stage:
    bufs = ()
    n_sems = 0
    MIDDLE_AT = 0.6
    base = 0

    def start(self, refs, send, recv):
        raise NotImplementedError

    def middle(self, refs, send, recv):
        pass

    def finish(self, refs, send, recv):
        raise NotImplementedError


def _position():
    return lax.axis_index("x"), lax.axis_index("y"), lax.axis_index("c")


def _other_chips(x, y):
    return [(1 - x, y), (x, 1 - y), (1 - x, 1 - y)]


def _remote(src, dst, send, recv, k, to):
    return pltpu.make_async_remote_copy(src_ref=src, dst_ref=dst, send_sem=send.at[k], recv_sem=recv.at[k],
                                        device_id=to, device_id_type=MESH)


def _call(body, *, name, args, in_specs, out_specs, out_shape, grid=(), scratch_shapes=(), semantics=None, stages=(),
          prefetch=None):
    n_in, n_out, n_scratch = len(args), len(out_shape), len(scratch_shapes)
    n_pre = 0 if prefetch is None else 1
    any_spec = pl.BlockSpec(memory_space=pl.ANY)
    extra_args, extra_out, aliases, layout = [], [], {}, []
    for st in stages:
        where = []
        for kind, buf in st.bufs:
            if kind in ("in", "alias"):
                extra_args.append(buf)
                pos_in = n_in + len(extra_args) - 1
            if kind in ("out", "alias"):
                extra_out.append(jax.ShapeDtypeStruct(buf.shape, buf.dtype))
                pos_out = n_out + len(extra_out) - 1
            if kind == "alias":
                aliases[n_pre + pos_in] = pos_out
            where.append(("in", pos_in) if kind == "in" else ("out", pos_out))
        layout.append(where)
    n_sems = sum(st.n_sems for st in stages)
    n_xin, n_xout = len(extra_args), len(extra_out)

    def wrapped(*refs):
        pre, refs = refs[:n_pre], refs[n_pre:]
        ins = refs[:n_in + n_xin]
        outs = refs[n_in + n_xin:n_in + n_xin + n_out + n_xout]
        scratch = refs[n_in + n_xin + n_out + n_xout:]
        main = pre + ins[:n_in] + outs[:n_out] + scratch[:n_scratch]
        if not stages:
            body(*main)
            return
        send, recv = scratch[n_scratch], scratch[n_scratch + 1]
        step, n_steps = 0, 1
        for d, g in enumerate(grid):
            step = step * g + pl.program_id(d)
            n_steps *= g
        base, views = 0, []
        for st, where in zip(stages, layout):
            st_refs = [ins[p] if side == "in" else outs[p] for side, p in where]
            st.base = base
            views.append((st, st_refs, send, recv))
            base += st.n_sems

        def starts():
            for st, r, s, v in views:
                st.start(r, s, v)

        def middles():
            for st, r, s, v in views:
                st.middle(r, s, v)

        def finishes():
            for st, r, s, v in views:
                st.finish(r, s, v)

        if not grid:
            starts()
            body(*main)
            middles()
            finishes()
        else:
            pl.when(step == 0)(starts)
            body(*main)
            pl.when(step == min(int(n_steps * _Stage.MIDDLE_AT), n_steps - 1))(middles)
            pl.when(step == n_steps - 1)(finishes)

    sems = [pltpu.SemaphoreType.DMA((n_sems,)), pltpu.SemaphoreType.DMA((n_sems,))] if stages else []
    specs = dict(in_specs=list(in_specs) + [any_spec] * n_xin, out_specs=list(out_specs) + [any_spec] * n_xout,
                 scratch_shapes=list(scratch_shapes) + sems)
    if prefetch is None:
        kw = dict(specs, **(dict(grid=grid) if grid else {}))
    else:
        kw = dict(grid_spec=pltpu.PrefetchScalarGridSpec(num_scalar_prefetch=1, grid=grid, **specs))
    res = pl.pallas_call(
        wrapped, name=name,
        out_shape=list(out_shape) + extra_out,
        input_output_aliases=aliases,
        compiler_params=pltpu.CompilerParams(
            dimension_semantics=("arbitrary",) * len(grid) if stages and grid else semantics,
            vmem_limit_bytes=VMEM_LIMIT_V7X, has_side_effects=bool(stages)),
        **kw,
    )(*([] if prefetch is None else [prefetch]), *args, *extra_args)
    main_res, stage_res, pos = list(res[:n_out]), [], n_out
    for st in stages:
        k = sum(kind in ("out", "alias") for kind, _ in st.bufs)
        stage_res.append(list(res[pos:pos + k]))
        pos += k
    return main_res, stage_res


def _run_stages(stages, name):
    return _call(lambda: None, name=name, args=[], in_specs=[], out_specs=[], out_shape=[], stages=stages)[1]


class _GatherRows(_Stage):
    n_sems = 7

    def __init__(self, g, lo=0, n=None):
        self.hr = g.shape[1] // 2
        self.lo, self.n = lo, (self.hr if n is None else n)
        self.n0 = self.n // 2 // 16 * 16
        self.bufs = [("alias", g)]

    def _copy(self, g_ref, send, recv, k, chip_xy, half, to, lo=0, n=None):
        n = self.n if n is None else n
        blk = g_ref.at[2 * chip_xy[0] + chip_xy[1], pl.ds(half * self.hr + self.lo + lo, n), :]
        return _remote(blk, blk, send, recv, self.base + k, to)

    def _plan(self, g, send, recv):
        x, y, c = _position()
        me, sib = (x, y, c), (x, y, 1 - c)
        cx, cy, cd = _other_chips(x, y)
        n0, n1 = self.n0, self.n - self.n0
        mine = [((x, y), c, (*cx, c)), ((x, y), c, (*cy, c)), (cx, c, sib), (cy, c, sib),
                (cx, c, (*cy, c), 0, n0), (cy, c, (*cx, c), n0, n1), (cd, c, sib)]
        theirs = [(cx, c, me), (cy, c, me), (cx, 1 - c, me), (cy, 1 - c, me),
                  (cd, c, me, 0, n0), (cd, c, me, n0, n1), (cd, 1 - c, me)]
        return (lambda k: self._copy(g, send, recv, k, *mine[k])), (lambda k: self._copy(g, send, recv, k, *theirs[k]))

    def start(self, refs, send, recv):
        mine, _ = self._plan(refs[0], send, recv)
        mine(0).start()
        mine(1).start()

    def middle(self, refs, send, recv):
        mine, theirs = self._plan(refs[0], send, recv)
        theirs(0).wait_recv()
        mine(4).start()
        mine(2).start()
        theirs(1).wait_recv()
        mine(5).start()
        mine(3).start()

    def finish(self, refs, send, recv):
        mine, theirs = self._plan(refs[0], send, recv)
        theirs(4).wait_recv()
        theirs(5).wait_recv()
        mine(6).start()
        for k in (2, 3, 6):
            theirs(k).wait_recv()
        for k in range(self.n_sems):
            mine(k).wait_send()


class _GatherSmall(_Stage):
    n_sems = 8

    def __init__(self, block):
        self.m = block.shape[0]
        self.bufs = [("in", block), ("out", jax.ShapeDtypeStruct((N_DEV * self.m, block.shape[1]), block.dtype))]

    def _rows(self, out, px, py, pc):
        return out.at[pl.ds((4 * px + 2 * py + pc) * self.m, self.m), :]

    def _copy(self, refs, send, recv, k, blk, to, own=False):
        dst = self._rows(refs[1], *blk)
        return _remote(refs[0] if own else dst, dst, send, recv, self.base + k, to)

    def _local(self, refs, send):
        return pltpu.make_async_copy(refs[0], self._rows(refs[1], *_position()), send.at[self.base + 7])

    def start(self, refs, send, recv):
        x, y, c = _position()
        self._local(refs, send).start()
        self._copy(refs, send, recv, 0, (x, y, c), (x, y, 1 - c), own=True).start()
        for j, chip in enumerate(_other_chips(x, y)):
            self._copy(refs, send, recv, 1 + j, (x, y, c), (*chip, c), own=True).start()

    def finish(self, refs, send, recv):
        x, y, c = _position()
        me, sib, chips = (x, y, c), (x, y, 1 - c), _other_chips(x, y)
        for j, chip in enumerate(chips):
            self._copy(refs, send, recv, 1 + j, (*chip, c), me).wait_recv()
            self._copy(refs, send, recv, 4 + j, (*chip, c), sib).start()
        self._copy(refs, send, recv, 0, sib, me).wait_recv()
        for j, chip in enumerate(chips):
            self._copy(refs, send, recv, 4 + j, (*chip, 1 - c), me).wait_recv()
        self._copy(refs, send, recv, 0, me, sib, own=True).wait_send()
        for j, chip in enumerate(chips):
            self._copy(refs, send, recv, 1 + j, me, (*chip, c), own=True).wait_send()
            self._copy(refs, send, recv, 4 + j, (*chip, c), sib).wait_send()
        self._local(refs, send).wait()


class _PairExchange(_Stage):
    n_sems = 1

    def __init__(self, dw):
        S, R, C = dw.shape
        self.hr = R // 2
        self.bufs = [("in", dw), ("out", jax.ShapeDtypeStruct((S, self.hr, C), dw.dtype))]

    def _copy(self, refs, send, recv):
        x, y, c = _position()
        return _remote(refs[0].at[:, pl.ds((1 - c) * self.hr, self.hr), :], refs[1], send, recv, self.base,
                       (x, y, 1 - c))

    def start(self, refs, send, recv):
        self._copy(refs, send, recv).start()

    def finish(self, refs, send, recv):
        cp = self._copy(refs, send, recv)
        cp.wait_recv()
        cp.wait_send()


class _ChipExchange(_Stage):
    n_sems = 3

    def __init__(self, part, landed=None, lo=0, n=None):
        S, hr, C = part.shape
        self.lo, self.n = lo, (hr if n is None else n)
        self.bufs = [("in", part), ("out", jax.ShapeDtypeStruct((3, hr, C), part.dtype)) if landed is None
                     else ("alias", landed)]

    def _copies(self, refs, send, recv):
        x, y, c = _position()
        rows = pl.ds(self.lo, self.n)
        return [_remote(refs[0].at[2 * chip[0] + chip[1], rows, :], refs[1].at[j, rows, :], send, recv,
                        self.base + j, (*chip, c))
                for j, chip in enumerate(_other_chips(x, y))]

    def start(self, refs, send, recv):
        for cp in self._copies(refs, send, recv):
            cp.start()

    def finish(self, refs, send, recv):
        copies = self._copies(refs, send, recv)
        for cp in copies:
            cp.wait_recv()
        for cp in copies:
            cp.wait_send()


class _HalfExchange(_Stage):
    n_sems = 1

    def __init__(self, grad):
        self.hr = grad.shape[0] // 2
        self.bufs = [("alias", grad)]

    def start(self, refs, send, recv):
        x, y, c = _position()
        mine = refs[0].at[pl.ds(c * self.hr, self.hr), :]
        _remote(mine, mine, send, recv, self.base, (x, y, 1 - c)).start()

    def finish(self, refs, send, recv):
        x, y, c = _position()
        mine = refs[0].at[pl.ds(c * self.hr, self.hr), :]
        theirs = refs[0].at[pl.ds((1 - c) * self.hr, self.hr), :]
        _remote(theirs, theirs, send, recv, self.base, (x, y, 1 - c)).wait_recv()
        _remote(mine, mine, send, recv, self.base, (x, y, 1 - c)).wait_send()


def _matmul(a, b, *, mode, name, tm, tn, tk, out_dtypes, epilogue=None, extras=(), b_shard=None, out_shard=False,
            stages=(), n_sub=1):
    if mode == "tn":
        K, M = a.shape
    else:
        M, K = a.shape
    if b_shard == "n":
        S, Kb, Ns = b.shape
        N = S * Ns
    elif b_shard == "k":
        S, N, Ks = b.shape
        Kb = S * Ks
    elif mode == "nt":
        N, Kb = b.shape
    else:
        Kb, N = b.shape
    assert Kb == K, (name, a.shape, b.shape)
    tm, tn, tk = _tile(M, tm), _tile(N, tn), _tile(K, tk)
    if b_shard == "n" or out_shard:
        n_per = N // N_CHIPS
        tn = _tile(n_per, tn)
        njs = n_per // tn
    if b_shard == "k":
        tk = _tile(K // N_CHIPS, tk)
        nks = (K // N_CHIPS) // tk
    gm, gn, gk = M // tm, N // tn, K // tk
    if gk > 1 or tn % (n_sub * LANES):
        n_sub = 1

    if mode == "tn":
        a_spec = pl.BlockSpec((tk, tm), lambda i, j, k: (k, i))
        dims = (((0,), (0,)), ((), ()))
    else:
        a_spec = pl.BlockSpec((tm, tk), lambda i, j, k: (i, k))
        dims = (((1,), (1,)), ((), ())) if mode == "nt" else (((1,), (0,)), ((), ()))
    if b_shard == "n":
        b_spec = pl.BlockSpec((None, tk, tn), lambda i, j, k: (j // njs, k, j % njs))
    elif b_shard == "k":
        b_spec = pl.BlockSpec((None, tn, tk), lambda i, j, k: (k // nks, j, k % nks))
    elif mode == "nt":
        b_spec = pl.BlockSpec((tn, tk), lambda i, j, k: (j, k))
    else:
        b_spec = pl.BlockSpec((tk, tn), lambda i, j, k: (k, j))
    mn_spec = pl.BlockSpec((tm, tn), lambda i, j, k: (i, j))
    if out_shard:
        out_spec = pl.BlockSpec((None, tm, tn), lambda i, j, k: (j // njs, i, j % njs))
        out_shape = [jax.ShapeDtypeStruct((N_CHIPS, M, N // N_CHIPS), dt) for dt in out_dtypes]
    else:
        out_spec = mn_spec
        out_shape = [jax.ShapeDtypeStruct((M, N), dt) for dt in out_dtypes]
    n_extra, n_out = len(extras), len(out_dtypes)

    def finish_tile(acc, extra_refs, out_refs):
        if epilogue is None:
            for o in out_refs:
                o[...] = acc.astype(o.dtype)
        else:
            epilogue(acc, extra_refs, out_refs)

    def body(*refs):
        a_ref, b_ref = refs[0], refs[1]
        extra_refs = refs[2:2 + n_extra]
        out_refs = refs[2 + n_extra:2 + n_extra + n_out]

        def product():
            return lax.dot_general(a_ref[...], b_ref[...], dims, preferred_element_type=F32)

        if gk == 1:
            sub = tn // n_sub
            for h in range(n_sub):
                cols = slice(h * sub, (h + 1) * sub)
                b_part = b_ref[cols, :] if mode == "nt" else b_ref[:, cols]
                acc = lax.dot_general(a_ref[...], b_part, dims, preferred_element_type=F32)
                finish_tile(acc, [e.at[:, cols] for e in extra_refs], [o.at[:, cols] for o in out_refs])
            return
        acc_ref = refs[-1]
        k = pl.program_id(2)

        @pl.when(k == 0)
        def _():
            acc_ref[...] = product()

        @pl.when((k > 0) & (k < gk - 1))
        def _():
            acc_ref[...] += product()

        @pl.when(k == gk - 1)
        def _():
            finish_tile(acc_ref[...] + product(), extra_refs, out_refs)

    outs, carried = _call(
        body, name=name, args=[a, b, *extras], grid=(gm, gn, gk),
        in_specs=[a_spec, b_spec] + [mn_spec] * n_extra, out_specs=[out_spec] * n_out, out_shape=out_shape,
        scratch_shapes=[pltpu.VMEM((tm, tn), F32)] if gk > 1 else [],
        semantics=("parallel", "parallel", "arbitrary"), stages=stages)
    return (outs[0] if n_out == 1 else outs), carried


def _ep_residual(acc, extra_refs, out_refs):
    out_refs[0][...] = extra_refs[0][...] + acc


def _ep_relu2(acc, extra_refs, out_refs):
    r = jnp.maximum(acc, 0.0)
    out_refs[0][...] = r.astype(BF16)
    out_refs[1][...] = (r * r).astype(BF16)


def _ep_relu2_bwd(acc, extra_refs, out_refs):
    out_refs[0][...] = (acc * (2.0 * extra_refs[0][...].astype(F32))).astype(BF16)


def _row_inv(x):
    return lax.rsqrt(jnp.mean(x * x, axis=-1, keepdims=True) + NORM_EPS)


def _rmsnorm_fwd(x, g, name, stages=()):
    T, D = x.shape
    tt = _tile(T, 256, SUBLANES)

    def body(x_ref, g_ref, o_ref):
        xv = x_ref[...]
        o_ref[...] = (xv * _row_inv(xv) * g_ref[...]).astype(BF16)

    outs, carried = _call(
        body, name=name, args=[x, g], grid=(T // tt,),
        in_specs=[pl.BlockSpec((tt, D), lambda i: (i, 0)), pl.BlockSpec((1, D), lambda i: (0, 0))],
        out_specs=[pl.BlockSpec((tt, D), lambda i: (i, 0))], out_shape=[jax.ShapeDtypeStruct((T, D), BF16)],
        semantics=("parallel",), stages=stages)
    return outs[0], carried


def _rmsnorm_fwd_and_casts(x, g, weights, chip, name, stages=()):
    T, D = x.shape
    tt = _tile(T, 256, SUBLANES)
    n, nw = T // tt, len(weights)
    rows = [w.shape[0] // n for w in weights]
    assert all(r % 16 == 0 and r * n == w.shape[0] for r, w in zip(rows, weights))

    def body(chip_ref, x_ref, g_ref, *refs):
        w_refs, o_ref, slot_refs = refs[:nw], refs[nw], refs[nw + 1:]
        xv = x_ref[...]
        o_ref[...] = (xv * _row_inv(xv) * g_ref[...]).astype(BF16)
        for w_ref, s_ref in zip(w_refs, slot_refs):
            s_ref[...] = w_ref[...].astype(BF16)

    outs, carried = _call(
        body, name=name, args=[x, g, *weights], grid=(n,), prefetch=chip,
        in_specs=[pl.BlockSpec((tt, D), lambda i, chip_ref: (i, 0)), pl.BlockSpec((1, D), lambda i, chip_ref: (0, 0))]
        + [pl.BlockSpec((r, w.shape[1]), lambda i, chip_ref: (i, 0)) for r, w in zip(rows, weights)],
        out_specs=[pl.BlockSpec((tt, D), lambda i, chip_ref: (i, 0))]
        + [pl.BlockSpec((None, r, w.shape[1]), lambda i, chip_ref: (chip_ref[0], i, 0)) for r, w in zip(rows, weights)],
        out_shape=[jax.ShapeDtypeStruct((T, D), BF16)]
        + [jax.ShapeDtypeStruct((N_CHIPS, *w.shape), BF16) for w in weights],
        semantics=("parallel",), stages=stages)
    return outs[0], outs[1:], carried


def _rmsnorm_bwd(dxn, h, g, dres, name, stages=()):
    T, D = h.shape
    tt = _tile(T, 128, SUBLANES)

    def body(dxn_ref, h_ref, g_ref, dres_ref, dh_ref, dhb_ref, dg_ref):
        @pl.when(pl.program_id(0) == 0)
        def _():
            dg_ref[...] = jnp.zeros_like(dg_ref)

        hv = h_ref[...]
        inv = _row_inv(hv)
        n = hv * inv
        d = dxn_ref[...]
        dg_ref[...] += jnp.sum(d * n, axis=0, keepdims=True)
        dn = d * g_ref[...]
        dh = dres_ref[...] + inv * (dn - n * jnp.mean(dn * n, axis=-1, keepdims=True))
        dh_ref[...] = dh
        dhb_ref[...] = dh.astype(BF16)

    row = pl.BlockSpec((tt, D), lambda i: (i, 0))
    vec = pl.BlockSpec((1, D), lambda i: (0, 0))
    return _call(
        body, name=name, args=[dxn, h, g, dres], grid=(T // tt,), in_specs=[row, row, vec, row],
        out_specs=[row, row, vec],
        out_shape=[jax.ShapeDtypeStruct((T, D), F32), jax.ShapeDtypeStruct((T, D), BF16),
                   jax.ShapeDtypeStruct((1, D), F32)],
        semantics=("arbitrary",), stages=stages)


def _loss_and_final_norm_bwd(h1, d2, tgt, g, name):
    T, D = h1.shape
    tt = _tile(T, 128, SUBLANES)

    def body(h1_ref, d2_ref, t_ref, g_ref, dh_ref, dhb_ref, dg_ref, loss_ref):
        @pl.when(pl.program_id(0) == 0)
        def _():
            dg_ref[...] = jnp.zeros_like(dg_ref)
            loss_ref[...] = jnp.zeros_like(loss_ref)

        hv = h1_ref[...] + d2_ref[...]
        gv = g_ref[...]
        inv = _row_inv(hv)
        n = hv * inv
        err = n * gv - t_ref[...]
        loss_ref[...] += 0.5 * jnp.sum(jnp.mean(err * err, axis=-1, keepdims=True))
        dy = err * (1.0 / D)
        dg_ref[...] += jnp.sum(dy * n, axis=0, keepdims=True)
        dn = dy * gv
        dh = inv * (dn - n * jnp.mean(dn * n, axis=-1, keepdims=True))
        dh_ref[...] = dh
        dhb_ref[...] = dh.astype(BF16)

    row = pl.BlockSpec((tt, D), lambda i: (i, 0))
    vec = pl.BlockSpec((1, D), lambda i: (0, 0))
    one = pl.BlockSpec((1, LANES), lambda i: (0, 0))
    return _call(
        body, name=name, args=[h1, d2, tgt, g], grid=(T // tt,), in_specs=[row, row, row, vec],
        out_specs=[row, row, vec, one],
        out_shape=[jax.ShapeDtypeStruct((T, D), F32), jax.ShapeDtypeStruct((T, D), BF16),
                   jax.ShapeDtypeStruct((1, D), F32), jax.ShapeDtypeStruct((1, LANES), F32)],
        semantics=("arbitrary",))[0]


def _gelu(x):
    th = jnp.tanh(GELU_K * (x + GELU_A * (x * x * x)))
    return 0.5 * x * (1.0 + th), th


def _gelu_grad(x, th):
    return 0.5 * (1.0 + th) + 0.5 * x * (1.0 - th * th) * (GELU_K * (1.0 + 3.0 * GELU_A * (x * x)))


def _shift_rows(cur, prev_rows, k):
    rolled = pltpu.roll(cur, k, 0)
    row = lax.broadcasted_iota(jnp.int32, cur.shape, 0)
    out = rolled
    for r in range(k):
        out = jnp.where(row == r, prev_rows[SUBLANES - k + r:SUBLANES - k + r + 1, :], out)
    return out


def _unshift_rows(cur, next_rows, k):
    n = cur.shape[0]
    rolled = pltpu.roll(cur, n - k, 0)
    row = lax.broadcasted_iota(jnp.int32, cur.shape, 0)
    out = rolled
    for r in range(k):
        out = jnp.where(row == n - k + r, next_rows[r:r + 1, :], out)
    return out


def _mixer_specs(W, blk, halo):
    cols = [pl.BlockSpec((CHUNK, W), functools.partial(lambda i, col: (blk(i), col), col=col)) for col in range(5)]
    halos = [pl.BlockSpec((SUBLANES, W), functools.partial(lambda i, col: (halo(i), col), col=col)) for col in (1, 2)]
    return cols, halos


def _mixers_fwd(proj, conv_w, wm, bias_e, g_a, g_b, seq_len, name, stages=()):
    T, W5 = proj.shape
    W = W5 // 5
    H = W // HEAD_DIM
    per_seq = seq_len // CHUNK
    rb = CHUNK // SUBLANES
    cols, halos = _mixer_specs(W, lambda i: i, lambda i: jnp.maximum(i * rb - 1, 0))

    def body(b_ref, c_ref, hin_ref, u_ref, v_ref, ch_ref, hh_ref, cw_ref, wm_ref, be_ref, ga_ref, gb_ref, y_ref, s_ref):
        first = (pl.program_id(0) % per_seq) == 0
        hc = c_ref[...] * hin_ref[...]
        hc_prev = jnp.where(first, 0.0, ch_ref[...] * hh_ref[...])
        cw = cw_ref[...]
        ya = b_ref[...] * (cw[0:1, :] * _shift_rows(hc, hc_prev, 2) + cw[1:2, :] * _shift_rows(hc, hc_prev, 1)
                           + cw[2:3, :] * hc)
        y_ref[:, 0:W] = (ya * _row_inv(ya) * ga_ref[...]).astype(BF16)
        gu, _ = _gelu(u_ref[...])
        gv, _ = _gelu(v_ref[...])
        gvb = gv.astype(BF16)
        for hd in range(H):
            sl = slice(hd * HEAD_DIM, (hd + 1) * HEAD_DIM)
            s_ref[:, sl] = jnp.dot(wm_ref[hd], gvb[:, sl], preferred_element_type=F32)
        yb = gu * (s_ref[...] + be_ref[...])
        y_ref[:, W:2 * W] = (yb * _row_inv(yb) * gb_ref[...]).astype(BF16)

    full = lambda shape: pl.BlockSpec(shape, lambda i: (0,) * len(shape))
    outs, carried = _call(
        body, name=name, args=[proj, proj, proj, proj, proj, proj, proj, conv_w, wm, bias_e, g_a, g_b],
        grid=(T // CHUNK,),
        in_specs=cols + halos + [full((CONV_K, W)), full((H, CHUNK, CHUNK)), full((CHUNK, W)), full((1, W)), full((1, W))],
        out_specs=[pl.BlockSpec((CHUNK, 2 * W), lambda i: (i, 0))], out_shape=[jax.ShapeDtypeStruct((T, 2 * W), BF16)],
        scratch_shapes=[pltpu.VMEM((CHUNK, W), F32)], semantics=("parallel",), stages=stages)
    return outs[0], carried


def _mixers_bwd(dy, proj, conv_w, wm, wmt, bias_e, g_a, g_b, head_onehot, seq_len, name, stages=()):
    T, W5 = proj.shape
    W = W5 // 5
    H = W // HEAD_DIM
    nb = T // CHUNK
    per_seq = seq_len // CHUNK
    rb = CHUNK // SUBLANES
    blk = lambda i: nb - 1 - i
    cols, halos = _mixer_specs(W, blk, lambda i: jnp.maximum(blk(i) * rb - 1, 0))

    def body(dy_ref, b_ref, c_ref, hin_ref, u_ref, v_ref, ch_ref, hh_ref, cw_ref, wm_ref, wmt_ref, be_ref, ga_ref,
             gb_ref, oh_ref, dp_ref, dcw_ref, dga_ref, dgb_ref, dws_ref, dbt_ref, carry_ref, s_ref, dgv_ref):
        i = pl.program_id(0)
        j = nb - 1 - i

        @pl.when(i == 0)
        def _():
            for r in (dcw_ref, dga_ref, dgb_ref, dws_ref, dbt_ref, carry_ref):
                r[...] = jnp.zeros_like(r)

        first = (j % per_seq) == 0
        last = (j % per_seq) == per_seq - 1
        b, c, hin = b_ref[...], c_ref[...], hin_ref[...]
        cw = cw_ref[...]
        hc = c * hin
        hc_prev = jnp.where(first, 0.0, ch_ref[...] * hh_ref[...])
        hc1 = _shift_rows(hc, hc_prev, 1)
        hc2 = _shift_rows(hc, hc_prev, 2)
        conv = cw[0:1, :] * hc2 + cw[1:2, :] * hc1 + cw[2:3, :] * hc
        ya = b * conv
        inv_a = _row_inv(ya)
        na = ya * inv_a
        do_a = dy_ref[:, 0:W]
        dga_ref[...] += jnp.sum(do_a * na, axis=0, keepdims=True)
        dna = do_a * ga_ref[...]
        dya = inv_a * (dna - na * jnp.mean(dna * na, axis=-1, keepdims=True))
        dp_ref[:, 0:W] = (dya * conv).astype(BF16)
        dconv = dya * b
        dcw_ref[0:1, :] += jnp.sum(dconv * hc2, axis=0, keepdims=True)
        dcw_ref[1:2, :] += jnp.sum(dconv * hc1, axis=0, keepdims=True)
        dcw_ref[2:3, :] += jnp.sum(dconv * hc, axis=0, keepdims=True)
        nxt = jnp.where(last, 0.0, carry_ref[...])
        dhc = cw[2:3, :] * dconv + cw[1:2, :] * _unshift_rows(dconv, nxt, 1) + cw[0:1, :] * _unshift_rows(dconv, nxt, 2)
        carry_ref[...] = dconv[0:SUBLANES, :]
        dp_ref[:, W:2 * W] = (dhc * hin).astype(BF16)
        dp_ref[:, 2 * W:3 * W] = (dhc * c).astype(BF16)
        u, v = u_ref[...], v_ref[...]
        gu, thu = _gelu(u)
        gv, thv = _gelu(v)
        gvb = gv.astype(BF16)
        for hd in range(H):
            sl = slice(hd * HEAD_DIM, (hd + 1) * HEAD_DIM)
            s_ref[:, sl] = jnp.dot(wm_ref[hd], gvb[:, sl], preferred_element_type=F32)
        s = s_ref[...] + be_ref[...]
        yb = gu * s
        inv_b = _row_inv(yb)
        nbv = yb * inv_b
        do_b = dy_ref[:, W:2 * W]
        dgb_ref[...] += jnp.sum(do_b * nbv, axis=0, keepdims=True)
        dnb = do_b * gb_ref[...]
        dyb = inv_b * (dnb - nbv * jnp.mean(dnb * nbv, axis=-1, keepdims=True))
        dp_ref[:, 3 * W:4 * W] = (dyb * s * _gelu_grad(u, thu)).astype(BF16)
        dsb = (dyb * gu).astype(BF16)
        dbt_ref[...] += jnp.dot(dsb, oh_ref[...], preferred_element_type=F32)
        for hd in range(H):
            sl = slice(hd * HEAD_DIM, (hd + 1) * HEAD_DIM)
            dws_ref[hd] += lax.dot_general(dsb[:, sl], gvb[:, sl], (((1,), (1,)), ((), ())), preferred_element_type=F32)
            dgv_ref[:, sl] = jnp.dot(wmt_ref[hd], dsb[:, sl], preferred_element_type=F32)
        dp_ref[:, 4 * W:5 * W] = (dgv_ref[...] * _gelu_grad(v, thv)).astype(BF16)

    full = lambda shape: pl.BlockSpec(shape, lambda i: (0,) * len(shape))
    return _call(
        body, name=name, grid=(nb,),
        args=[dy, proj, proj, proj, proj, proj, proj, proj, conv_w, wm, wmt, bias_e, g_a, g_b, head_onehot],
        in_specs=[pl.BlockSpec((CHUNK, 2 * W), lambda i: (blk(i), 0))] + cols + halos
        + [full((CONV_K, W)), full((H, CHUNK, CHUNK)), full((H, CHUNK, CHUNK)), full((CHUNK, W)), full((1, W)),
           full((1, W)), full((W, LANES))],
        out_specs=[pl.BlockSpec((CHUNK, 5 * W), lambda i: (blk(i), 0)), full((SUBLANES, W)), full((1, W)), full((1, W)),
                   full((H, CHUNK, CHUNK)), full((CHUNK, LANES))],
        out_shape=[jax.ShapeDtypeStruct((T, 5 * W), BF16), jax.ShapeDtypeStruct((SUBLANES, W), F32),
                   jax.ShapeDtypeStruct((1, W), F32), jax.ShapeDtypeStruct((1, W), F32),
                   jax.ShapeDtypeStruct((H, CHUNK, CHUNK), F32), jax.ShapeDtypeStruct((CHUNK, LANES), F32)],
        scratch_shapes=[pltpu.VMEM((SUBLANES, W), F32), pltpu.VMEM((CHUNK, W), F32), pltpu.VMEM((CHUNK, W), F32)],
        semantics=("arbitrary",), stages=stages)


def _cast_into_slot(w, chip, name):
    R, C = w.shape
    tr = _tile(R, 256, 16)

    def body(chip_ref, w_ref, o_ref):
        o_ref[...] = w_ref[...].astype(BF16)

    return pl.pallas_call(
        body, name=name,
        grid_spec=pltpu.PrefetchScalarGridSpec(
            num_scalar_prefetch=1, grid=(R // tr,),
            in_specs=[pl.BlockSpec((tr, C), lambda i, chip_ref: (i, 0))],
            out_specs=pl.BlockSpec((None, tr, C), lambda i, chip_ref: (chip_ref[0], i, 0))),
        out_shape=jax.ShapeDtypeStruct((N_CHIPS, R, C), BF16),
        compiler_params=_params(("parallel",)),
    )(chip, w)


def _rs_pair_add(dw, got, chip_core, name):
    S, R, C = dw.shape
    hr = R // 2
    tr = _tile(hr, 256, 16)
    nrb = hr // tr

    def body(cc_ref, dw_ref, got_ref, send_ref, own_ref):
        s = dw_ref[...].astype(F32) + got_ref[...].astype(F32)
        send_ref[...] = s.astype(BF16)

        @pl.when(pl.program_id(1) == cc_ref[0])
        def _():
            own_ref[...] = s

    return pl.pallas_call(
        body, name=name,
        grid_spec=pltpu.PrefetchScalarGridSpec(
            num_scalar_prefetch=1, grid=(nrb, S),
            in_specs=[pl.BlockSpec((None, tr, C), lambda i, q, cc: (q, cc[1] * nrb + i, 0)),
                      pl.BlockSpec((None, tr, C), lambda i, q, cc: (q, i, 0))],
            out_specs=[pl.BlockSpec((None, tr, C), lambda i, q, cc: (q, i, 0)),
                       pl.BlockSpec((tr, C), lambda i, q, cc: (i, 0))]),
        out_shape=[jax.ShapeDtypeStruct((S, hr, C), BF16), jax.ShapeDtypeStruct((hr, C), F32)],
        compiler_params=_params(("parallel", "arbitrary")),
    )(chip_core, dw, got)


def _rs_final_add(own, got, chip_core, name):
    hr, C = own.shape
    tr = _tile(hr, 256, 16)
    nrb = hr // tr

    def body(cc_ref, own_ref, got_ref, o_ref):
        o_ref[...] = ((own_ref[...] + got_ref[0].astype(F32)) + got_ref[1].astype(F32)) + got_ref[2].astype(F32)

    return pl.pallas_call(
        body, name=name,
        grid_spec=pltpu.PrefetchScalarGridSpec(
            num_scalar_prefetch=1, grid=(nrb,),
            in_specs=[pl.BlockSpec((tr, C), lambda i, cc: (i, 0)), pl.BlockSpec((3, tr, C), lambda i, cc: (0, i, 0))],
            out_specs=pl.BlockSpec((tr, C), lambda i, cc: (cc[1] * nrb + i, 0))),
        out_shape=jax.ShapeDtypeStruct((2 * hr, C), F32),
        compiler_params=_params(("parallel",)),
    )(chip_core, own, got)


def _adamw_math(w, g, m, v):
    m2 = ADAM_B1 * m + (1.0 - ADAM_B1) * g
    v2 = ADAM_B2 * v + (1.0 - ADAM_B2) * (g * g)
    delta = -ADAM_LR * ((m2 / ADAM_C1) / (jnp.sqrt(v2 / ADAM_C2) + ADAM_EPS) + ADAM_WD * w)
    return delta, m2, v2


def _adamw(w, g, m, v, name):
    R, C = w.shape
    tr = _tile(R, max(SUBLANES, (256 * 1024) // C), SUBLANES)

    def body(w_ref, g_ref, m_ref, v_ref, g2_ref, d_ref, m2_ref, v2_ref):
        g = g_ref[...]
        g2_ref[...] = g
        d_ref[...], m2_ref[...], v2_ref[...] = _adamw_math(w_ref[...], g, m_ref[...], v_ref[...])

    blk = pl.BlockSpec((tr, C), lambda i: (i, 0))
    return _call(body, name=name, args=[w, g, m, v], grid=(R // tr,), in_specs=[blk] * 4, out_specs=[blk] * 4,
                 out_shape=[jax.ShapeDtypeStruct((R, C), F32)] * 4, semantics=("parallel",))[0]


def _all_gather_small(block, name):
    m_per, n = block.shape

    def body(x_ref, out_ref, send_sems, recv_sems, local_sem):
        x, y, c = _position()
        me, sibling = (x, y, c), (x, y, 1 - c)
        chips = _other_chips(x, y)

        def rows(px, py, pc):
            return out_ref.at[pl.ds((4 * px + 2 * py + pc) * m_per, m_per), :]

        def copy(k, blk, to, src=None):
            return pltpu.make_async_remote_copy(src_ref=rows(*blk) if src is None else src, dst_ref=rows(*blk),
                                                send_sem=send_sems.at[k], recv_sem=recv_sems.at[k], device_id=to,
                                                device_id_type=MESH)

        mine = pltpu.make_async_copy(x_ref, rows(*me), local_sem)
        mine.start()
        first = [copy(0, me, sibling, src=x_ref)]
        first += [copy(1 + j, me, (*chip, c), src=x_ref) for j, chip in enumerate(chips)]
        for cp in first:
            cp.start()
        passed = [copy(4 + j, (*chip, c), sibling) for j, chip in enumerate(chips)]
        for j, chip in enumerate(chips):
            copy(1 + j, (*chip, c), me).wait_recv()
            passed[j].start()
        copy(0, sibling, me).wait_recv()
        for j, chip in enumerate(chips):
            copy(4 + j, (*chip, 1 - c), me).wait_recv()
        for cp in first + passed:
            cp.wait_send()
        mine.wait()

    return pl.pallas_call(
        body, name=name,
        in_specs=[pl.BlockSpec(memory_space=pltpu.VMEM)],
        out_specs=pl.BlockSpec(memory_space=pltpu.VMEM),
        out_shape=jax.ShapeDtypeStruct((N_DEV * m_per, n), block.dtype),
        scratch_shapes=[pltpu.SemaphoreType.DMA((7,)), pltpu.SemaphoreType.DMA((7,)), pltpu.SemaphoreType.DMA],
        compiler_params=pltpu.CompilerParams(vmem_limit_bytes=VMEM_LIMIT_V7X, has_side_effects=True),
    )(block)


def _sum_and_adamw_small(gathered, w, m, v, name):
    rows, n = w.shape
    tr = _tile(rows, 32, SUBLANES)

    def body(p_ref, w_ref, m_ref, v_ref, g_ref, d_ref, m2_ref, v2_ref):
        g = p_ref[0]
        for d in range(1, N_DEV):
            g = g + p_ref[d]
        g_ref[...] = g
        d_ref[...], m2_ref[...], v2_ref[...] = _adamw_math(w_ref[...], g, m_ref[...], v_ref[...])

    blk = pl.BlockSpec((tr, n), lambda i: (i, 0))
    return pl.pallas_call(
        body, name=name, grid=(rows // tr,),
        in_specs=[pl.BlockSpec((N_DEV, tr, n), lambda i: (0, i, 0))] + [blk] * 3,
        out_specs=[blk] * 4,
        out_shape=[jax.ShapeDtypeStruct((rows, n), F32)] * 4,
        compiler_params=_params(("parallel",)),
    )(gathered.reshape(N_DEV, rows, n), w, m, v)


def _pad_rows(a):
    pad = (-a.shape[0]) % SUBLANES
    return jnp.pad(a, ((0, pad), (0, 0))) if pad else a


class _SmallPack:
    def __init__(self, W, D, H, chip):
        self.W, self.D, self.H, self.chip = W, D, H, chip
        self.offsets = {}
        self.rows = 0

    def pack(self, pieces):
        out = []
        self.offsets, self.rows = {}, 0
        for name, a in pieces:
            a = _pad_rows(a.astype(F32))
            self.offsets[name] = (self.rows, a.shape[0])
            self.rows += a.shape[0]
            out.append(a)
        return jnp.concatenate(out, axis=0)

    def piece(self, packed, name):
        start, n = self.offsets[name]
        return packed[start:start + n]


def _bias_rows(b, W):
    bt = jnp.pad(b.T, ((0, 0), (0, LANES - b.shape[0])))
    return bt.reshape(-1, W)


def _bias_from_rows(rows, H):
    return rows.reshape(-1)[:CHUNK * LANES].reshape(CHUNK, LANES)[:, :H].T


def kernel(x, mix_norm_g, w_in, conv_w, spatial_w, spatial_b, conv_out_norm_g, gmlp_out_norm_g, w_out, mlp_norm_g, w_up, w_down, final_norm_g, loss_target, m_mix_norm_g, m_w_in, m_conv_w, m_spatial_w, m_spatial_b, m_conv_out_norm_g, m_gmlp_out_norm_g, m_w_out, m_mlp_norm_g, m_w_up, m_w_down, m_final_norm_g, v_mix_norm_g, v_w_in, v_conv_w, v_spatial_w, v_spatial_b, v_conv_out_norm_g, v_gmlp_out_norm_g, v_w_out, v_mlp_norm_g, v_w_up, v_w_down, v_final_norm_g):
    Bl, S, D = x.shape
    T = Bl * S
    W = conv_out_norm_g.shape[-1]
    H = W // HEAD_DIM
    Wl = conv_w.shape[-1]
    xi, yi, ci = _position()
    chip = (2 * xi + yi).astype(jnp.int32)
    chip_arr = chip.reshape(1)
    chip_core = jnp.stack([chip, ci.astype(jnp.int32)])

    x2 = x.reshape(T, D)
    tgt2 = loss_target.reshape(T, D)

    s_in = _cast_into_slot(w_in[0], chip_arr, "cast_w_in")
    up_rows = w_up.shape[1] // 2
    up_cuts = [0] + [up_rows * pct // 100 // 16 * 16 for pct in (43, 57, 82)] + [up_rows]
    up_part = [(lo, hi - lo) for lo, hi in zip(up_cuts[:-1], up_cuts[1:])]

    causal = jnp.tril(jnp.ones((CHUNK, CHUNK), dtype=bool))
    wm = jnp.where(causal[None], spatial_w[0], 0.0).astype(BF16)
    wmt = jnp.swapaxes(wm, 1, 2)
    bias_e = jnp.repeat(spatial_b[0].T, HEAD_DIM, axis=1)
    conv_full = lax.dynamic_update_slice(jnp.zeros((CONV_K, W), F32), conv_w[0], (0, chip * Wl))
    head_onehot = (jnp.arange(W)[:, None] // HEAD_DIM == jnp.arange(LANES)[None, :]).astype(BF16)
    g_a, g_b = conv_out_norm_g, gmlp_out_norm_g

    xn, (s_out, s_up, s_down), ((g_in,), (conv_gathered,)) = _rmsnorm_fwd_and_casts(
        x2, mix_norm_g, [w_out[0], w_up[0], w_down[0]], chip_arr, "mix_norm_fwd",
        stages=[_GatherRows(s_in), _GatherSmall(_pad_rows(conv_full))])
    conv_w_all = conv_gathered.reshape(N_DEV, SUBLANES, W)[:, :CONV_K]
    conv_w_all = conv_w_all[0] + conv_w_all[2] + conv_w_all[4] + conv_w_all[6]
    proj, ((g_out,), (g_up,)) = _matmul(xn, g_in, mode="nn", name="proj_fwd", tm=1024, tn=512, tk=4096,
                                        out_dtypes=[F32], b_shard="n",
                                        stages=[_GatherRows(s_out), _GatherRows(s_up, *up_part[0])])
    g_out = g_out.reshape(-1, D)
    y, ((g_up,),) = _mixers_fwd(proj, conv_w_all, wm, bias_e, g_a, g_b, S, "mixers_fwd",
                                stages=[_GatherRows(g_up, *up_part[1])])
    h1, ((g_up,),) = _matmul(y, g_out, mode="nn", name="out_proj_fwd", tm=1024, tn=512, tk=4096, out_dtypes=[F32],
                             epilogue=_ep_residual, extras=(x2,), stages=[_GatherRows(g_up, *up_part[2])])
    xn2, ((g_up,),) = _rmsnorm_fwd(h1, mlp_norm_g, "mlp_norm_fwd", stages=[_GatherRows(g_up, *up_part[3])])
    (r, a), ((g_down,),) = _matmul(xn2, g_up, mode="nn", name="up_fwd", tm=1024, tn=1024, tk=4096, n_sub=2,
                                   out_dtypes=[BF16, BF16], epilogue=_ep_relu2, b_shard="n",
                                   stages=[_GatherRows(s_down)])
    g_down = g_down.reshape(-1, D)
    d2, _ = _matmul(a, g_down, mode="nn", name="down_fwd", tm=2048, tn=1024, tk=1024, out_dtypes=[F32])
    dh2, dh2b, d_final_g, loss_part = _loss_and_final_norm_bwd(h1, d2, tgt2, final_norm_g.reshape(1, D),
                                                               "loss_final_norm")

    def rs_adds(dw, got, tag):
        return _rs_pair_add(dw, got, chip_core, f"rs_pair_add_{tag}")

    dw_down, _ = _matmul(a, dh2b, mode="tn", name="down_dw", tm=1024, tn=1024, tk=4096, n_sub=2, out_dtypes=[BF16])
    dw_down = dw_down.reshape(N_CHIPS, -1, D)
    dpre, ((got_down,),) = _matmul(dh2b, g_down, mode="nt", name="down_dx", tm=1024, tn=1024, tk=4096, n_sub=2,
                                   out_dtypes=[BF16], epilogue=_ep_relu2_bwd, extras=(r,),
                                   stages=[_PairExchange(dw_down)])
    part_down, own_down = rs_adds(dw_down, got_down, "w_down")
    rows_down = part_down.shape[1]
    down_a = rows_down * 3 // 4 // 16 * 16
    dw_up, ((landed_down,),) = _matmul(xn2, dpre, mode="tn", name="up_dw", tm=1024, tn=1024, tk=4096, n_sub=2,
                                       out_dtypes=[BF16], out_shard=True,
                                       stages=[_ChipExchange(part_down, None, 0, down_a)])
    dxn2, ((got_up,), (landed_down,)) = _matmul(
        dpre, g_up, mode="nt", name="up_dx", tm=2048, tn=1024, tk=1024, out_dtypes=[F32], b_shard="k",
        stages=[_PairExchange(dw_up), _ChipExchange(part_down, landed_down, down_a, rows_down - down_a)])
    half_down = _rs_final_add(own_down, landed_down, chip_core, "rs_final_add_w_down")
    part_up, own_up = rs_adds(dw_up, got_up, "w_up")
    (dh1, dh1b, d_mlp_g), ((grad_down,),) = _rmsnorm_bwd(dxn2, h1, mlp_norm_g, dh2, "mlp_norm_bwd",
                                                        stages=[_HalfExchange(half_down)])
    rows_up = part_up.shape[1]
    q_up = rows_up // 4 // 16 * 16
    dy, ((landed_up,),) = _matmul(dh1b, g_out, mode="nt", name="out_proj_dx", tm=1024, tn=512, tk=4096,
                                  out_dtypes=[F32], stages=[_ChipExchange(part_up, None, 0, q_up)])
    (dproj, d_conv, d_ga, d_gb, d_ws, d_bt), ((landed_up,),) = _mixers_bwd(
        dy, proj, conv_w_all, wm, wmt, bias_e, g_a, g_b, head_onehot, S, "mixers_bwd",
        stages=[_ChipExchange(part_up, landed_up, q_up, q_up)])
    dw_in, ((landed_up,),) = _matmul(
        xn, dproj, mode="tn", name="proj_dw", tm=1024, tn=512, tk=4096, out_dtypes=[BF16], out_shard=True,
        stages=[_ChipExchange(part_up, landed_up, 2 * q_up, rows_up - 2 * q_up)])
    half_up = _rs_final_add(own_up, landed_up, chip_core, "rs_final_add_w_up")
    pack = _SmallPack(W, D, H, chip)
    loss_row = jnp.pad(loss_part[:, :1], ((0, 0), (0, W - 1)))
    g_part = pack.pack([("spatial_w", (d_ws * causal.astype(F32)[None]).reshape(-1, W)), ("conv_w", d_conv),
                        ("mlp_norm_g", d_mlp_g.reshape(-1, W)), ("final_norm_g", d_final_g.reshape(-1, W)),
                        ("conv_out_norm_g", d_ga), ("gmlp_out_norm_g", d_gb), ("spatial_b", d_bt.reshape(-1, W)),
                        ("loss", loss_row)])
    dw_out, ((got_in,), (grad_up,), (g_all,)) = _matmul(
        y, dh1b, mode="tn", name="out_proj_dw", tm=1024, tn=1024, tk=4096, n_sub=2, out_dtypes=[BF16],
        stages=[_PairExchange(dw_in), _HalfExchange(half_up), _GatherSmall(g_part)])
    dw_out = dw_out.reshape(N_CHIPS, -1, D)
    part_in, own_in = rs_adds(dw_in, got_in, "w_in")
    dxn, ((landed_in,), (got_out,)) = _matmul(
        dproj, g_in, mode="nt", name="proj_dx", tm=2048, tn=1024, tk=1280, out_dtypes=[F32], b_shard="k",
        stages=[_ChipExchange(part_in), _PairExchange(dw_out)])
    part_out, own_out = rs_adds(dw_out, got_out, "w_out")
    half_in = _rs_final_add(own_in, landed_in, chip_core, "rs_final_add_w_in")
    (grad_x, _unused, d_mix_g), ((landed_out,), (grad_in,)) = _rmsnorm_bwd(
        dxn, x2, mix_norm_g, dh1, "mix_norm_bwd", stages=[_ChipExchange(part_out), _HalfExchange(half_in)])
    half_out = _rs_final_add(own_out, landed_out, chip_core, "rs_final_add_w_out")
    ((grad_out,),) = _run_stages([_HalfExchange(half_out)], "rs_half_exchange_w_out")
    big = {"w_down": _adamw(w_down[0], grad_down, m_w_down[0], v_w_down[0], "adamw_w_down"),
           "w_up": _adamw(w_up[0], grad_up, m_w_up[0], v_w_up[0], "adamw_w_up"),
           "w_in": _adamw(w_in[0], grad_in, m_w_in[0], v_w_in[0], "adamw_w_in"),
           "w_out": _adamw(w_out[0], grad_out, m_w_out[0], v_w_out[0], "adamw_w_out")}
    big = {k: [t[None] for t in v] for k, v in big.items()}

    def small(conv, sw, sb, ga, gb, mlp, fin):
        return pack.pack([("spatial_w", sw.reshape(-1, W)), ("conv_w", conv), ("mlp_norm_g", mlp.reshape(-1, W)),
                          ("final_norm_g", fin.reshape(-1, W)), ("conv_out_norm_g", ga.reshape(-1, W)),
                          ("gmlp_out_norm_g", gb.reshape(-1, W)), ("spatial_b", _bias_rows(sb, W)),
                          ("loss", jnp.zeros((1, W), F32))])

    def full_conv(cw):
        return lax.dynamic_update_slice(jnp.zeros((CONV_K, W), F32), cw[0], (0, chip * Wl))

    def mix_rows(a):
        return _pad_rows(a.reshape(-1, W))

    w_s = small(full_conv(conv_w), spatial_w, spatial_b[0], conv_out_norm_g, gmlp_out_norm_g, mlp_norm_g, final_norm_g)
    m_s = small(full_conv(m_conv_w), m_spatial_w, m_spatial_b[0], m_conv_out_norm_g, m_gmlp_out_norm_g, m_mlp_norm_g,
                m_final_norm_g)
    v_s = small(full_conv(v_conv_w), v_spatial_w, v_spatial_b[0], v_conv_out_norm_g, v_gmlp_out_norm_g, v_mlp_norm_g,
                v_final_norm_g)
    small_outs = _sum_and_adamw_small(g_all, w_s, m_s, v_s, "sum_adamw_small")
    mix_all = _all_gather_small(mix_rows(d_mix_g), "all_gather_mix_norm_grad")
    mix_outs = _sum_and_adamw_small(mix_all, mix_rows(mix_norm_g), mix_rows(m_mix_norm_g), mix_rows(v_mix_norm_g),
                                    "sum_adamw_mix_norm")

    def unpack(kind, name):
        if name == "mix_norm_g":
            return mix_outs[kind].reshape(-1)[:D].reshape(1, D)
        rows = pack.piece(small_outs[kind], name)
        if name == "spatial_w":
            return rows.reshape(1, H, CHUNK, CHUNK)
        if name == "conv_w":
            return lax.dynamic_slice(rows[:CONV_K], (0, chip * Wl), (CONV_K, Wl))[None]
        if name == "spatial_b":
            return _bias_from_rows(rows, H)[None]
        if name == "final_norm_g":
            return rows.reshape(-1)[:D]
        n = D if name == "mlp_norm_g" else W
        return rows.reshape(-1)[:n].reshape(1, n)

    loss = pack.piece(small_outs[0], "loss")[0, 0]
    order = ["mix_norm_g", "w_in", "conv_w", "spatial_w", "spatial_b", "conv_out_norm_g", "gmlp_out_norm_g", "w_out",
             "mlp_norm_g", "w_up", "w_down", "final_norm_g"]
    outs = [loss, grad_x.reshape(Bl, S, D)]
    for kind in range(4):
        for name in order:
            outs.append(big[name][kind] if name in big else unpack(kind, name))
    return tuple(outs)
```

```python
import functools
import math

import jax
import jax.numpy as jnp
from jax import lax
from jax.experimental import pallas as pl
from jax.experimental.pallas import tpu as pltpu
from jax.experimental.pallas import tpu_sc as plsc

F32 = jnp.float32
BF16 = jnp.bfloat16
MESH = pl.DeviceIdType.MESH

NORM_EPS = 1e-5
HEAD_DIM = 128
CHUNK = 128
CONV_K = 3
N_CHIPS = 4
N_DEV = 8

ADAM_LR = 0.001
ADAM_B1 = 0.9
ADAM_B2 = 0.999
ADAM_EPS = 1e-08
ADAM_WD = 0.01
ADAM_STEP = 10
ADAM_C1 = 1.0 - ADAM_B1 ** ADAM_STEP
ADAM_C2 = 1.0 - ADAM_B2 ** ADAM_STEP

GELU_K = math.sqrt(2.0 / math.pi)
GELU_A = 0.044715

VMEM_LIMIT_V7X = 56 * 1024 * 1024
SUBLANES = 8
LANES = 128
SC_LANES = 16


def _tile(dim, target, mult=LANES):
    if dim <= target:
        return dim
    t = (target // mult) * mult
    while t > mult and dim % t:
        t -= mult
    assert dim % t == 0, (dim, target, mult)
    return t


def _params(sem=None):
    return pltpu.CompilerParams(dimension_semantics=sem, vmem_limit_bytes=VMEM_LIMIT_V7X)


class _Stage:
    bufs = ()
    n_sems = 0
    MIDDLE_AT = 0.6
    base = 0

    def start(self, refs, send, recv):
        raise NotImplementedError

    def middle(self, refs, send, recv):
        pass

    def finish(self, refs, send, recv):
        raise NotImplementedError


def _position():
    return lax.axis_index("x"), lax.axis_index("y"), lax.axis_index("c")


def _other_chips(x, y):
    return [(1 - x, y), (x, 1 - y), (1 - x, 1 - y)]


def _remote(src, dst, send, recv, k, to):
    return pltpu.make_async_remote_copy(src_ref=src, dst_ref=dst, send_sem=send.at[k], recv_sem=recv.at[k],
                                        device_id=to, device_id_type=MESH)


def _call(body, *, name, args, in_specs, out_specs, out_shape, grid=(), scratch_shapes=(), semantics=None, stages=(),
          prefetch=None):
    n_in, n_out, n_scratch = len(args), len(out_shape), len(scratch_shapes)
    n_pre = 0 if prefetch is None else 1
    any_spec = pl.BlockSpec(memory_space=pl.ANY)
    extra_args, extra_out, aliases, layout = [], [], {}, []
    for st in stages:
        where = []
        for kind, buf in st.bufs:
            if kind in ("in", "alias"):
                extra_args.append(buf)
                pos_in = n_in + len(extra_args) - 1
            if kind in ("out", "alias"):
                extra_out.append(jax.ShapeDtypeStruct(buf.shape, buf.dtype))
                pos_out = n_out + len(extra_out) - 1
            if kind == "alias":
                aliases[n_pre + pos_in] = pos_out
            where.append(("in", pos_in) if kind == "in" else ("out", pos_out))
        layout.append(where)
    n_sems = sum(st.n_sems for st in stages)
    n_xin, n_xout = len(extra_args), len(extra_out)

    def wrapped(*refs):
        pre, refs = refs[:n_pre], refs[n_pre:]
        ins = refs[:n_in + n_xin]
        outs = refs[n_in + n_xin:n_in + n_xin + n_out + n_xout]
        scratch = refs[n_in + n_xin + n_out + n_xout:]
        main = pre + ins[:n_in] + outs[:n_out] + scratch[:n_scratch]
        if not stages:
            body(*main)
            return
        send, recv = scratch[n_scratch], scratch[n_scratch + 1]
        step, n_steps = 0, 1
        for d, g in enumerate(grid):
            step = step * g + pl.program_id(d)
            n_steps *= g
        base, views = 0, []
        for st, where in zip(stages, layout):
            st_refs = [ins[p] if side == "in" else outs[p] for side, p in where]
            st.base = base
            views.append((st, st_refs, send, recv))
            base += st.n_sems

        def starts():
            for st, r, s, v in views:
                st.start(r, s, v)

        def middles():
            for st, r, s, v in views:
                st.middle(r, s, v)

        def finishes():
            for st, r, s, v in views:
                st.finish(r, s, v)

        if not grid:
            starts()
            body(*main)
            middles()
            finishes()
        else:
            pl.when(step == 0)(starts)
            body(*main)
            pl.when(step == min(int(n_steps * _Stage.MIDDLE_AT), n_steps - 1))(middles)
            pl.when(step == n_steps - 1)(finishes)

    sems = [pltpu.SemaphoreType.DMA((n_sems,)), pltpu.SemaphoreType.DMA((n_sems,))] if stages else []
    specs = dict(in_specs=list(in_specs) + [any_spec] * n_xin, out_specs=list(out_specs) + [any_spec] * n_xout,
                 scratch_shapes=list(scratch_shapes) + sems)
    if prefetch is None:
        kw = dict(specs, **(dict(grid=grid) if grid else {}))
    else:
        kw = dict(grid_spec=pltpu.PrefetchScalarGridSpec(num_scalar_prefetch=1, grid=grid, **specs))
    res = pl.pallas_call(
        wrapped, name=name,
        out_shape=list(out_shape) + extra_out,
        input_output_aliases=aliases,
        compiler_params=pltpu.CompilerParams(
            dimension_semantics=("arbitrary",) * len(grid) if stages and grid else semantics,
            vmem_limit_bytes=VMEM_LIMIT_V7X, has_side_effects=bool(stages)),
        **kw,
    )(*([] if prefetch is None else [prefetch]), *args, *extra_args)
    main_res, stage_res, pos = list(res[:n_out]), [], n_out
    for st in stages:
        k = sum(kind in ("out", "alias") for kind, _ in st.bufs)
        stage_res.append(list(res[pos:pos + k]))
        pos += k
    return main_res, stage_res


def _run_stages(stages, name):
    return _call(lambda: None, name=name, args=[], in_specs=[], out_specs=[], out_shape=[], stages=stages)[1]


class _GatherRows(_Stage):
    n_sems = 7

    def __init__(self, g, lo=0, n=None):
        self.hr = g.shape[1] // 2
        self.lo, self.n = lo, (self.hr if n is None else n)
        self.n0 = self.n // 2 // 16 * 16
        self.bufs = [("alias", g)]

    def _copy(self, g_ref, send, recv, k, chip_xy, half, to, lo=0, n=None):
        n = self.n if n is None else n
        blk = g_ref.at[2 * chip_xy[0] + chip_xy[1], pl.ds(half * self.hr + self.lo + lo, n), :]
        return _remote(blk, blk, send, recv, self.base + k, to)

    def _plan(self, g, send, recv):
        x, y, c = _position()
        me, sib = (x, y, c), (x, y, 1 - c)
        cx, cy, cd = _other_chips(x, y)
        n0, n1 = self.n0, self.n - self.n0
        mine = [((x, y), c, (*cx, c)), ((x, y), c, (*cy, c)), (cx, c, sib), (cy, c, sib),
                (cx, c, (*cy, c), 0, n0), (cy, c, (*cx, c), n0, n1), (cd, c, sib)]
        theirs = [(cx, c, me), (cy, c, me), (cx, 1 - c, me), (cy, 1 - c, me),
                  (cd, c, me, 0, n0), (cd, c, me, n0, n1), (cd, 1 - c, me)]
        return (lambda k: self._copy(g, send, recv, k, *mine[k])), (lambda k: self._copy(g, send, recv, k, *theirs[k]))

    def start(self, refs, send, recv):
        mine, _ = self._plan(refs[0], send, recv)
        mine(0).start()
        mine(1).start()

    def middle(self, refs, send, recv):
        mine, theirs = self._plan(refs[0], send, recv)
        theirs(0).wait_recv()
        mine(4).start()
        mine(2).start()
        theirs(1).wait_recv()
        mine(5).start()
        mine(3).start()

    def finish(self, refs, send, recv):
        mine, theirs = self._plan(refs[0], send, recv)
        theirs(4).wait_recv()
        theirs(5).wait_recv()
        mine(6).start()
        for k in (2, 3, 6):
            theirs(k).wait_recv()
        for k in range(self.n_sems):
            mine(k).wait_send()


class _GatherSmall(_Stage):
    n_sems = 8

    def __init__(self, block):
        self.m = block.shape[0]
        self.bufs = [("in", block), ("out", jax.ShapeDtypeStruct((N_DEV * self.m, block.shape[1]), block.dtype))]

    def _rows(self, out, px, py, pc):
        return out.at[pl.ds((4 * px + 2 * py + pc) * self.m, self.m), :]

    def _copy(self, refs, send, recv, k, blk, to, own=False):
        dst = self._rows(refs[1], *blk)
        return _remote(refs[0] if own else dst, dst, send, recv, self.base + k, to)

    def _local(self, refs, send):
        return pltpu.make_async_copy(refs[0], self._rows(refs[1], *_position()), send.at[self.base + 7])

    def start(self, refs, send, recv):
        x, y, c = _position()
        self._local(refs, send).start()
        self._copy(refs, send, recv, 0, (x, y, c), (x, y, 1 - c), own=True).start()
        for j, chip in enumerate(_other_chips(x, y)):
            self._copy(refs, send, recv, 1 + j, (x, y, c), (*chip, c), own=True).start()

    def finish(self, refs, send, recv):
        x, y, c = _position()
        me, sib, chips = (x, y, c), (x, y, 1 - c), _other_chips(x, y)
        for j, chip in enumerate(chips):
            self._copy(refs, send, recv, 1 + j, (*chip, c), me).wait_recv()
            self._copy(refs, send, recv, 4 + j, (*chip, c), sib).start()
        self._copy(refs, send, recv, 0, sib, me).wait_recv()
        for j, chip in enumerate(chips):
            self._copy(refs, send, recv, 4 + j, (*chip, 1 - c), me).wait_recv()
        self._copy(refs, send, recv, 0, me, sib, own=True).wait_send()
        for j, chip in enumerate(chips):
            self._copy(refs, send, recv, 1 + j, me, (*chip, c), own=True).wait_send()
            self._copy(refs, send, recv, 4 + j, (*chip, c), sib).wait_send()
        self._local(refs, send).wait()


class _PairExchange(_Stage):
    n_sems = 1

    def __init__(self, dw):
        S, R, C = dw.shape
        self.hr = R // 2
        self.bufs = [("in", dw), ("out", jax.ShapeDtypeStruct((S, self.hr, C), dw.dtype))]

    def _copy(self, refs, send, recv):
        x, y, c = _position()
        return _remote(refs[0].at[:, pl.ds((1 - c) * self.hr, self.hr), :], refs[1], send, recv, self.base,
                       (x, y, 1 - c))

    def start(self, refs, send, recv):
        self._copy(refs, send, recv).start()

    def finish(self, refs, send, recv):
        cp = self._copy(refs, send, recv)
        cp.wait_recv()
        cp.wait_send()


class _ChipExchange(_Stage):
    n_sems = 3

    def __init__(self, part, landed=None, lo=0, n=None):
        S, hr, C = part.shape
        self.lo, self.n = lo, (hr if n is None else n)
        self.bufs = [("in", part), ("out", jax.ShapeDtypeStruct((3, hr, C), part.dtype)) if landed is None
                     else ("alias", landed)]

    def _copies(self, refs, send, recv):
        x, y, c = _position()
        rows = pl.ds(self.lo, self.n)
        return [_remote(refs[0].at[2 * chip[0] + chip[1], rows, :], refs[1].at[j, rows, :], send, recv,
                        self.base + j, (*chip, c))
                for j, chip in enumerate(_other_chips(x, y))]

    def start(self, refs, send, recv):
        for cp in self._copies(refs, send, recv):
            cp.start()

    def finish(self, refs, send, recv):
        copies = self._copies(refs, send, recv)
        for cp in copies:
            cp.wait_recv()
        for cp in copies:
            cp.wait_send()


class _HalfExchange(_Stage):
    n_sems = 1

    def __init__(self, grad):
        self.hr = grad.shape[0] // 2
        self.bufs = [("alias", grad)]

    def start(self, refs, send, recv):
        x, y, c = _position()
        mine = refs[0].at[pl.ds(c * self.hr, self.hr), :]
        _remote(mine, mine, send, recv, self.base, (x, y, 1 - c)).start()

    def finish(self, refs, send, recv):
        x, y, c = _position()
        mine = refs[0].at[pl.ds(c * self.hr, self.hr), :]
        theirs = refs[0].at[pl.ds((1 - c) * self.hr, self.hr), :]
        _remote(theirs, theirs, send, recv, self.base, (x, y, 1 - c)).wait_recv()
        _remote(mine, mine, send, recv, self.base, (x, y, 1 - c)).wait_send()


def _matmul(a, b, *, mode, name, tm, tn, tk, out_dtypes, epilogue=None, extras=(), b_shard=None, out_shard=False,
            stages=(), n_sub=1):
    if mode == "tn":
        K, M = a.shape
    else:
        M, K = a.shape
    if b_shard == "n":
        S, Kb, Ns = b.shape
        N = S * Ns
    elif b_shard == "k":
        S, N, Ks = b.shape
        Kb = S * Ks
    elif mode == "nt":
        N, Kb = b.shape
    else:
        Kb, N = b.shape
    assert Kb == K, (name, a.shape, b.shape)
    tm, tn, tk = _tile(M, tm), _tile(N, tn), _tile(K, tk)
    if b_shard == "n" or out_shard:
        n_per = N // N_CHIPS
        tn = _tile(n_per, tn)
        njs = n_per // tn
    if b_shard == "k":
        tk = _tile(K // N_CHIPS, tk)
        nks = (K // N_CHIPS) // tk
    gm, gn, gk = M // tm, N // tn, K // tk
    if gk > 1 or tn % (n_sub * LANES):
        n_sub = 1

    if mode == "tn":
        a_spec = pl.BlockSpec((tk, tm), lambda i, j, k: (k, i))
        dims = (((0,), (0,)), ((), ()))
    else:
        a_spec = pl.BlockSpec((tm, tk), lambda i, j, k: (i, k))
        dims = (((1,), (1,)), ((), ())) if mode == "nt" else (((1,), (0,)), ((), ()))
    if b_shard == "n":
        b_spec = pl.BlockSpec((None, tk, tn), lambda i, j, k: (j // njs, k, j % njs))
    elif b_shard == "k":
        b_spec = pl.BlockSpec((None, tn, tk), lambda i, j, k: (k // nks, j, k % nks))
    elif mode == "nt":
        b_spec = pl.BlockSpec((tn, tk), lambda i, j, k: (j, k))
    else:
        b_spec = pl.BlockSpec((tk, tn), lambda i, j, k: (k, j))
    mn_spec = pl.BlockSpec((tm, tn), lambda i, j, k: (i, j))
    if out_shard:
        out_spec = pl.BlockSpec((None, tm, tn), lambda i, j, k: (j // njs, i, j % njs))
        out_shape = [jax.ShapeDtypeStruct((N_CHIPS, M, N // N_CHIPS), dt) for dt in out_dtypes]
    else:
        out_spec = mn_spec
        out_shape = [jax.ShapeDtypeStruct((M, N), dt) for dt in out_dtypes]
    n_extra, n_out = len(extras), len(out_dtypes)

    def finish_tile(acc, extra_refs, out_refs):
        if epilogue is None:
            for o in out_refs:
                o[...] = acc.astype(o.dtype)
        else:
            epilogue(acc, extra_refs, out_refs)

    def body(*refs):
        a_ref, b_ref = refs[0], refs[1]
        extra_refs = refs[2:2 + n_extra]
        out_refs = refs[2 + n_extra:2 + n_extra + n_out]

        def product():
            return lax.dot_general(a_ref[...], b_ref[...], dims, preferred_element_type=F32)

        if gk == 1:
            sub = tn // n_sub
            for h in range(n_sub):
                cols = slice(h * sub, (h + 1) * sub)
                b_part = b_ref[cols, :] if mode == "nt" else b_ref[:, cols]
                acc = lax.dot_general(a_ref[...], b_part, dims, preferred_element_type=F32)
                finish_tile(acc, [e.at[:, cols] for e in extra_refs], [o.at[:, cols] for o in out_refs])
            return
        acc_ref = refs[-1]
        k = pl.program_id(2)

        @pl.when(k == 0)
        def _():
            acc_ref[...] = product()

        @pl.when((k > 0) & (k < gk - 1))
        def _():
            acc_ref[...] += product()

        @pl.when(k == gk - 1)
        def _():
            finish_tile(acc_ref[...] + product(), extra_refs, out_refs)

    outs, carried = _call(
        body, name=name, args=[a, b, *extras], grid=(gm, gn, gk),
        in_specs=[a_spec, b_spec] + [mn_spec] * n_extra, out_specs=[out_spec] * n_out, out_shape=out_shape,
        scratch_shapes=[pltpu.VMEM((tm, tn), F32)] if gk > 1 else [],
        semantics=("parallel", "parallel", "arbitrary"), stages=stages)
    return (outs[0] if n_out == 1 else outs), carried


def _ep_residual(acc, extra_refs, out_refs):
    out_refs[0][...] = extra_refs[0][...] + acc


def _ep_relu2(acc, extra_refs, out_refs):
    r = jnp.maximum(acc, 0.0)
    out_refs[0][...] = r.astype(BF16)
    out_refs[1][...] = (r * r).astype(BF16)


def _ep_relu2_bwd(acc, extra_refs, out_refs):
    out_refs[0][...] = (acc * (2.0 * extra_refs[0][...].astype(F32))).astype(BF16)


def _row_inv(x):
    return lax.rsqrt(jnp.mean(x * x, axis=-1, keepdims=True) + NORM_EPS)


def _rmsnorm_fwd(x, g, name, stages=()):
    T, D = x.shape
    tt = _tile(T, 256, SUBLANES)

    def body(x_ref, g_ref, o_ref):
        xv = x_ref[...]
        o_ref[...] = (xv * _row_inv(xv) * g_ref[...]).astype(BF16)

    outs, carried = _call(
        body, name=name, args=[x, g], grid=(T // tt,),
        in_specs=[pl.BlockSpec((tt, D), lambda i: (i, 0)), pl.BlockSpec((1, D), lambda i: (0, 0))],
        out_specs=[pl.BlockSpec((tt, D), lambda i: (i, 0))], out_shape=[jax.ShapeDtypeStruct((T, D), BF16)],
        semantics=("parallel",), stages=stages)
    return outs[0], carried


def _rmsnorm_fwd_and_casts(x, g, weights, chip, name, stages=()):
    T, D = x.shape
    tt = _tile(T, 256, SUBLANES)
    n, nw = T // tt, len(weights)
    rows = [w.shape[0] // n for w in weights]
    assert all(r % 16 == 0 and r * n == w.shape[0] for r, w in zip(rows, weights))

    def body(chip_ref, x_ref, g_ref, *refs):
        w_refs, o_ref, slot_refs = refs[:nw], refs[nw], refs[nw + 1:]
        xv = x_ref[...]
        o_ref[...] = (xv * _row_inv(xv) * g_ref[...]).astype(BF16)
        for w_ref, s_ref in zip(w_refs, slot_refs):
            s_ref[...] = w_ref[...].astype(BF16)

    outs, carried = _call(
        body, name=name, args=[x, g, *weights], grid=(n,), prefetch=chip,
        in_specs=[pl.BlockSpec((tt, D), lambda i, chip_ref: (i, 0)), pl.BlockSpec((1, D), lambda i, chip_ref: (0, 0))]
        + [pl.BlockSpec((r, w.shape[1]), lambda i, chip_ref: (i, 0)) for r, w in zip(rows, weights)],
        out_specs=[pl.BlockSpec((tt, D), lambda i, chip_ref: (i, 0))]
        + [pl.BlockSpec((None, r, w.shape[1]), lambda i, chip_ref: (chip_ref[0], i, 0)) for r, w in zip(rows, weights)],
        out_shape=[jax.ShapeDtypeStruct((T, D), BF16)]
        + [jax.ShapeDtypeStruct((N_CHIPS, *w.shape), BF16) for w in weights],
        semantics=("parallel",), stages=stages)
    return outs[0], outs[1:], carried


def _rmsnorm_bwd(dxn, h, g, dres, name, stages=()):
    T, D = h.shape
    tt = _tile(T, 128, SUBLANES)

    def body(dxn_ref, h_ref, g_ref, dres_ref, dh_ref, dhb_ref, dg_ref):
        @pl.when(pl.program_id(0) == 0)
        def _():
            dg_ref[...] = jnp.zeros_like(dg_ref)

        hv = h_ref[...]
        inv = _row_inv(hv)
        n = hv * inv
        d = dxn_ref[...]
        dg_ref[...] += jnp.sum(d * n, axis=0, keepdims=True)
        dn = d * g_ref[...]
        dh = dres_ref[...] + inv * (dn - n * jnp.mean(dn * n, axis=-1, keepdims=True))
        dh_ref[...] = dh
        dhb_ref[...] = dh.astype(BF16)

    row = pl.BlockSpec((tt, D), lambda i: (i, 0))
    vec = pl.BlockSpec((1, D), lambda i: (0, 0))
    return _call(
        body, name=name, args=[dxn, h, g, dres], grid=(T // tt,), in_specs=[row, row, vec, row],
        out_specs=[row, row, vec],
        out_shape=[jax.ShapeDtypeStruct((T, D), F32), jax.ShapeDtypeStruct((T, D), BF16),
                   jax.ShapeDtypeStruct((1, D), F32)],
        semantics=("arbitrary",), stages=stages)


def _loss_and_final_norm_bwd(h1, d2, tgt, g, name):
    T, D = h1.shape
    tt = _tile(T, 128, SUBLANES)

    def body(h1_ref, d2_ref, t_ref, g_ref, dh_ref, dhb_ref, dg_ref, loss_ref):
        @pl.when(pl.program_id(0) == 0)
        def _():
            dg_ref[...] = jnp.zeros_like(dg_ref)
            loss_ref[...] = jnp.zeros_like(loss_ref)

        hv = h1_ref[...] + d2_ref[...]
        gv = g_ref[...]
        inv = _row_inv(hv)
        n = hv * inv
        err = n * gv - t_ref[...]
        loss_ref[...] += 0.5 * jnp.sum(jnp.mean(err * err, axis=-1, keepdims=True))
        dy = err * (1.0 / D)
        dg_ref[...] += jnp.sum(dy * n, axis=0, keepdims=True)
        dn = dy * gv
        dh = inv * (dn - n * jnp.mean(dn * n, axis=-1, keepdims=True))
        dh_ref[...] = dh
        dhb_ref[...] = dh.astype(BF16)

    row = pl.BlockSpec((tt, D), lambda i: (i, 0))
    vec = pl.BlockSpec((1, D), lambda i: (0, 0))
    one = pl.BlockSpec((1, LANES), lambda i: (0, 0))
    return _call(
        body, name=name, args=[h1, d2, tgt, g], grid=(T // tt,), in_specs=[row, row, row, vec],
        out_specs=[row, row, vec, one],
        out_shape=[jax.ShapeDtypeStruct((T, D), F32), jax.ShapeDtypeStruct((T, D), BF16),
                   jax.ShapeDtypeStruct((1, D), F32), jax.ShapeDtypeStruct((1, LANES), F32)],
        semantics=("arbitrary",))[0]


def _gelu(x):
    th = jnp.tanh(GELU_K * (x + GELU_A * (x * x * x)))
    return 0.5 * x * (1.0 + th), th


def _gelu_grad(x, th):
    return 0.5 * (1.0 + th) + 0.5 * x * (1.0 - th * th) * (GELU_K * (1.0 + 3.0 * GELU_A * (x * x)))


def _shift_rows(cur, prev_rows, k):
    rolled = pltpu.roll(cur, k, 0)
    row = lax.broadcasted_iota(jnp.int32, cur.shape, 0)
    out = rolled
    for r in range(k):
        out = jnp.where(row == r, prev_rows[SUBLANES - k + r:SUBLANES - k + r + 1, :], out)
    return out


def _unshift_rows(cur, next_rows, k):
    n = cur.shape[0]
    rolled = pltpu.roll(cur, n - k, 0)
    row = lax.broadcasted_iota(jnp.int32, cur.shape, 0)
    out = rolled
    for r in range(k):
        out = jnp.where(row == n - k + r, next_rows[r:r + 1, :], out)
    return out


def _mixer_specs(W, blk, halo):
    cols = [pl.BlockSpec((CHUNK, W), functools.partial(lambda i, col: (blk(i), col), col=col)) for col in range(5)]
    halos = [pl.BlockSpec((SUBLANES, W), functools.partial(lambda i, col: (halo(i), col), col=col)) for col in (1, 2)]
    return cols, halos


def _mixers_fwd(proj, conv_w, wm, bias_e, g_a, g_b, seq_len, name, stages=()):
    T, W5 = proj.shape
    W = W5 // 5
    H = W // HEAD_DIM
    per_seq = seq_len // CHUNK
    rb = CHUNK // SUBLANES
    cols, halos = _mixer_specs(W, lambda i: i, lambda i: jnp.maximum(i * rb - 1, 0))

    def body(b_ref, c_ref, hin_ref, u_ref, v_ref, ch_ref, hh_ref, cw_ref, wm_ref, be_ref, ga_ref, gb_ref, y_ref, s_ref):
        first = (pl.program_id(0) % per_seq) == 0
        hc = c_ref[...] * hin_ref[...]
        hc_prev = jnp.where(first, 0.0, ch_ref[...] * hh_ref[...])
        cw = cw_ref[...]
        ya = b_ref[...] * (cw[0:1, :] * _shift_rows(hc, hc_prev, 2) + cw[1:2, :] * _shift_rows(hc, hc_prev, 1)
                           + cw[2:3, :] * hc)
        y_ref[:, 0:W] = (ya * _row_inv(ya) * ga_ref[...]).astype(BF16)
        gu, _ = _gelu(u_ref[...])
        gv, _ = _gelu(v_ref[...])
        gvb = gv.astype(BF16)
        for hd in range(H):
            sl = slice(hd * HEAD_DIM, (hd + 1) * HEAD_DIM)
            s_ref[:, sl] = jnp.dot(wm_ref[hd], gvb[:, sl], preferred_element_type=F32)
        yb = gu * (s_ref[...] + be_ref[...])
        y_ref[:, W:2 * W] = (yb * _row_inv(yb) * gb_ref[...]).astype(BF16)

    full = lambda shape: pl.BlockSpec(shape, lambda i: (0,) * len(shape))
    outs, carried = _call(
        body, name=name, args=[proj, proj, proj, proj, proj, proj, proj, conv_w, wm, bias_e, g_a, g_b],
        grid=(T // CHUNK,),
        in_specs=cols + halos + [full((CONV_K, W)), full((H, CHUNK, CHUNK)), full((CHUNK, W)), full((1, W)), full((1, W))],
        out_specs=[pl.BlockSpec((CHUNK, 2 * W), lambda i: (i, 0))], out_shape=[jax.ShapeDtypeStruct((T, 2 * W), BF16)],
        scratch_shapes=[pltpu.VMEM((CHUNK, W), F32)], semantics=("parallel",), stages=stages)
    return outs[0], carried


def _mixers_bwd(dy, proj, conv_w, wm, wmt, bias_e, g_a, g_b, head_onehot, seq_len, name, stages=()):
    T, W5 = proj.shape
    W = W5 // 5
    H = W // HEAD_DIM
    nb = T // CHUNK
    per_seq = seq_len // CHUNK
    rb = CHUNK // SUBLANES
    blk = lambda i: nb - 1 - i
    cols, halos = _mixer_specs(W, blk, lambda i: jnp.maximum(blk(i) * rb - 1, 0))

    def body(dy_ref, b_ref, c_ref, hin_ref, u_ref, v_ref, ch_ref, hh_ref, cw_ref, wm_ref, wmt_ref, be_ref, ga_ref,
             gb_ref, oh_ref, dp_ref, dcw_ref, dga_ref, dgb_ref, dws_ref, dbt_ref, carry_ref, s_ref, dgv_ref):
        i = pl.program_id(0)
        j = nb - 1 - i

        @pl.when(i == 0)
        def _():
            for r in (dcw_ref, dga_ref, dgb_ref, dws_ref, dbt_ref, carry_ref):
                r[...] = jnp.zeros_like(r)

        first = (j % per_seq) == 0
        last = (j % per_seq) == per_seq - 1
        b, c, hin = b_ref[...], c_ref[...], hin_ref[...]
        cw = cw_ref[...]
        hc = c * hin
        hc_prev = jnp.where(first, 0.0, ch_ref[...] * hh_ref[...])
        hc1 = _shift_rows(hc, hc_prev, 1)
        hc2 = _shift_rows(hc, hc_prev, 2)
        conv = cw[0:1, :] * hc2 + cw[1:2, :] * hc1 + cw[2:3, :] * hc
        ya = b * conv
        inv_a = _row_inv(ya)
        na = ya * inv_a
        do_a = dy_ref[:, 0:W]
        dga_ref[...] += jnp.sum(do_a * na, axis=0, keepdims=True)
        dna = do_a * ga_ref[...]
        dya = inv_a * (dna - na * jnp.mean(dna * na, axis=-1, keepdims=True))
        dp_ref[:, 0:W] = (dya * conv).astype(BF16)
        dconv = dya * b
        dcw_ref[0:1, :] += jnp.sum(dconv * hc2, axis=0, keepdims=True)
        dcw_ref[1:2, :] += jnp.sum(dconv * hc1, axis=0, keepdims=True)
        dcw_ref[2:3, :] += jnp.sum(dconv * hc, axis=0, keepdims=True)
        nxt = jnp.where(last, 0.0, carry_ref[...])
        dhc = cw[2:3, :] * dconv + cw[1:2, :] * _unshift_rows(dconv, nxt, 1) + cw[0:1, :] * _unshift_rows(dconv, nxt, 2)
        carry_ref[...] = dconv[0:SUBLANES, :]
        dp_ref[:, W:2 * W] = (dhc * hin).astype(BF16)
        dp_ref[:, 2 * W:3 * W] = (dhc * c).astype(BF16)
        u, v = u_ref[...], v_ref[...]
        gu, thu = _gelu(u)
        gv, thv = _gelu(v)
        gvb = gv.astype(BF16)
        for hd in range(H):
            sl = slice(hd * HEAD_DIM, (hd + 1) * HEAD_DIM)
            s_ref[:, sl] = jnp.dot(wm_ref[hd], gvb[:, sl], preferred_element_type=F32)
        s = s_ref[...] + be_ref[...]
        yb = gu * s
        inv_b = _row_inv(yb)
        nbv = yb * inv_b
        do_b = dy_ref[:, W:2 * W]
        dgb_ref[...] += jnp.sum(do_b * nbv, axis=0, keepdims=True)
        dnb = do_b * gb_ref[...]
        dyb = inv_b * (dnb - nbv * jnp.mean(dnb * nbv, axis=-1, keepdims=True))
        dp_ref[:, 3 * W:4 * W] = (dyb * s * _gelu_grad(u, thu)).astype(BF16)
        dsb = (dyb * gu).astype(BF16)
        dbt_ref[...] += jnp.dot(dsb, oh_ref[...], preferred_element_type=F32)
        for hd in range(H):
            sl = slice(hd * HEAD_DIM, (hd + 1) * HEAD_DIM)
            dws_ref[hd] += lax.dot_general(dsb[:, sl], gvb[:, sl], (((1,), (1,)), ((), ())), preferred_element_type=F32)
            dgv_ref[:, sl] = jnp.dot(wmt_ref[hd], dsb[:, sl], preferred_element_type=F32)
        dp_ref[:, 4 * W:5 * W] = (dgv_ref[...] * _gelu_grad(v, thv)).astype(BF16)

    full = lambda shape: pl.BlockSpec(shape, lambda i: (0,) * len(shape))
    return _call(
        body, name=name, grid=(nb,),
        args=[dy, proj, proj, proj, proj, proj, proj, proj, conv_w, wm, wmt, bias_e, g_a, g_b, head_onehot],
        in_specs=[pl.BlockSpec((CHUNK, 2 * W), lambda i: (blk(i), 0))] + cols + halos
        + [full((CONV_K, W)), full((H, CHUNK, CHUNK)), full((H, CHUNK, CHUNK)), full((CHUNK, W)), full((1, W)),
           full((1, W)), full((W, LANES))],
        out_specs=[pl.BlockSpec((CHUNK, 5 * W), lambda i: (blk(i), 0)), full((SUBLANES, W)), full((1, W)), full((1, W)),
                   full((H, CHUNK, CHUNK)), full((CHUNK, LANES))],
        out_shape=[jax.ShapeDtypeStruct((T, 5 * W), BF16), jax.ShapeDtypeStruct((SUBLANES, W), F32),
                   jax.ShapeDtypeStruct((1, W), F32), jax.ShapeDtypeStruct((1, W), F32),
                   jax.ShapeDtypeStruct((H, CHUNK, CHUNK), F32), jax.ShapeDtypeStruct((CHUNK, LANES), F32)],
        scratch_shapes=[pltpu.VMEM((SUBLANES, W), F32), pltpu.VMEM((CHUNK, W), F32), pltpu.VMEM((CHUNK, W), F32)],
        semantics=("arbitrary",), stages=stages)


def _cast_into_slot(w, chip, name):
    R, C = w.shape
    tr = _tile(R, 256, 16)

    def body(chip_ref, w_ref, o_ref):
        o_ref[...] = w_ref[...].astype(BF16)

    return pl.pallas_call(
        body, name=name,
        grid_spec=pltpu.PrefetchScalarGridSpec(
            num_scalar_prefetch=1, grid=(R // tr,),
            in_specs=[pl.BlockSpec((tr, C), lambda i, chip_ref: (i, 0))],
            out_specs=pl.BlockSpec((None, tr, C), lambda i, chip_ref: (chip_ref[0], i, 0))),
        out_shape=jax.ShapeDtypeStruct((N_CHIPS, R, C), BF16),
        compiler_params=_params(("parallel",)),
    )(chip, w)


def _rs_pair_add(dw, got, chip_core, name):
    S, R, C = dw.shape
    hr = R // 2
    tr = _tile(hr, 256, 16)
    nrb = hr // tr

    def body(cc_ref, dw_ref, got_ref, send_ref, own_ref):
        s = dw_ref[...].astype(F32) + got_ref[...].astype(F32)
        send_ref[...] = s.astype(BF16)

        @pl.when(pl.program_id(1) == cc_ref[0])
        def _():
            own_ref[...] = s

    return pl.pallas_call(
        body, name=name,
        grid_spec=pltpu.PrefetchScalarGridSpec(
            num_scalar_prefetch=1, grid=(nrb, S),
            in_specs=[pl.BlockSpec((None, tr, C), lambda i, q, cc: (q, cc[1] * nrb + i, 0)),
                      pl.BlockSpec((None, tr, C), lambda i, q, cc: (q, i, 0))],
            out_specs=[pl.BlockSpec((None, tr, C), lambda i, q, cc: (q, i, 0)),
                       pl.BlockSpec((tr, C), lambda i, q, cc: (i, 0))]),
        out_shape=[jax.ShapeDtypeStruct((S, hr, C), BF16), jax.ShapeDtypeStruct((hr, C), F32)],
        compiler_params=_params(("parallel", "arbitrary")),
    )(chip_core, dw, got)


def _rs_final_add(own, got, chip_core, name):
    hr, C = own.shape
    tr = _tile(hr, 256, 16)
    nrb = hr // tr

    def body(cc_ref, own_ref, got_ref, o_ref):
        o_ref[...] = ((own_ref[...] + got_ref[0].astype(F32)) + got_ref[1].astype(F32)) + got_ref[2].astype(F32)

    return pl.pallas_call(
        body, name=name,
        grid_spec=pltpu.PrefetchScalarGridSpec(
            num_scalar_prefetch=1, grid=(nrb,),
            in_specs=[pl.BlockSpec((tr, C), lambda i, cc: (i, 0)), pl.BlockSpec((3, tr, C), lambda i, cc: (0, i, 0))],
            out_specs=pl.BlockSpec((tr, C), lambda i, cc: (cc[1] * nrb + i, 0))),
        out_shape=jax.ShapeDtypeStruct((2 * hr, C), F32),
        compiler_params=_params(("parallel",)),
    )(chip_core, own, got)


def _adamw_math(w, g, m, v):
    m2 = ADAM_B1 * m + (1.0 - ADAM_B1) * g
    v2 = ADAM_B2 * v + (1.0 - ADAM_B2) * (g * g)
    delta = -ADAM_LR * ((m2 / ADAM_C1) / (jnp.sqrt(v2 / ADAM_C2) + ADAM_EPS) + ADAM_WD * w)
    return delta, m2, v2


def _adamw(w, g, m, v, name):
    R, C = w.shape
    tr = _tile(R, max(SUBLANES, (256 * 1024) // C), SUBLANES)

    def body(w_ref, g_ref, m_ref, v_ref, g2_ref, d_ref, m2_ref, v2_ref):
        g = g_ref[...]
        g2_ref[...] = g
        d_ref[...], m2_ref[...], v2_ref[...] = _adamw_math(w_ref[...], g, m_ref[...], v_ref[...])

    blk = pl.BlockSpec((tr, C), lambda i: (i, 0))
    return _call(body, name=name, args=[w, g, m, v], grid=(R // tr,), in_specs=[blk] * 4, out_specs=[blk] * 4,
                 out_shape=[jax.ShapeDtypeStruct((R, C), F32)] * 4, semantics=("parallel",))[0]


def _adamw_sc(w, g, m, v, name):
    R, C = w.shape
    blk = (SUBLANES, 512)
    assert R % blk[0] == 0 and C % blk[1] == 0
    mesh = plsc.VectorSubcoreMesh(core_axis_name="sc_core", subcore_axis_name="sc_tile")
    spec = pl.BlockSpec(block_shape=blk, index_map=lambda i, j: (i, j))

    def kern(w_hbm, g_hbm, m_hbm, v_hbm, g2_hbm, d_hbm, m2_hbm, v2_hbm):
        def body(w_v, g_v, m_v, v_v, g2_v, d_v, m2_v, v2_v):
            for r in range(blk[0]):
                @plsc.parallel_loop(0, blk[1], SC_LANES, unroll=8)
                def _(c):
                    at = (pl.ds(r, 1), pl.ds(c, SC_LANES))
                    gv = g_v.at[*at][...]
                    g2_v.at[*at][...] = gv
                    d_v.at[*at][...], m2_v.at[*at][...], v2_v.at[*at][...] = _adamw_math(
                        w_v.at[*at][...], gv, m_v.at[*at][...], v_v.at[*at][...])

        pltpu.emit_pipeline(
            body, grid=(R // blk[0], C // blk[1]), in_specs=[spec] * 4, out_specs=[spec] * 4,
            core_axis_name=("sc_core", "sc_tile"), dimension_semantics=(pltpu.PARALLEL, pltpu.PARALLEL),
        )(w_hbm, g_hbm, m_hbm, v_hbm, g2_hbm, d_hbm, m2_hbm, v2_hbm)

    return pl.kernel(kern, name=name, out_type=[jax.ShapeDtypeStruct((R, C), F32)] * 4, mesh=mesh,
                     scratch_types=[])(w, g, m, v)


def _all_gather_small(block, name):
    m_per, n = block.shape

    def body(x_ref, out_ref, send_sems, recv_sems, local_sem):
        x, y, c = _position()
        me, sibling = (x, y, c), (x, y, 1 - c)
        chips = _other_chips(x, y)

        def rows(px, py, pc):
            return out_ref.at[pl.ds((4 * px + 2 * py + pc) * m_per, m_per), :]

        def copy(k, blk, to, src=None):
            return pltpu.make_async_remote_copy(src_ref=rows(*blk) if src is None else src, dst_ref=rows(*blk),
                                                send_sem=send_sems.at[k], recv_sem=recv_sems.at[k], device_id=to,
                                                device_id_type=MESH)

        mine = pltpu.make_async_copy(x_ref, rows(*me), local_sem)
        mine.start()
        first = [copy(0, me, sibling, src=x_ref)]
        first += [copy(1 + j, me, (*chip, c), src=x_ref) for j, chip in enumerate(chips)]
        for cp in first:
            cp.start()
        passed = [copy(4 + j, (*chip, c), sibling) for j, chip in enumerate(chips)]
        for j, chip in enumerate(chips):
            copy(1 + j, (*chip, c), me).wait_recv()
            passed[j].start()
        copy(0, sibling, me).wait_recv()
        for j, chip in enumerate(chips):
            copy(4 + j, (*chip, 1 - c), me).wait_recv()
        for cp in first + passed:
            cp.wait_send()
        mine.wait()

    return pl.pallas_call(
        body, name=name,
        in_specs=[pl.BlockSpec(memory_space=pltpu.VMEM)],
        out_specs=pl.BlockSpec(memory_space=pltpu.VMEM),
        out_shape=jax.ShapeDtypeStruct((N_DEV * m_per, n), block.dtype),
        scratch_shapes=[pltpu.SemaphoreType.DMA((7,)), pltpu.SemaphoreType.DMA((7,)), pltpu.SemaphoreType.DMA],
        compiler_params=pltpu.CompilerParams(vmem_limit_bytes=VMEM_LIMIT_V7X, has_side_effects=True),
    )(block)


def _sum_and_adamw_small(gathered, w, m, v, name):
    rows, n = w.shape
    tr = _tile(rows, 32, SUBLANES)

    def body(p_ref, w_ref, m_ref, v_ref, g_ref, d_ref, m2_ref, v2_ref):
        g = p_ref[0]
        for d in range(1, N_DEV):
            g = g + p_ref[d]
        g_ref[...] = g
        d_ref[...], m2_ref[...], v2_ref[...] = _adamw_math(w_ref[...], g, m_ref[...], v_ref[...])

    blk = pl.BlockSpec((tr, n), lambda i: (i, 0))
    return pl.pallas_call(
        body, name=name, grid=(rows // tr,),
        in_specs=[pl.BlockSpec((N_DEV, tr, n), lambda i: (0, i, 0))] + [blk] * 3,
        out_specs=[blk] * 4,
        out_shape=[jax.ShapeDtypeStruct((rows, n), F32)] * 4,
        compiler_params=_params(("parallel",)),
    )(gathered.reshape(N_DEV, rows, n), w, m, v)


def _pad_rows(a):
    pad = (-a.shape[0]) % SUBLANES
    return jnp.pad(a, ((0, pad), (0, 0))) if pad else a


class _SmallPack:
    def __init__(self, W, D, H, chip):
        self.W, self.D, self.H, self.chip = W, D, H, chip
        self.offsets = {}
        self.rows = 0

    def pack(self, pieces):
        out = []
        self.offsets, self.rows = {}, 0
        for name, a in pieces:
            a = _pad_rows(a.astype(F32))
            self.offsets[name] = (self.rows, a.shape[0])
            self.rows += a.shape[0]
            out.append(a)
        return jnp.concatenate(out, axis=0)

    def piece(self, packed, name):
        start, n = self.offsets[name]
        return packed[start:start + n]


def _bias_rows(b, W):
    bt = jnp.pad(b.T, ((0, 0), (0, LANES - b.shape[0])))
    return bt.reshape(-1, W)


def _bias_from_rows(rows, H):
    return rows.reshape(-1)[:CHUNK * LANES].reshape(CHUNK, LANES)[:, :H].T


def kernel(x, mix_norm_g, w_in, conv_w, spatial_w, spatial_b, conv_out_norm_g, gmlp_out_norm_g, w_out, mlp_norm_g, w_up, w_down, final_norm_g, loss_target, m_mix_norm_g, m_w_in, m_conv_w, m_spatial_w, m_spatial_b, m_conv_out_norm_g, m_gmlp_out_norm_g, m_w_out, m_mlp_norm_g, m_w_up, m_w_down, m_final_norm_g, v_mix_norm_g, v_w_in, v_conv_w, v_spatial_w, v_spatial_b, v_conv_out_norm_g, v_gmlp_out_norm_g, v_w_out, v_mlp_norm_g, v_w_up, v_w_down, v_final_norm_g):
    Bl, S, D = x.shape
    T = Bl * S
    W = conv_out_norm_g.shape[-1]
    H = W // HEAD_DIM
    Wl = conv_w.shape[-1]
    xi, yi, ci = _position()
    chip = (2 * xi + yi).astype(jnp.int32)
    chip_arr = chip.reshape(1)
    chip_core = jnp.stack([chip, ci.astype(jnp.int32)])

    x2 = x.reshape(T, D)
    tgt2 = loss_target.reshape(T, D)

    s_in = _cast_into_slot(w_in[0], chip_arr, "cast_w_in")
    up_rows = w_up.shape[1] // 2
    up_cuts = [0] + [up_rows * pct // 100 // 16 * 16 for pct in (43, 57, 82)] + [up_rows]
    up_part = [(lo, hi - lo) for lo, hi in zip(up_cuts[:-1], up_cuts[1:])]

    causal = jnp.tril(jnp.ones((CHUNK, CHUNK), dtype=bool))
    wm = jnp.where(causal[None], spatial_w[0], 0.0).astype(BF16)
    wmt = jnp.swapaxes(wm, 1, 2)
    bias_e = jnp.repeat(spatial_b[0].T, HEAD_DIM, axis=1)
    conv_full = lax.dynamic_update_slice(jnp.zeros((CONV_K, W), F32), conv_w[0], (0, chip * Wl))
    head_onehot = (jnp.arange(W)[:, None] // HEAD_DIM == jnp.arange(LANES)[None, :]).astype(BF16)
    g_a, g_b = conv_out_norm_g, gmlp_out_norm_g

    xn, (s_out, s_up, s_down), ((g_in,), (conv_gathered,)) = _rmsnorm_fwd_and_casts(
        x2, mix_norm_g, [w_out[0], w_up[0], w_down[0]], chip_arr, "mix_norm_fwd",
        stages=[_GatherRows(s_in), _GatherSmall(_pad_rows(conv_full))])
    conv_w_all = conv_gathered.reshape(N_DEV, SUBLANES, W)[:, :CONV_K]
    conv_w_all = conv_w_all[0] + conv_w_all[2] + conv_w_all[4] + conv_w_all[6]
    proj, ((g_out,), (g_up,)) = _matmul(xn, g_in, mode="nn", name="proj_fwd", tm=1024, tn=512, tk=4096,
                                        out_dtypes=[F32], b_shard="n",
                                        stages=[_GatherRows(s_out), _GatherRows(s_up, *up_part[0])])
    g_out = g_out.reshape(-1, D)
    y, ((g_up,),) = _mixers_fwd(proj, conv_w_all, wm, bias_e, g_a, g_b, S, "mixers_fwd",
                                stages=[_GatherRows(g_up, *up_part[1])])
    h1, ((g_up,),) = _matmul(y, g_out, mode="nn", name="out_proj_fwd", tm=1024, tn=512, tk=4096, out_dtypes=[F32],
                             epilogue=_ep_residual, extras=(x2,), stages=[_GatherRows(g_up, *up_part[2])])
    xn2, ((g_up,),) = _rmsnorm_fwd(h1, mlp_norm_g, "mlp_norm_fwd", stages=[_GatherRows(g_up, *up_part[3])])
    (r, a), ((g_down,),) = _matmul(xn2, g_up, mode="nn", name="up_fwd", tm=1024, tn=1024, tk=4096, n_sub=2,
                                   out_dtypes=[BF16, BF16], epilogue=_ep_relu2, b_shard="n",
                                   stages=[_GatherRows(s_down)])
    g_down = g_down.reshape(-1, D)
    d2, _ = _matmul(a, g_down, mode="nn", name="down_fwd", tm=2048, tn=1024, tk=1024, out_dtypes=[F32])
    dh2, dh2b, d_final_g, loss_part = _loss_and_final_norm_bwd(h1, d2, tgt2, final_norm_g.reshape(1, D),
                                                               "loss_final_norm")

    def rs_adds(dw, got, tag):
        return _rs_pair_add(dw, got, chip_core, f"rs_pair_add_{tag}")

    dw_down, _ = _matmul(a, dh2b, mode="tn", name="down_dw", tm=1024, tn=1024, tk=4096, n_sub=2, out_dtypes=[BF16])
    dw_down = dw_down.reshape(N_CHIPS, -1, D)
    dpre, ((got_down,),) = _matmul(dh2b, g_down, mode="nt", name="down_dx", tm=1024, tn=1024, tk=4096, n_sub=2,
                                   out_dtypes=[BF16], epilogue=_ep_relu2_bwd, extras=(r,),
                                   stages=[_PairExchange(dw_down)])
    part_down, own_down = rs_adds(dw_down, got_down, "w_down")
    rows_down = part_down.shape[1]
    down_a = rows_down * 3 // 4 // 16 * 16
    dw_up, ((landed_down,),) = _matmul(xn2, dpre, mode="tn", name="up_dw", tm=1024, tn=1024, tk=4096, n_sub=2,
                                       out_dtypes=[BF16], out_shard=True,
                                       stages=[_ChipExchange(part_down, None, 0, down_a)])
    dxn2, ((got_up,), (landed_down,)) = _matmul(
        dpre, g_up, mode="nt", name="up_dx", tm=2048, tn=1024, tk=1024, out_dtypes=[F32], b_shard="k",
        stages=[_PairExchange(dw_up), _ChipExchange(part_down, landed_down, down_a, rows_down - down_a)])
    half_down = _rs_final_add(own_down, landed_down, chip_core, "rs_final_add_w_down")
    part_up, own_up = rs_adds(dw_up, got_up, "w_up")
    (dh1, dh1b, d_mlp_g), ((grad_down,),) = _rmsnorm_bwd(dxn2, h1, mlp_norm_g, dh2, "mlp_norm_bwd",
                                                        stages=[_HalfExchange(half_down)])
    rows_up = part_up.shape[1]
    q_up = rows_up // 4 // 16 * 16
    dy, ((landed_up,),) = _matmul(dh1b, g_out, mode="nt", name="out_proj_dx", tm=1024, tn=512, tk=4096,
                                  out_dtypes=[F32], stages=[_ChipExchange(part_up, None, 0, q_up)])
    (dproj, d_conv, d_ga, d_gb, d_ws, d_bt), ((landed_up,),) = _mixers_bwd(
        dy, proj, conv_w_all, wm, wmt, bias_e, g_a, g_b, head_onehot, S, "mixers_bwd",
        stages=[_ChipExchange(part_up, landed_up, q_up, q_up)])
    dw_in, ((landed_up,),) = _matmul(
        xn, dproj, mode="tn", name="proj_dw", tm=1024, tn=512, tk=4096, out_dtypes=[BF16], out_shard=True,
        stages=[_ChipExchange(part_up, landed_up, 2 * q_up, rows_up - 2 * q_up)])
    half_up = _rs_final_add(own_up, landed_up, chip_core, "rs_final_add_w_up")
    pack = _SmallPack(W, D, H, chip)
    loss_row = jnp.pad(loss_part[:, :1], ((0, 0), (0, W - 1)))
    g_part = pack.pack([("spatial_w", (d_ws * causal.astype(F32)[None]).reshape(-1, W)), ("conv_w", d_conv),
                        ("mlp_norm_g", d_mlp_g.reshape(-1, W)), ("final_norm_g", d_final_g.reshape(-1, W)),
                        ("conv_out_norm_g", d_ga), ("gmlp_out_norm_g", d_gb), ("spatial_b", d_bt.reshape(-1, W)),
                        ("loss", loss_row)])
    dw_out, ((got_in,), (grad_up,), (g_all,)) = _matmul(
        y, dh1b, mode="tn", name="out_proj_dw", tm=1024, tn=1024, tk=4096, n_sub=2, out_dtypes=[BF16],
        stages=[_PairExchange(dw_in), _HalfExchange(half_up), _GatherSmall(g_part)])
    dw_out = dw_out.reshape(N_CHIPS, -1, D)
    part_in, own_in = rs_adds(dw_in, got_in, "w_in")
    dxn, ((landed_in,), (got_out,)) = _matmul(
        dproj, g_in, mode="nt", name="proj_dx", tm=2048, tn=1024, tk=1280, out_dtypes=[F32], b_shard="k",
        stages=[_ChipExchange(part_in), _PairExchange(dw_out)])
    part_out, own_out = rs_adds(dw_out, got_out, "w_out")
    half_in = _rs_final_add(own_in, landed_in, chip_core, "rs_final_add_w_in")
    (grad_x, _unused, d_mix_g), ((landed_out,), (grad_in,)) = _rmsnorm_bwd(
        dxn, x2, mix_norm_g, dh1, "mix_norm_bwd", stages=[_ChipExchange(part_out), _HalfExchange(half_in)])
    half_out = _rs_final_add(own_out, landed_out, chip_core, "rs_final_add_w_out")
    ((grad_out,),) = _run_stages([_HalfExchange(half_out)], "rs_half_exchange_w_out")
    big = {"w_down": _adamw_sc(w_down[0], grad_down, m_w_down[0], v_w_down[0], "adamw_sc_w_down"),
           "w_up": _adamw_sc(w_up[0], grad_up, m_w_up[0], v_w_up[0], "adamw_sc_w_up"),
           "w_in": _adamw(w_in[0], grad_in, m_w_in[0], v_w_in[0], "adamw_w_in"),
           "w_out": _adamw(w_out[0], grad_out, m_w_out[0], v_w_out[0], "adamw_w_out")}
    big = {k: [t[None] for t in v] for k, v in big.items()}

    def small(conv, sw, sb, ga, gb, mlp, fin):
        return pack.pack([("spatial_w", sw.reshape(-1, W)), ("conv_w", conv), ("mlp_norm_g", mlp.reshape(-1, W)),
                          ("final_norm_g", fin.reshape(-1, W)), ("conv_out_norm_g", ga.reshape(-1, W)),
                          ("gmlp_out_norm_g", gb.reshape(-1, W)), ("spatial_b", _bias_rows(sb, W)),
                          ("loss", jnp.zeros((1, W), F32))])

    def full_conv(cw):
        return lax.dynamic_update_slice(jnp.zeros((CONV_K, W), F32), cw[0], (0, chip * Wl))

    def mix_rows(a):
        return _pad_rows(a.reshape(-1, W))

    w_s = small(full_conv(conv_w), spatial_w, spatial_b[0], conv_out_norm_g, gmlp_out_norm_g, mlp_norm_g, final_norm_g)
    m_s = small(full_conv(m_conv_w), m_spatial_w, m_spatial_b[0], m_conv_out_norm_g, m_gmlp_out_norm_g, m_mlp_norm_g,
                m_final_norm_g)
    v_s = small(full_conv(v_conv_w), v_spatial_w, v_spatial_b[0], v_conv_out_norm_g, v_gmlp_out_norm_g, v_mlp_norm_g,
                v_final_norm_g)
    small_outs = _sum_and_adamw_small(g_all, w_s, m_s, v_s, "sum_adamw_small")
    mix_all = _all_gather_small(mix_rows(d_mix_g), "all_gather_mix_norm_grad")
    mix_outs = _sum_and_adamw_small(mix_all, mix_rows(mix_norm_g), mix_rows(m_mix_norm_g), mix_rows(v_mix_norm_g),
                                    "sum_adamw_mix_norm")

    def unpack(kind, name):
        if name == "mix_norm_g":
            return mix_outs[kind].reshape(-1)[:D].reshape(1, D)
        rows = pack.piece(small_outs[kind], name)
        if name == "spatial_w":
            return rows.reshape(1, H, CHUNK, CHUNK)
        if name == "conv_w":
            return lax.dynamic_slice(rows[:CONV_K], (0, chip * Wl), (CONV_K, Wl))[None]
        if name == "spatial_b":
            return _bias_from_rows(rows, H)[None]
        if name == "final_norm_g":
            return rows.reshape(-1)[:D]
        n = D if name == "mlp_norm_g" else W
        return rows.reshape(-1)[:n].reshape(1, n)

    loss = pack.piece(small_outs[0], "loss")[0, 0]
    order = ["mix_norm_g", "w_in", "conv_w", "spatial_w", "spatial_b", "conv_out_norm_g", "gmlp_out_norm_g", "w_out",
             "mlp_norm_g", "w_up", "w_down", "final_norm_g"]
    outs = [loss, grad_x.reshape(Bl, S, D)]
    for kind in range(4):
        for name in order:
            outs.append(big[name][kind] if name in big else unpack(kind, name))
    return tuple(outs)
```

```python
import functools
import math

import jax
import jax.numpy as jnp
from jax import lax
from jax.experimental import pallas as pl
from jax.experimental.pallas import tpu as pltpu
from jax.experimental.pallas import tpu_sc as plsc

F32 = jnp.float32
BF16 = jnp.bfloat16
MESH = pl.DeviceIdType.MESH

NORM_EPS = 1e-5
HEAD_DIM = 128
CHUNK = 128
CONV_K = 3
N_CHIPS = 4
N_DEV = 8

ADAM_LR = 0.001
ADAM_B1 = 0.9
ADAM_B2 = 0.999
ADAM_EPS = 1e-08
ADAM_WD = 0.01
ADAM_STEP = 10
ADAM_C1 = 1.0 - ADAM_B1 ** ADAM_STEP
ADAM_C2 = 1.0 - ADAM_B2 ** ADAM_STEP

GELU_K = math.sqrt(2.0 / math.pi)
GELU_A = 0.044715

VMEM_LIMIT_V7X = 56 * 1024 * 1024
SUBLANES = 8
LANES = 128
SC_LANES = 16


def _tile(dim, target, mult=LANES):
    if dim <= target:
        return dim
    t = (target // mult) * mult
    while t > mult and dim % t:
        t -= mult
    assert dim % t == 0, (dim, target, mult)
    return t


def _params(sem=None):
    return pltpu.CompilerParams(dimension_semantics=sem, vmem_limit_bytes=VMEM_LIMIT_V7X)


class _Stage:
    bufs = ()
    n_sems = 0
    MIDDLE_AT = 0.6
    base = 0

    def start(self, refs, send, recv):
        raise NotImplementedError

    def middle(self, refs, send, recv):
        pass

    def finish(self, refs, send, recv):
        raise NotImplementedError


def _position():
    return lax.axis_index("x"), lax.axis_index("y"), lax.axis_index("c")


def _other_chips(x, y):
    return [(1 - x, y), (x, 1 - y), (1 - x, 1 - y)]


def _remote(src, dst, send, recv, k, to):
    return pltpu.make_async_remote_copy(src_ref=src, dst_ref=dst, send_sem=send.at[k], recv_sem=recv.at[k],
                                        device_id=to, device_id_type=MESH)


def _call(body, *, name, args, in_specs, out_specs, out_shape, grid=(), scratch_shapes=(), semantics=None, stages=(),
          prefetch=None):
    n_in, n_out, n_scratch = len(args), len(out_shape), len(scratch_shapes)
    n_pre = 0 if prefetch is None else 1
    any_spec = pl.BlockSpec(memory_space=pl.ANY)
    extra_args, extra_out, aliases, layout = [], [], {}, []
    for st in stages:
        where = []
        for kind, buf in st.bufs:
            if kind in ("in", "alias"):
                extra_args.append(buf)
                pos_in = n_in + len(extra_args) - 1
            if kind in ("out", "alias"):
                extra_out.append(jax.ShapeDtypeStruct(buf.shape, buf.dtype))
                pos_out = n_out + len(extra_out) - 1
            if kind == "alias":
                aliases[n_pre + pos_in] = pos_out
            where.append(("in", pos_in) if kind == "in" else ("out", pos_out))
        layout.append(where)
    n_sems = sum(st.n_sems for st in stages)
    n_xin, n_xout = len(extra_args), len(extra_out)

    def wrapped(*refs):
        pre, refs = refs[:n_pre], refs[n_pre:]
        ins = refs[:n_in + n_xin]
        outs = refs[n_in + n_xin:n_in + n_xin + n_out + n_xout]
        scratch = refs[n_in + n_xin + n_out + n_xout:]
        main = pre + ins[:n_in] + outs[:n_out] + scratch[:n_scratch]
        if not stages:
            body(*main)
            return
        send, recv = scratch[n_scratch], scratch[n_scratch + 1]
        step, n_steps = 0, 1
        for d, g in enumerate(grid):
            step = step * g + pl.program_id(d)
            n_steps *= g
        base, views = 0, []
        for st, where in zip(stages, layout):
            st_refs = [ins[p] if side == "in" else outs[p] for side, p in where]
            st.base = base
            views.append((st, st_refs, send, recv))
            base += st.n_sems

        def starts():
            for st, r, s, v in views:
                st.start(r, s, v)

        def middles():
            for st, r, s, v in views:
                st.middle(r, s, v)

        def finishes():
            for st, r, s, v in views:
                st.finish(r, s, v)

        if not grid:
            starts()
            body(*main)
            middles()
            finishes()
        else:
            pl.when(step == 0)(starts)
            body(*main)
            pl.when(step == min(int(n_steps * _Stage.MIDDLE_AT), n_steps - 1))(middles)
            pl.when(step == n_steps - 1)(finishes)

    sems = [pltpu.SemaphoreType.DMA((n_sems,)), pltpu.SemaphoreType.DMA((n_sems,))] if stages else []
    specs = dict(in_specs=list(in_specs) + [any_spec] * n_xin, out_specs=list(out_specs) + [any_spec] * n_xout,
                 scratch_shapes=list(scratch_shapes) + sems)
    if prefetch is None:
        kw = dict(specs, **(dict(grid=grid) if grid else {}))
    else:
        kw = dict(grid_spec=pltpu.PrefetchScalarGridSpec(num_scalar_prefetch=1, grid=grid, **specs))
    res = pl.pallas_call(
        wrapped, name=name,
        out_shape=list(out_shape) + extra_out,
        input_output_aliases=aliases,
        compiler_params=pltpu.CompilerParams(
            dimension_semantics=("arbitrary",) * len(grid) if stages and grid else semantics,
            vmem_limit_bytes=VMEM_LIMIT_V7X, has_side_effects=bool(stages)),
        **kw,
    )(*([] if prefetch is None else [prefetch]), *args, *extra_args)
    main_res, stage_res, pos = list(res[:n_out]), [], n_out
    for st in stages:
        k = sum(kind in ("out", "alias") for kind, _ in st.bufs)
        stage_res.append(list(res[pos:pos + k]))
        pos += k
    return main_res, stage_res


def _run_stages(stages, name):
    return _call(lambda: None, name=name, args=[], in_specs=[], out_specs=[], out_shape=[], stages=stages)[1]


class _GatherRows(_Stage):
    n_sems = 8

    def __init__(self, g, lo=0, n=None):
        self.hr = g.shape[1] // 2
        self.lo, self.n = lo, (self.hr if n is None else n)
        self.n0 = self.n // 2 // 16 * 16
        self.bufs = [("alias", g)]

    def _copy(self, g_ref, send, recv, k, chip_xy, half, to, lo=0, n=None):
        n = self.n if n is None else n
        blk = g_ref.at[2 * chip_xy[0] + chip_xy[1], pl.ds(half * self.hr + self.lo + lo, n), :]
        return _remote(blk, blk, send, recv, self.base + k, to)

    def _plan(self, g, send, recv):
        x, y, c = _position()
        me, sib = (x, y, c), (x, y, 1 - c)
        cx, cy, cd = _other_chips(x, y)
        n0, n1 = self.n0, self.n - self.n0
        mine = [((x, y), c, (*cx, c)), ((x, y), c, (*cy, c)), (cx, c, sib), (cy, c, sib),
                (cx, c, (*cy, c), 0, n0), (cy, c, (*cx, c), n0, n1), (cd, c, sib, 0, n0), (cd, c, sib, n0, n1)]
        theirs = [(cx, c, me), (cy, c, me), (cx, 1 - c, me), (cy, 1 - c, me),
                  (cd, c, me, 0, n0), (cd, c, me, n0, n1), (cd, 1 - c, me, 0, n0), (cd, 1 - c, me, n0, n1)]
        return (lambda k: self._copy(g, send, recv, k, *mine[k])), (lambda k: self._copy(g, send, recv, k, *theirs[k]))

    def start(self, refs, send, recv):
        mine, _ = self._plan(refs[0], send, recv)
        mine(0).start()
        mine(1).start()

    def middle(self, refs, send, recv):
        mine, theirs = self._plan(refs[0], send, recv)
        theirs(0).wait_recv()
        mine(4).start()
        mine(2).start()
        theirs(1).wait_recv()
        mine(5).start()
        mine(3).start()

    def finish(self, refs, send, recv):
        mine, theirs = self._plan(refs[0], send, recv)
        theirs(4).wait_recv()
        mine(6).start()
        theirs(5).wait_recv()
        mine(7).start()
        for k in (2, 3, 6, 7):
            theirs(k).wait_recv()
        for k in range(self.n_sems):
            mine(k).wait_send()


class _GatherSmall(_Stage):
    n_sems = 8

    def __init__(self, block):
        self.m = block.shape[0]
        self.bufs = [("in", block), ("out", jax.ShapeDtypeStruct((N_DEV * self.m, block.shape[1]), block.dtype))]

    def _rows(self, out, px, py, pc):
        return out.at[pl.ds((4 * px + 2 * py + pc) * self.m, self.m), :]

    def _copy(self, refs, send, recv, k, blk, to, own=False):
        dst = self._rows(refs[1], *blk)
        return _remote(refs[0] if own else dst, dst, send, recv, self.base + k, to)

    def _local(self, refs, send):
        return pltpu.make_async_copy(refs[0], self._rows(refs[1], *_position()), send.at[self.base + 7])

    def start(self, refs, send, recv):
        x, y, c = _position()
        self._local(refs, send).start()
        self._copy(refs, send, recv, 0, (x, y, c), (x, y, 1 - c), own=True).start()
        for j, chip in enumerate(_other_chips(x, y)):
            self._copy(refs, send, recv, 1 + j, (x, y, c), (*chip, c), own=True).start()

    def finish(self, refs, send, recv):
        x, y, c = _position()
        me, sib, chips = (x, y, c), (x, y, 1 - c), _other_chips(x, y)
        for j, chip in enumerate(chips):
            self._copy(refs, send, recv, 1 + j, (*chip, c), me).wait_recv()
            self._copy(refs, send, recv, 4 + j, (*chip, c), sib).start()
        self._copy(refs, send, recv, 0, sib, me).wait_recv()
        for j, chip in enumerate(chips):
            self._copy(refs, send, recv, 4 + j, (*chip, 1 - c), me).wait_recv()
        self._copy(refs, send, recv, 0, me, sib, own=True).wait_send()
        for j, chip in enumerate(chips):
            self._copy(refs, send, recv, 1 + j, me, (*chip, c), own=True).wait_send()
            self._copy(refs, send, recv, 4 + j, (*chip, c), sib).wait_send()
        self._local(refs, send).wait()


class _PairExchange(_Stage):
    n_sems = 1

    def __init__(self, dw):
        S, R, C = dw.shape
        self.hr = R // 2
        self.bufs = [("in", dw), ("out", jax.ShapeDtypeStruct((S, self.hr, C), dw.dtype))]

    def _copy(self, refs, send, recv):
        x, y, c = _position()
        return _remote(refs[0].at[:, pl.ds((1 - c) * self.hr, self.hr), :], refs[1], send, recv, self.base,
                       (x, y, 1 - c))

    def start(self, refs, send, recv):
        self._copy(refs, send, recv).start()

    def finish(self, refs, send, recv):
        cp = self._copy(refs, send, recv)
        cp.wait_recv()
        cp.wait_send()


class _ChipExchange(_Stage):
    n_sems = 3

    def __init__(self, part, landed=None, lo=0, n=None):
        S, hr, C = part.shape
        self.lo, self.n = lo, (hr if n is None else n)
        self.bufs = [("in", part), ("out", jax.ShapeDtypeStruct((3, hr, C), part.dtype)) if landed is None
                     else ("alias", landed)]

    def _copies(self, refs, send, recv):
        x, y, c = _position()
        rows = pl.ds(self.lo, self.n)
        return [_remote(refs[0].at[2 * chip[0] + chip[1], rows, :], refs[1].at[j, rows, :], send, recv,
                        self.base + j, (*chip, c))
                for j, chip in enumerate(_other_chips(x, y))]

    def start(self, refs, send, recv):
        for cp in self._copies(refs, send, recv):
            cp.start()

    def finish(self, refs, send, recv):
        copies = self._copies(refs, send, recv)
        for cp in copies:
            cp.wait_recv()
        for cp in copies:
            cp.wait_send()


class _HalfExchange(_Stage):
    n_sems = 1

    def __init__(self, grad):
        self.hr = grad.shape[0] // 2
        self.bufs = [("alias", grad)]

    def start(self, refs, send, recv):
        x, y, c = _position()
        mine = refs[0].at[pl.ds(c * self.hr, self.hr), :]
        _remote(mine, mine, send, recv, self.base, (x, y, 1 - c)).start()

    def finish(self, refs, send, recv):
        x, y, c = _position()
        mine = refs[0].at[pl.ds(c * self.hr, self.hr), :]
        theirs = refs[0].at[pl.ds((1 - c) * self.hr, self.hr), :]
        _remote(theirs, theirs, send, recv, self.base, (x, y, 1 - c)).wait_recv()
        _remote(mine, mine, send, recv, self.base, (x, y, 1 - c)).wait_send()


def _matmul(a, b, *, mode, name, tm, tn, tk, out_dtypes, epilogue=None, extras=(), b_shard=None, out_shard=False,
            stages=(), n_sub=1):
    if mode == "tn":
        K, M = a.shape
    else:
        M, K = a.shape
    if b_shard == "n":
        S, Kb, Ns = b.shape
        N = S * Ns
    elif b_shard == "k":
        S, N, Ks = b.shape
        Kb = S * Ks
    elif mode == "nt":
        N, Kb = b.shape
    else:
        Kb, N = b.shape
    assert Kb == K, (name, a.shape, b.shape)
    tm, tn, tk = _tile(M, tm), _tile(N, tn), _tile(K, tk)
    if b_shard == "n" or out_shard:
        n_per = N // N_CHIPS
        tn = _tile(n_per, tn)
        njs = n_per // tn
    if b_shard == "k":
        tk = _tile(K // N_CHIPS, tk)
        nks = (K // N_CHIPS) // tk
    gm, gn, gk = M // tm, N // tn, K // tk
    if gk > 1 or tn % (n_sub * LANES):
        n_sub = 1

    if mode == "tn":
        a_spec = pl.BlockSpec((tk, tm), lambda i, j, k: (k, i))
        dims = (((0,), (0,)), ((), ()))
    else:
        a_spec = pl.BlockSpec((tm, tk), lambda i, j, k: (i, k))
        dims = (((1,), (1,)), ((), ())) if mode == "nt" else (((1,), (0,)), ((), ()))
    if b_shard == "n":
        b_spec = pl.BlockSpec((None, tk, tn), lambda i, j, k: (j // njs, k, j % njs))
    elif b_shard == "k":
        b_spec = pl.BlockSpec((None, tn, tk), lambda i, j, k: (k // nks, j, k % nks))
    elif mode == "nt":
        b_spec = pl.BlockSpec((tn, tk), lambda i, j, k: (j, k))
    else:
        b_spec = pl.BlockSpec((tk, tn), lambda i, j, k: (k, j))
    mn_spec = pl.BlockSpec((tm, tn), lambda i, j, k: (i, j))
    if out_shard:
        out_spec = pl.BlockSpec((None, tm, tn), lambda i, j, k: (j // njs, i, j % njs))
        out_shape = [jax.ShapeDtypeStruct((N_CHIPS, M, N // N_CHIPS), dt) for dt in out_dtypes]
    else:
        out_spec = mn_spec
        out_shape = [jax.ShapeDtypeStruct((M, N), dt) for dt in out_dtypes]
    n_extra, n_out = len(extras), len(out_dtypes)

    def finish_tile(acc, extra_refs, out_refs):
        if epilogue is None:
            for o in out_refs:
                o[...] = acc.astype(o.dtype)
        else:
            epilogue(acc, extra_refs, out_refs)

    def body(*refs):
        a_ref, b_ref = refs[0], refs[1]
        extra_refs = refs[2:2 + n_extra]
        out_refs = refs[2 + n_extra:2 + n_extra + n_out]

        def product():
            return lax.dot_general(a_ref[...], b_ref[...], dims, preferred_element_type=F32)

        if gk == 1:
            sub = tn // n_sub
            for h in range(n_sub):
                cols = slice(h * sub, (h + 1) * sub)
                b_part = b_ref[cols, :] if mode == "nt" else b_ref[:, cols]
                acc = lax.dot_general(a_ref[...], b_part, dims, preferred_element_type=F32)
                finish_tile(acc, [e.at[:, cols] for e in extra_refs], [o.at[:, cols] for o in out_refs])
            return
        acc_ref = refs[-1]
        k = pl.program_id(2)

        @pl.when(k == 0)
        def _():
            acc_ref[...] = product()

        @pl.when((k > 0) & (k < gk - 1))
        def _():
            acc_ref[...] += product()

        @pl.when(k == gk - 1)
        def _():
            finish_tile(acc_ref[...] + product(), extra_refs, out_refs)

    outs, carried = _call(
        body, name=name, args=[a, b, *extras], grid=(gm, gn, gk),
        in_specs=[a_spec, b_spec] + [mn_spec] * n_extra, out_specs=[out_spec] * n_out, out_shape=out_shape,
        scratch_shapes=[pltpu.VMEM((tm, tn), F32)] if gk > 1 else [],
        semantics=("parallel", "parallel", "arbitrary"), stages=stages)
    return (outs[0] if n_out == 1 else outs), carried


def _ep_residual(acc, extra_refs, out_refs):
    out_refs[0][...] = extra_refs[0][...] + acc


def _ep_relu2(acc, extra_refs, out_refs):
    r = jnp.maximum(acc, 0.0)
    out_refs[0][...] = r.astype(BF16)
    out_refs[1][...] = (r * r).astype(BF16)


def _ep_relu2_bwd(acc, extra_refs, out_refs):
    out_refs[0][...] = (acc * (2.0 * extra_refs[0][...].astype(F32))).astype(BF16)


def _row_inv(x):
    return lax.rsqrt(jnp.mean(x * x, axis=-1, keepdims=True) + NORM_EPS)


def _rmsnorm_fwd(x, g, name, stages=()):
    T, D = x.shape
    tt = _tile(T, 256, SUBLANES)

    def body(x_ref, g_ref, o_ref):
        xv = x_ref[...]
        o_ref[...] = (xv * _row_inv(xv) * g_ref[...]).astype(BF16)

    outs, carried = _call(
        body, name=name, args=[x, g], grid=(T // tt,),
        in_specs=[pl.BlockSpec((tt, D), lambda i: (i, 0)), pl.BlockSpec((1, D), lambda i: (0, 0))],
        out_specs=[pl.BlockSpec((tt, D), lambda i: (i, 0))], out_shape=[jax.ShapeDtypeStruct((T, D), BF16)],
        semantics=("parallel",), stages=stages)
    return outs[0], carried


def _rmsnorm_fwd_and_casts(x, g, weights, chip, name, stages=()):
    T, D = x.shape
    tt = _tile(T, 256, SUBLANES)
    n, nw = T // tt, len(weights)
    rows = [w.shape[0] // n for w in weights]
    assert all(r % 16 == 0 and r * n == w.shape[0] for r, w in zip(rows, weights))

    def body(chip_ref, x_ref, g_ref, *refs):
        w_refs, o_ref, slot_refs = refs[:nw], refs[nw], refs[nw + 1:]
        xv = x_ref[...]
        o_ref[...] = (xv * _row_inv(xv) * g_ref[...]).astype(BF16)
        for w_ref, s_ref in zip(w_refs, slot_refs):
            s_ref[...] = w_ref[...].astype(BF16)

    outs, carried = _call(
        body, name=name, args=[x, g, *weights], grid=(n,), prefetch=chip,
        in_specs=[pl.BlockSpec((tt, D), lambda i, chip_ref: (i, 0)), pl.BlockSpec((1, D), lambda i, chip_ref: (0, 0))]
        + [pl.BlockSpec((r, w.shape[1]), lambda i, chip_ref: (i, 0)) for r, w in zip(rows, weights)],
        out_specs=[pl.BlockSpec((tt, D), lambda i, chip_ref: (i, 0))]
        + [pl.BlockSpec((None, r, w.shape[1]), lambda i, chip_ref: (chip_ref[0], i, 0)) for r, w in zip(rows, weights)],
        out_shape=[jax.ShapeDtypeStruct((T, D), BF16)]
        + [jax.ShapeDtypeStruct((N_CHIPS, *w.shape), BF16) for w in weights],
        semantics=("parallel",), stages=stages)
    return outs[0], outs[1:], carried


def _rmsnorm_bwd(dxn, h, g, dres, name, stages=()):
    T, D = h.shape
    tt = _tile(T, 128, SUBLANES)

    def body(dxn_ref, h_ref, g_ref, dres_ref, dh_ref, dhb_ref, dg_ref):
        @pl.when(pl.program_id(0) == 0)
        def _():
            dg_ref[...] = jnp.zeros_like(dg_ref)

        hv = h_ref[...]
        inv = _row_inv(hv)
        n = hv * inv
        d = dxn_ref[...]
        dg_ref[...] += jnp.sum(d * n, axis=0, keepdims=True)
        dn = d * g_ref[...]
        dh = dres_ref[...] + inv * (dn - n * jnp.mean(dn * n, axis=-1, keepdims=True))
        dh_ref[...] = dh
        dhb_ref[...] = dh.astype(BF16)

    row = pl.BlockSpec((tt, D), lambda i: (i, 0))
    vec = pl.BlockSpec((1, D), lambda i: (0, 0))
    return _call(
        body, name=name, args=[dxn, h, g, dres], grid=(T // tt,), in_specs=[row, row, vec, row],
        out_specs=[row, row, vec],
        out_shape=[jax.ShapeDtypeStruct((T, D), F32), jax.ShapeDtypeStruct((T, D), BF16),
                   jax.ShapeDtypeStruct((1, D), F32)],
        semantics=("arbitrary",), stages=stages)


def _loss_and_final_norm_bwd(h1, d2, tgt, g, name):
    T, D = h1.shape
    tt = _tile(T, 128, SUBLANES)

    def body(h1_ref, d2_ref, t_ref, g_ref, dh_ref, dhb_ref, dg_ref, loss_ref):
        @pl.when(pl.program_id(0) == 0)
        def _():
            dg_ref[...] = jnp.zeros_like(dg_ref)
            loss_ref[...] = jnp.zeros_like(loss_ref)

        hv = h1_ref[...] + d2_ref[...]
        gv = g_ref[...]
        inv = _row_inv(hv)
        n = hv * inv
        err = n * gv - t_ref[...]
        loss_ref[...] += 0.5 * jnp.sum(jnp.mean(err * err, axis=-1, keepdims=True))
        dy = err * (1.0 / D)
        dg_ref[...] += jnp.sum(dy * n, axis=0, keepdims=True)
        dn = dy * gv
        dh = inv * (dn - n * jnp.mean(dn * n, axis=-1, keepdims=True))
        dh_ref[...] = dh
        dhb_ref[...] = dh.astype(BF16)

    row = pl.BlockSpec((tt, D), lambda i: (i, 0))
    vec = pl.BlockSpec((1, D), lambda i: (0, 0))
    one = pl.BlockSpec((1, LANES), lambda i: (0, 0))
    return _call(
        body, name=name, args=[h1, d2, tgt, g], grid=(T // tt,), in_specs=[row, row, row, vec],
        out_specs=[row, row, vec, one],
        out_shape=[jax.ShapeDtypeStruct((T, D), F32), jax.ShapeDtypeStruct((T, D), BF16),
                   jax.ShapeDtypeStruct((1, D), F32), jax.ShapeDtypeStruct((1, LANES), F32)],
        semantics=("arbitrary",))[0]


def _gelu(x):
    th = jnp.tanh(GELU_K * (x + GELU_A * (x * x * x)))
    return 0.5 * x * (1.0 + th), th


def _gelu_grad(x, th):
    return 0.5 * (1.0 + th) + 0.5 * x * (1.0 - th * th) * (GELU_K * (1.0 + 3.0 * GELU_A * (x * x)))


def _shift_rows(cur, prev_rows, k):
    rolled = pltpu.roll(cur, k, 0)
    row = lax.broadcasted_iota(jnp.int32, cur.shape, 0)
    out = rolled
    for r in range(k):
        out = jnp.where(row == r, prev_rows[SUBLANES - k + r:SUBLANES - k + r + 1, :], out)
    return out


def _unshift_rows(cur, next_rows, k):
    n = cur.shape[0]
    rolled = pltpu.roll(cur, n - k, 0)
    row = lax.broadcasted_iota(jnp.int32, cur.shape, 0)
    out = rolled
    for r in range(k):
        out = jnp.where(row == n - k + r, next_rows[r:r + 1, :], out)
    return out


def _mixer_specs(W, blk, halo):
    cols = [pl.BlockSpec((CHUNK, W), functools.partial(lambda i, col: (blk(i), col), col=col)) for col in range(5)]
    halos = [pl.BlockSpec((SUBLANES, W), functools.partial(lambda i, col: (halo(i), col), col=col)) for col in (1, 2)]
    return cols, halos


def _mixers_fwd(proj, conv_w, wm, bias_e, g_a, g_b, seq_len, name, stages=()):
    T, W5 = proj.shape
    W = W5 // 5
    H = W // HEAD_DIM
    per_seq = seq_len // CHUNK
    rb = CHUNK // SUBLANES
    cols, halos = _mixer_specs(W, lambda i: i, lambda i: jnp.maximum(i * rb - 1, 0))

    def body(b_ref, c_ref, hin_ref, u_ref, v_ref, ch_ref, hh_ref, cw_ref, wm_ref, be_ref, ga_ref, gb_ref, y_ref, s_ref):
        first = (pl.program_id(0) % per_seq) == 0
        hc = c_ref[...] * hin_ref[...]
        hc_prev = jnp.where(first, 0.0, ch_ref[...] * hh_ref[...])
        cw = cw_ref[...]
        ya = b_ref[...] * (cw[0:1, :] * _shift_rows(hc, hc_prev, 2) + cw[1:2, :] * _shift_rows(hc, hc_prev, 1)
                           + cw[2:3, :] * hc)
        y_ref[:, 0:W] = (ya * _row_inv(ya) * ga_ref[...]).astype(BF16)
        gu, _ = _gelu(u_ref[...])
        gv, _ = _gelu(v_ref[...])
        gvb = gv.astype(BF16)
        for hd in range(H):
            sl = slice(hd * HEAD_DIM, (hd + 1) * HEAD_DIM)
            s_ref[:, sl] = jnp.dot(wm_ref[hd], gvb[:, sl], preferred_element_type=F32)
        yb = gu * (s_ref[...] + be_ref[...])
        y_ref[:, W:2 * W] = (yb * _row_inv(yb) * gb_ref[...]).astype(BF16)

    full = lambda shape: pl.BlockSpec(shape, lambda i: (0,) * len(shape))
    outs, carried = _call(
        body, name=name, args=[proj, proj, proj, proj, proj, proj, proj, conv_w, wm, bias_e, g_a, g_b],
        grid=(T // CHUNK,),
        in_specs=cols + halos + [full((CONV_K, W)), full((H, CHUNK, CHUNK)), full((CHUNK, W)), full((1, W)), full((1, W))],
        out_specs=[pl.BlockSpec((CHUNK, 2 * W), lambda i: (i, 0))], out_shape=[jax.ShapeDtypeStruct((T, 2 * W), BF16)],
        scratch_shapes=[pltpu.VMEM((CHUNK, W), F32)], semantics=("parallel",), stages=stages)
    return outs[0], carried


def _mixers_bwd(dy, proj, conv_w, wm, wmt, bias_e, g_a, g_b, head_onehot, seq_len, name, stages=()):
    T, W5 = proj.shape
    W = W5 // 5
    H = W // HEAD_DIM
    nb = T // CHUNK
    per_seq = seq_len // CHUNK
    rb = CHUNK // SUBLANES
    blk = lambda i: nb - 1 - i
    cols, halos = _mixer_specs(W, blk, lambda i: jnp.maximum(blk(i) * rb - 1, 0))

    def body(dy_ref, b_ref, c_ref, hin_ref, u_ref, v_ref, ch_ref, hh_ref, cw_ref, wm_ref, wmt_ref, be_ref, ga_ref,
             gb_ref, oh_ref, dp_ref, dcw_ref, dga_ref, dgb_ref, dws_ref, dbt_ref, carry_ref, s_ref, dgv_ref):
        i = pl.program_id(0)
        j = nb - 1 - i

        @pl.when(i == 0)
        def _():
            for r in (dcw_ref, dga_ref, dgb_ref, dws_ref, dbt_ref, carry_ref):
                r[...] = jnp.zeros_like(r)

        first = (j % per_seq) == 0
        last = (j % per_seq) == per_seq - 1
        b, c, hin = b_ref[...], c_ref[...], hin_ref[...]
        cw = cw_ref[...]
        hc = c * hin
        hc_prev = jnp.where(first, 0.0, ch_ref[...] * hh_ref[...])
        hc1 = _shift_rows(hc, hc_prev, 1)
        hc2 = _shift_rows(hc, hc_prev, 2)
        conv = cw[0:1, :] * hc2 + cw[1:2, :] * hc1 + cw[2:3, :] * hc
        ya = b * conv
        inv_a = _row_inv(ya)
        na = ya * inv_a
        do_a = dy_ref[:, 0:W]
        dga_ref[...] += jnp.sum(do_a * na, axis=0, keepdims=True)
        dna = do_a * ga_ref[...]
        dya = inv_a * (dna - na * jnp.mean(dna * na, axis=-1, keepdims=True))
        dp_ref[:, 0:W] = (dya * conv).astype(BF16)
        dconv = dya * b
        dcw_ref[0:1, :] += jnp.sum(dconv * hc2, axis=0, keepdims=True)
        dcw_ref[1:2, :] += jnp.sum(dconv * hc1, axis=0, keepdims=True)
        dcw_ref[2:3, :] += jnp.sum(dconv * hc, axis=0, keepdims=True)
        nxt = jnp.where(last, 0.0, carry_ref[...])
        dhc = cw[2:3, :] * dconv + cw[1:2, :] * _unshift_rows(dconv, nxt, 1) + cw[0:1, :] * _unshift_rows(dconv, nxt, 2)
        carry_ref[...] = dconv[0:SUBLANES, :]
        dp_ref[:, W:2 * W] = (dhc * hin).astype(BF16)
        dp_ref[:, 2 * W:3 * W] = (dhc * c).astype(BF16)
        u, v = u_ref[...], v_ref[...]
        gu, thu = _gelu(u)
        gv, thv = _gelu(v)
        gvb = gv.astype(BF16)
        for hd in range(H):
            sl = slice(hd * HEAD_DIM, (hd + 1) * HEAD_DIM)
            s_ref[:, sl] = jnp.dot(wm_ref[hd], gvb[:, sl], preferred_element_type=F32)
        s = s_ref[...] + be_ref[...]
        yb = gu * s
        inv_b = _row_inv(yb)
        nbv = yb * inv_b
        do_b = dy_ref[:, W:2 * W]
        dgb_ref[...] += jnp.sum(do_b * nbv, axis=0, keepdims=True)
        dnb = do_b * gb_ref[...]
        dyb = inv_b * (dnb - nbv * jnp.mean(dnb * nbv, axis=-1, keepdims=True))
        dp_ref[:, 3 * W:4 * W] = (dyb * s * _gelu_grad(u, thu)).astype(BF16)
        dsb = (dyb * gu).astype(BF16)
        dbt_ref[...] += jnp.dot(dsb, oh_ref[...], preferred_element_type=F32)
        for hd in range(H):
            sl = slice(hd * HEAD_DIM, (hd + 1) * HEAD_DIM)
            dws_ref[hd] += lax.dot_general(dsb[:, sl], gvb[:, sl], (((1,), (1,)), ((), ())), preferred_element_type=F32)
            dgv_ref[:, sl] = jnp.dot(wmt_ref[hd], dsb[:, sl], preferred_element_type=F32)
        dp_ref[:, 4 * W:5 * W] = (dgv_ref[...] * _gelu_grad(v, thv)).astype(BF16)

    full = lambda shape: pl.BlockSpec(shape, lambda i: (0,) * len(shape))
    return _call(
        body, name=name, grid=(nb,),
        args=[dy, proj, proj, proj, proj, proj, proj, proj, conv_w, wm, wmt, bias_e, g_a, g_b, head_onehot],
        in_specs=[pl.BlockSpec((CHUNK, 2 * W), lambda i: (blk(i), 0))] + cols + halos
        + [full((CONV_K, W)), full((H, CHUNK, CHUNK)), full((H, CHUNK, CHUNK)), full((CHUNK, W)), full((1, W)),
           full((1, W)), full((W, LANES))],
        out_specs=[pl.BlockSpec((CHUNK, 5 * W), lambda i: (blk(i), 0)), full((SUBLANES, W)), full((1, W)), full((1, W)),
                   full((H, CHUNK, CHUNK)), full((CHUNK, LANES))],
        out_shape=[jax.ShapeDtypeStruct((T, 5 * W), BF16), jax.ShapeDtypeStruct((SUBLANES, W), F32),
                   jax.ShapeDtypeStruct((1, W), F32), jax.ShapeDtypeStruct((1, W), F32),
                   jax.ShapeDtypeStruct((H, CHUNK, CHUNK), F32), jax.ShapeDtypeStruct((CHUNK, LANES), F32)],
        scratch_shapes=[pltpu.VMEM((SUBLANES, W), F32), pltpu.VMEM((CHUNK, W), F32), pltpu.VMEM((CHUNK, W), F32)],
        semantics=("arbitrary",), stages=stages)


def _cast_into_slot(w, chip, name):
    R, C = w.shape
    tr = _tile(R, 256, 16)

    def body(chip_ref, w_ref, o_ref):
        o_ref[...] = w_ref[...].astype(BF16)

    return pl.pallas_call(
        body, name=name,
        grid_spec=pltpu.PrefetchScalarGridSpec(
            num_scalar_prefetch=1, grid=(R // tr,),
            in_specs=[pl.BlockSpec((tr, C), lambda i, chip_ref: (i, 0))],
            out_specs=pl.BlockSpec((None, tr, C), lambda i, chip_ref: (chip_ref[0], i, 0))),
        out_shape=jax.ShapeDtypeStruct((N_CHIPS, R, C), BF16),
        compiler_params=_params(("parallel",)),
    )(chip, w)


def _rs_pair_add(dw, got, chip_core, name):
    S, R, C = dw.shape
    hr = R // 2
    tr = _tile(hr, 256, 16)
    nrb = hr // tr

    def body(cc_ref, dw_ref, got_ref, send_ref, own_ref):
        s = dw_ref[...].astype(F32) + got_ref[...].astype(F32)
        send_ref[...] = s.astype(BF16)

        @pl.when(pl.program_id(1) == cc_ref[0])
        def _():
            own_ref[...] = s

    return pl.pallas_call(
        body, name=name,
        grid_spec=pltpu.PrefetchScalarGridSpec(
            num_scalar_prefetch=1, grid=(nrb, S),
            in_specs=[pl.BlockSpec((None, tr, C), lambda i, q, cc: (q, cc[1] * nrb + i, 0)),
                      pl.BlockSpec((None, tr, C), lambda i, q, cc: (q, i, 0))],
            out_specs=[pl.BlockSpec((None, tr, C), lambda i, q, cc: (q, i, 0)),
                       pl.BlockSpec((tr, C), lambda i, q, cc: (i, 0))]),
        out_shape=[jax.ShapeDtypeStruct((S, hr, C), BF16), jax.ShapeDtypeStruct((hr, C), F32)],
        compiler_params=_params(("parallel", "arbitrary")),
    )(chip_core, dw, got)


def _rs_final_add(own, got, chip_core, name):
    hr, C = own.shape
    tr = _tile(hr, 256, 16)
    nrb = hr // tr

    def body(cc_ref, own_ref, got_ref, o_ref):
        o_ref[...] = ((own_ref[...] + got_ref[0].astype(F32)) + got_ref[1].astype(F32)) + got_ref[2].astype(F32)

    return pl.pallas_call(
        body, name=name,
        grid_spec=pltpu.PrefetchScalarGridSpec(
            num_scalar_prefetch=1, grid=(nrb,),
            in_specs=[pl.BlockSpec((tr, C), lambda i, cc: (i, 0)), pl.BlockSpec((3, tr, C), lambda i, cc: (0, i, 0))],
            out_specs=pl.BlockSpec((tr, C), lambda i, cc: (cc[1] * nrb + i, 0))),
        out_shape=jax.ShapeDtypeStruct((2 * hr, C), F32),
        compiler_params=_params(("parallel",)),
    )(chip_core, own, got)


def _adamw_math(w, g, m, v):
    m2 = ADAM_B1 * m + (1.0 - ADAM_B1) * g
    v2 = ADAM_B2 * v + (1.0 - ADAM_B2) * (g * g)
    delta = -ADAM_LR * ((m2 / ADAM_C1) / (jnp.sqrt(v2 / ADAM_C2) + ADAM_EPS) + ADAM_WD * w)
    return delta, m2, v2


def _adamw(w, g, m, v, name):
    R, C = w.shape
    tr = _tile(R, max(SUBLANES, (256 * 1024) // C), SUBLANES)

    def body(w_ref, g_ref, m_ref, v_ref, g2_ref, d_ref, m2_ref, v2_ref):
        g = g_ref[...]
        g2_ref[...] = g
        d_ref[...], m2_ref[...], v2_ref[...] = _adamw_math(w_ref[...], g, m_ref[...], v_ref[...])

    blk = pl.BlockSpec((tr, C), lambda i: (i, 0))
    return _call(body, name=name, args=[w, g, m, v], grid=(R // tr,), in_specs=[blk] * 4, out_specs=[blk] * 4,
                 out_shape=[jax.ShapeDtypeStruct((R, C), F32)] * 4, semantics=("parallel",))[0]


def _adamw_sc(w, g, m, v, name):
    R, C = w.shape
    blk = (SUBLANES, 512)
    assert R % blk[0] == 0 and C % blk[1] == 0
    mesh = plsc.VectorSubcoreMesh(core_axis_name="sc_core", subcore_axis_name="sc_tile")
    spec = pl.BlockSpec(block_shape=blk, index_map=lambda i, j: (i, j))

    def kern(w_hbm, g_hbm, m_hbm, v_hbm, g2_hbm, d_hbm, m2_hbm, v2_hbm):
        def body(w_v, g_v, m_v, v_v, g2_v, d_v, m2_v, v2_v):
            for r in range(blk[0]):
                @plsc.parallel_loop(0, blk[1], SC_LANES, unroll=8)
                def _(c):
                    at = (pl.ds(r, 1), pl.ds(c, SC_LANES))
                    gv = g_v.at[*at][...]
                    g2_v.at[*at][...] = gv
                    d_v.at[*at][...], m2_v.at[*at][...], v2_v.at[*at][...] = _adamw_math(
                        w_v.at[*at][...], gv, m_v.at[*at][...], v_v.at[*at][...])

        pltpu.emit_pipeline(
            body, grid=(R // blk[0], C // blk[1]), in_specs=[spec] * 4, out_specs=[spec] * 4,
            core_axis_name=("sc_core", "sc_tile"), dimension_semantics=(pltpu.PARALLEL, pltpu.PARALLEL),
        )(w_hbm, g_hbm, m_hbm, v_hbm, g2_hbm, d_hbm, m2_hbm, v2_hbm)

    return pl.kernel(kern, name=name, out_type=[jax.ShapeDtypeStruct((R, C), F32)] * 4, mesh=mesh,
                     scratch_types=[])(w, g, m, v)


def _all_gather_small(block, name):
    m_per, n = block.shape

    def body(x_ref, out_ref, send_sems, recv_sems, local_sem):
        x, y, c = _position()
        me, sibling = (x, y, c), (x, y, 1 - c)
        chips = _other_chips(x, y)

        def rows(px, py, pc):
            return out_ref.at[pl.ds((4 * px + 2 * py + pc) * m_per, m_per), :]

        def copy(k, blk, to, src=None):
            return pltpu.make_async_remote_copy(src_ref=rows(*blk) if src is None else src, dst_ref=rows(*blk),
                                                send_sem=send_sems.at[k], recv_sem=recv_sems.at[k], device_id=to,
                                                device_id_type=MESH)

        mine = pltpu.make_async_copy(x_ref, rows(*me), local_sem)
        mine.start()
        first = [copy(0, me, sibling, src=x_ref)]
        first += [copy(1 + j, me, (*chip, c), src=x_ref) for j, chip in enumerate(chips)]
        for cp in first:
            cp.start()
        passed = [copy(4 + j, (*chip, c), sibling) for j, chip in enumerate(chips)]
        for j, chip in enumerate(chips):
            copy(1 + j, (*chip, c), me).wait_recv()
            passed[j].start()
        copy(0, sibling, me).wait_recv()
        for j, chip in enumerate(chips):
            copy(4 + j, (*chip, 1 - c), me).wait_recv()
        for cp in first + passed:
            cp.wait_send()
        mine.wait()

    return pl.pallas_call(
        body, name=name,
        in_specs=[pl.BlockSpec(memory_space=pltpu.VMEM)],
        out_specs=pl.BlockSpec(memory_space=pltpu.VMEM),
        out_shape=jax.ShapeDtypeStruct((N_DEV * m_per, n), block.dtype),
        scratch_shapes=[pltpu.SemaphoreType.DMA((7,)), pltpu.SemaphoreType.DMA((7,)), pltpu.SemaphoreType.DMA],
        compiler_params=pltpu.CompilerParams(vmem_limit_bytes=VMEM_LIMIT_V7X, has_side_effects=True),
    )(block)


def _sum_and_adamw_small(gathered, w, m, v, name):
    rows, n = w.shape
    tr = _tile(rows, 32, SUBLANES)

    def body(p_ref, w_ref, m_ref, v_ref, g_ref, d_ref, m2_ref, v2_ref):
        g = p_ref[0]
        for d in range(1, N_DEV):
            g = g + p_ref[d]
        g_ref[...] = g
        d_ref[...], m2_ref[...], v2_ref[...] = _adamw_math(w_ref[...], g, m_ref[...], v_ref[...])

    blk = pl.BlockSpec((tr, n), lambda i: (i, 0))
    return pl.pallas_call(
        body, name=name, grid=(rows // tr,),
        in_specs=[pl.BlockSpec((N_DEV, tr, n), lambda i: (0, i, 0))] + [blk] * 3,
        out_specs=[blk] * 4,
        out_shape=[jax.ShapeDtypeStruct((rows, n), F32)] * 4,
        compiler_params=_params(("parallel",)),
    )(gathered.reshape(N_DEV, rows, n), w, m, v)


def _pad_rows(a):
    pad = (-a.shape[0]) % SUBLANES
    return jnp.pad(a, ((0, pad), (0, 0))) if pad else a


class _SmallPack:
    def __init__(self, W, D, H, chip):
        self.W, self.D, self.H, self.chip = W, D, H, chip
        self.offsets = {}
        self.rows = 0

    def pack(self, pieces):
        out = []
        self.offsets, self.rows = {}, 0
        for name, a in pieces:
            a = _pad_rows(a.astype(F32))
            self.offsets[name] = (self.rows, a.shape[0])
            self.rows += a.shape[0]
            out.append(a)
        return jnp.concatenate(out, axis=0)

    def piece(self, packed, name):
        start, n = self.offsets[name]
        return packed[start:start + n]


def _bias_rows(b, W):
    bt = jnp.pad(b.T, ((0, 0), (0, LANES - b.shape[0])))
    return bt.reshape(-1, W)


def _bias_from_rows(rows, H):
    return rows.reshape(-1)[:CHUNK * LANES].reshape(CHUNK, LANES)[:, :H].T


def kernel(x, mix_norm_g, w_in, conv_w, spatial_w, spatial_b, conv_out_norm_g, gmlp_out_norm_g, w_out, mlp_norm_g, w_up, w_down, final_norm_g, loss_target, m_mix_norm_g, m_w_in, m_conv_w, m_spatial_w, m_spatial_b, m_conv_out_norm_g, m_gmlp_out_norm_g, m_w_out, m_mlp_norm_g, m_w_up, m_w_down, m_final_norm_g, v_mix_norm_g, v_w_in, v_conv_w, v_spatial_w, v_spatial_b, v_conv_out_norm_g, v_gmlp_out_norm_g, v_w_out, v_mlp_norm_g, v_w_up, v_w_down, v_final_norm_g):
    Bl, S, D = x.shape
    T = Bl * S
    W = conv_out_norm_g.shape[-1]
    H = W // HEAD_DIM
    Wl = conv_w.shape[-1]
    xi, yi, ci = _position()
    chip = (2 * xi + yi).astype(jnp.int32)
    chip_arr = chip.reshape(1)
    chip_core = jnp.stack([chip, ci.astype(jnp.int32)])

    x2 = x.reshape(T, D)
    tgt2 = loss_target.reshape(T, D)

    s_in = _cast_into_slot(w_in[0], chip_arr, "cast_w_in")
    up_rows = w_up.shape[1] // 2
    up_cuts = [0] + [up_rows * pct // 100 // 16 * 16 for pct in (43, 57, 82)] + [up_rows]
    up_part = [(lo, hi - lo) for lo, hi in zip(up_cuts[:-1], up_cuts[1:])]

    causal = jnp.tril(jnp.ones((CHUNK, CHUNK), dtype=bool))
    wm = jnp.where(causal[None], spatial_w[0], 0.0).astype(BF16)
    wmt = jnp.swapaxes(wm, 1, 2)
    bias_e = jnp.repeat(spatial_b[0].T, HEAD_DIM, axis=1)
    conv_full = lax.dynamic_update_slice(jnp.zeros((CONV_K, W), F32), conv_w[0], (0, chip * Wl))
    head_onehot = (jnp.arange(W)[:, None] // HEAD_DIM == jnp.arange(LANES)[None, :]).astype(BF16)
    g_a, g_b = conv_out_norm_g, gmlp_out_norm_g

    xn, (s_out, s_up, s_down), ((g_in,), (conv_gathered,)) = _rmsnorm_fwd_and_casts(
        x2, mix_norm_g, [w_out[0], w_up[0], w_down[0]], chip_arr, "mix_norm_fwd",
        stages=[_GatherRows(s_in), _GatherSmall(_pad_rows(conv_full))])
    conv_w_all = conv_gathered.reshape(N_DEV, SUBLANES, W)[:, :CONV_K]
    conv_w_all = conv_w_all[0] + conv_w_all[2] + conv_w_all[4] + conv_w_all[6]
    proj, ((g_out,), (g_up,)) = _matmul(xn, g_in, mode="nn", name="proj_fwd", tm=1024, tn=512, tk=4096,
                                        out_dtypes=[F32], b_shard="n",
                                        stages=[_GatherRows(s_out), _GatherRows(s_up, *up_part[0])])
    g_out = g_out.reshape(-1, D)
    y, ((g_up,),) = _mixers_fwd(proj, conv_w_all, wm, bias_e, g_a, g_b, S, "mixers_fwd",
                                stages=[_GatherRows(g_up, *up_part[1])])
    h1, ((g_up,),) = _matmul(y, g_out, mode="nn", name="out_proj_fwd", tm=1024, tn=512, tk=4096, out_dtypes=[F32],
                             epilogue=_ep_residual, extras=(x2,), stages=[_GatherRows(g_up, *up_part[2])])
    xn2, ((g_up,),) = _rmsnorm_fwd(h1, mlp_norm_g, "mlp_norm_fwd", stages=[_GatherRows(g_up, *up_part[3])])
    (r, a), ((g_down,),) = _matmul(xn2, g_up, mode="nn", name="up_fwd", tm=1024, tn=1024, tk=4096, n_sub=2,
                                   out_dtypes=[BF16, BF16], epilogue=_ep_relu2, b_shard="n",
                                   stages=[_GatherRows(s_down)])
    g_down = g_down.reshape(-1, D)
    d2, _ = _matmul(a, g_down, mode="nn", name="down_fwd", tm=2048, tn=1024, tk=1024, out_dtypes=[F32])
    dh2, dh2b, d_final_g, loss_part = _loss_and_final_norm_bwd(h1, d2, tgt2, final_norm_g.reshape(1, D),
                                                               "loss_final_norm")

    def rs_adds(dw, got, tag):
        return _rs_pair_add(dw, got, chip_core, f"rs_pair_add_{tag}")

    dw_down, _ = _matmul(a, dh2b, mode="tn", name="down_dw", tm=1024, tn=1024, tk=4096, n_sub=2, out_dtypes=[BF16])
    dw_down = dw_down.reshape(N_CHIPS, -1, D)
    dpre, ((got_down,),) = _matmul(dh2b, g_down, mode="nt", name="down_dx", tm=1024, tn=1024, tk=4096, n_sub=2,
                                   out_dtypes=[BF16], epilogue=_ep_relu2_bwd, extras=(r,),
                                   stages=[_PairExchange(dw_down)])
    part_down, own_down = rs_adds(dw_down, got_down, "w_down")
    rows_down = part_down.shape[1]
    down_a = rows_down * 3 // 4 // 16 * 16
    dw_up, ((landed_down,),) = _matmul(xn2, dpre, mode="tn", name="up_dw", tm=1024, tn=1024, tk=4096, n_sub=2,
                                       out_dtypes=[BF16], out_shard=True,
                                       stages=[_ChipExchange(part_down, None, 0, down_a)])
    dxn2, ((got_up,), (landed_down,)) = _matmul(
        dpre, g_up, mode="nt", name="up_dx", tm=2048, tn=1024, tk=1024, out_dtypes=[F32], b_shard="k",
        stages=[_PairExchange(dw_up), _ChipExchange(part_down, landed_down, down_a, rows_down - down_a)])
    half_down = _rs_final_add(own_down, landed_down, chip_core, "rs_final_add_w_down")
    part_up, own_up = rs_adds(dw_up, got_up, "w_up")
    (dh1, dh1b, d_mlp_g), ((grad_down,),) = _rmsnorm_bwd(dxn2, h1, mlp_norm_g, dh2, "mlp_norm_bwd",
                                                        stages=[_HalfExchange(half_down)])
    rows_up = part_up.shape[1]
    q_up = rows_up // 4 // 16 * 16
    dy, ((landed_up,),) = _matmul(dh1b, g_out, mode="nt", name="out_proj_dx", tm=1024, tn=512, tk=4096,
                                  out_dtypes=[F32], stages=[_ChipExchange(part_up, None, 0, q_up)])
    (dproj, d_conv, d_ga, d_gb, d_ws, d_bt), ((landed_up,),) = _mixers_bwd(
        dy, proj, conv_w_all, wm, wmt, bias_e, g_a, g_b, head_onehot, S, "mixers_bwd",
        stages=[_ChipExchange(part_up, landed_up, q_up, q_up)])
    dw_in, ((landed_up,),) = _matmul(
        xn, dproj, mode="tn", name="proj_dw", tm=1024, tn=512, tk=4096, out_dtypes=[BF16], out_shard=True,
        stages=[_ChipExchange(part_up, landed_up, 2 * q_up, rows_up - 2 * q_up)])
    half_up = _rs_final_add(own_up, landed_up, chip_core, "rs_final_add_w_up")
    pack = _SmallPack(W, D, H, chip)
    loss_row = jnp.pad(loss_part[:, :1], ((0, 0), (0, W - 1)))
    g_part = pack.pack([("spatial_w", (d_ws * causal.astype(F32)[None]).reshape(-1, W)), ("conv_w", d_conv),
                        ("mlp_norm_g", d_mlp_g.reshape(-1, W)), ("final_norm_g", d_final_g.reshape(-1, W)),
                        ("conv_out_norm_g", d_ga), ("gmlp_out_norm_g", d_gb), ("spatial_b", d_bt.reshape(-1, W)),
                        ("loss", loss_row)])
    dw_out, ((got_in,), (grad_up,), (g_all,)) = _matmul(
        y, dh1b, mode="tn", name="out_proj_dw", tm=1024, tn=1024, tk=4096, n_sub=2, out_dtypes=[BF16],
        stages=[_PairExchange(dw_in), _HalfExchange(half_up), _GatherSmall(g_part)])
    dw_out = dw_out.reshape(N_CHIPS, -1, D)
    part_in, own_in = rs_adds(dw_in, got_in, "w_in")
    dxn, ((landed_in,), (got_out,)) = _matmul(
        dproj, g_in, mode="nt", name="proj_dx", tm=2048, tn=1024, tk=1280, out_dtypes=[F32], b_shard="k",
        stages=[_ChipExchange(part_in), _PairExchange(dw_out)])
    part_out, own_out = rs_adds(dw_out, got_out, "w_out")
    half_in = _rs_final_add(own_in, landed_in, chip_core, "rs_final_add_w_in")
    (grad_x, _unused, d_mix_g), ((landed_out,), (grad_in,)) = _rmsnorm_bwd(
        dxn, x2, mix_norm_g, dh1, "mix_norm_bwd", stages=[_ChipExchange(part_out), _HalfExchange(half_in)])
    half_out = _rs_final_add(own_out, landed_out, chip_core, "rs_final_add_w_out")
    ((grad_out,),) = _run_stages([_HalfExchange(half_out)], "rs_half_exchange_w_out")
    big = {"w_down": _adamw_sc(w_down[0], grad_down, m_w_down[0], v_w_down[0], "adamw_sc_w_down"),
           "w_up": _adamw_sc(w_up[0], grad_up, m_w_up[0], v_w_up[0], "adamw_sc_w_up"),
           "w_in": _adamw(w_in[0], grad_in, m_w_in[0], v_w_in[0], "adamw_w_in"),
           "w_out": _adamw(w_out[0], grad_out, m_w_out[0], v_w_out[0], "adamw_w_out")}
    big = {k: [t[None] for t in v] for k, v in big.items()}

    def small(conv, sw, sb, ga, gb, mlp, fin):
        return pack.pack([("spatial_w", sw.reshape(-1, W)), ("conv_w", conv), ("mlp_norm_g", mlp.reshape(-1, W)),
                          ("final_norm_g", fin.reshape(-1, W)), ("conv_out_norm_g", ga.reshape(-1, W)),
                          ("gmlp_out_norm_g", gb.reshape(-1, W)), ("spatial_b", _bias_rows(sb, W)),
                          ("loss", jnp.zeros((1, W), F32))])

    def full_conv(cw):
        return lax.dynamic_update_slice(jnp.zeros((CONV_K, W), F32), cw[0], (0, chip * Wl))

    def mix_rows(a):
        return _pad_rows(a.reshape(-1, W))

    w_s = small(full_conv(conv_w), spatial_w, spatial_b[0], conv_out_norm_g, gmlp_out_norm_g, mlp_norm_g, final_norm_g)
    m_s = small(full_conv(m_conv_w), m_spatial_w, m_spatial_b[0], m_conv_out_norm_g, m_gmlp_out_norm_g, m_mlp_norm_g,
                m_final_norm_g)
    v_s = small(full_conv(v_conv_w), v_spatial_w, v_spatial_b[0], v_conv_out_norm_g, v_gmlp_out_norm_g, v_mlp_norm_g,
                v_final_norm_g)
    small_outs = _sum_and_adamw_small(g_all, w_s, m_s, v_s, "sum_adamw_small")
    mix_all = _all_gather_small(mix_rows(d_mix_g), "all_gather_mix_norm_grad")
    mix_outs = _sum_and_adamw_small(mix_all, mix_rows(mix_norm_g), mix_rows(m_mix_norm_g), mix_rows(v_mix_norm_g),
                                    "sum_adamw_mix_norm")

    def unpack(kind, name):
        if name == "mix_norm_g":
            return mix_outs[kind].reshape(-1)[:D].reshape(1, D)
        rows = pack.piece(small_outs[kind], name)
        if name == "spatial_w":
            return rows.reshape(1, H, CHUNK, CHUNK)
        if name == "conv_w":
            return lax.dynamic_slice(rows[:CONV_K], (0, chip * Wl), (CONV_K, Wl))[None]
        if name == "spatial_b":
            return _bias_from_rows(rows, H)[None]
        if name == "final_norm_g":
            return rows.reshape(-1)[:D]
        n = D if name == "mlp_norm_g" else W
        return rows.reshape(-1)[:n].reshape(1, n)

    loss = pack.piece(small_outs[0], "loss")[0, 0]
    order = ["mix_norm_g", "w_in", "conv_w", "spatial_w", "spatial_b", "conv_out_norm_g", "gmlp_out_norm_g", "w_out",
             "mlp_norm_g", "w_up", "w_down", "final_norm_g"]
    outs = [loss, grad_x.reshape(Bl, S, D)]
    for kind in range(4):
        for name in order:
            outs.append(big[name][kind] if name in big else unpack(kind, name))
    return tuple(outs)
```

```python
import functools
import math

import jax
import jax.numpy as jnp
from jax import lax
from jax.experimental import pallas as pl
from jax.experimental.pallas import tpu as pltpu
from jax.experimental.pallas import tpu_sc as plsc

F32 = jnp.float32
BF16 = jnp.bfloat16
MESH = pl.DeviceIdType.MESH

NORM_EPS = 1e-5
HEAD_DIM = 128
CHUNK = 128
CONV_K = 3
N_CHIPS = 4
N_DEV = 8

ADAM_LR = 0.001
ADAM_B1 = 0.9
ADAM_B2 = 0.999
ADAM_EPS = 1e-08
ADAM_WD = 0.01
ADAM_STEP = 10
ADAM_C1 = 1.0 - ADAM_B1 ** ADAM_STEP
ADAM_C2 = 1.0 - ADAM_B2 ** ADAM_STEP

GELU_K = math.sqrt(2.0 / math.pi)
GELU_A = 0.044715

VMEM_LIMIT_V7X = 56 * 1024 * 1024
SUBLANES = 8
LANES = 128
SC_LANES = 16


def _tile(dim, target, mult=LANES):
    if dim <= target:
        return dim
    t = (target // mult) * mult
    while t > mult and dim % t:
        t -= mult
    assert dim % t == 0, (dim, target, mult)
    return t


def _params(sem=None):
    return pltpu.CompilerParams(dimension_semantics=sem, vmem_limit_bytes=VMEM_LIMIT_V7X)


class _Stage:
    bufs = ()
    n_sems = 0
    MIDDLE_AT = 0.6
    base = 0

    def start(self, refs, send, recv):
        raise NotImplementedError

    def middle(self, refs, send, recv):
        pass

    def finish(self, refs, send, recv):
        raise NotImplementedError


def _position():
    return lax.axis_index("x"), lax.axis_index("y"), lax.axis_index("c")


def _other_chips(x, y):
    return [(1 - x, y), (x, 1 - y), (1 - x, 1 - y)]


def _remote(src, dst, send, recv, k, to):
    return pltpu.make_async_remote_copy(src_ref=src, dst_ref=dst, send_sem=send.at[k], recv_sem=recv.at[k],
                                        device_id=to, device_id_type=MESH)


def _call(body, *, name, args, in_specs, out_specs, out_shape, grid=(), scratch_shapes=(), semantics=None, stages=(),
          prefetch=None):
    n_in, n_out, n_scratch = len(args), len(out_shape), len(scratch_shapes)
    n_pre = 0 if prefetch is None else 1
    any_spec = pl.BlockSpec(memory_space=pl.ANY)
    extra_args, extra_out, aliases, layout = [], [], {}, []
    for st in stages:
        where = []
        for kind, buf in st.bufs:
            if kind in ("in", "alias"):
                extra_args.append(buf)
                pos_in = n_in + len(extra_args) - 1
            if kind in ("out", "alias"):
                extra_out.append(jax.ShapeDtypeStruct(buf.shape, buf.dtype))
                pos_out = n_out + len(extra_out) - 1
            if kind == "alias":
                aliases[n_pre + pos_in] = pos_out
            where.append(("in", pos_in) if kind == "in" else ("out", pos_out))
        layout.append(where)
    n_sems = sum(st.n_sems for st in stages)
    n_xin, n_xout = len(extra_args), len(extra_out)

    def wrapped(*refs):
        pre, refs = refs[:n_pre], refs[n_pre:]
        ins = refs[:n_in + n_xin]
        outs = refs[n_in + n_xin:n_in + n_xin + n_out + n_xout]
        scratch = refs[n_in + n_xin + n_out + n_xout:]
        main = pre + ins[:n_in] + outs[:n_out] + scratch[:n_scratch]
        if not stages:
            body(*main)
            return
        send, recv = scratch[n_scratch], scratch[n_scratch + 1]
        step, n_steps = 0, 1
        for d, g in enumerate(grid):
            step = step * g + pl.program_id(d)
            n_steps *= g
        base, views = 0, []
        for st, where in zip(stages, layout):
            st_refs = [ins[p] if side == "in" else outs[p] for side, p in where]
            st.base = base
            views.append((st, st_refs, send, recv))
            base += st.n_sems

        def starts():
            for st, r, s, v in views:
                st.start(r, s, v)

        def middles():
            for st, r, s, v in views:
                st.middle(r, s, v)

        def finishes():
            for st, r, s, v in views:
                st.finish(r, s, v)

        if not grid:
            starts()
            body(*main)
            middles()
            finishes()
        else:
            pl.when(step == 0)(starts)
            body(*main)
            pl.when(step == min(int(n_steps * _Stage.MIDDLE_AT), n_steps - 1))(middles)
            pl.when(step == n_steps - 1)(finishes)

    sems = [pltpu.SemaphoreType.DMA((n_sems,)), pltpu.SemaphoreType.DMA((n_sems,))] if stages else []
    specs = dict(in_specs=list(in_specs) + [any_spec] * n_xin, out_specs=list(out_specs) + [any_spec] * n_xout,
                 scratch_shapes=list(scratch_shapes) + sems)
    if prefetch is None:
        kw = dict(specs, **(dict(grid=grid) if grid else {}))
    else:
        kw = dict(grid_spec=pltpu.PrefetchScalarGridSpec(num_scalar_prefetch=1, grid=grid, **specs))
    res = pl.pallas_call(
        wrapped, name=name,
        out_shape=list(out_shape) + extra_out,
        input_output_aliases=aliases,
        compiler_params=pltpu.CompilerParams(
            dimension_semantics=("arbitrary",) * len(grid) if stages and grid else semantics,
            vmem_limit_bytes=VMEM_LIMIT_V7X, has_side_effects=bool(stages)),
        **kw,
    )(*([] if prefetch is None else [prefetch]), *args, *extra_args)
    main_res, stage_res, pos = list(res[:n_out]), [], n_out
    for st in stages:
        k = sum(kind in ("out", "alias") for kind, _ in st.bufs)
        stage_res.append(list(res[pos:pos + k]))
        pos += k
    return main_res, stage_res


def _run_stages(stages, name):
    return _call(lambda: None, name=name, args=[], in_specs=[], out_specs=[], out_shape=[], stages=stages)[1]


class _GatherRows(_Stage):
    n_sems = 8

    def __init__(self, g, lo=0, n=None):
        self.hr = g.shape[1] // 2
        self.lo, self.n = lo, (self.hr if n is None else n)
        self.n0 = self.n // 2 // 16 * 16
        self.bufs = [("alias", g)]

    def _copy(self, g_ref, send, recv, k, chip_xy, half, to, lo=0, n=None):
        n = self.n if n is None else n
        blk = g_ref.at[2 * chip_xy[0] + chip_xy[1], pl.ds(half * self.hr + self.lo + lo, n), :]
        return _remote(blk, blk, send, recv, self.base + k, to)

    def _plan(self, g, send, recv):
        x, y, c = _position()
        me, sib = (x, y, c), (x, y, 1 - c)
        cx, cy, cd = _other_chips(x, y)
        n0, n1 = self.n0, self.n - self.n0
        mine = [((x, y), c, (*cx, c)), ((x, y), c, (*cy, c)), (cx, c, sib), (cy, c, sib),
                (cx, c, (*cy, c), 0, n0), (cy, c, (*cx, c), n0, n1), (cd, c, sib, 0, n0), (cd, c, sib, n0, n1)]
        theirs = [(cx, c, me), (cy, c, me), (cx, 1 - c, me), (cy, 1 - c, me),
                  (cd, c, me, 0, n0), (cd, c, me, n0, n1), (cd, 1 - c, me, 0, n0), (cd, 1 - c, me, n0, n1)]
        return (lambda k: self._copy(g, send, recv, k, *mine[k])), (lambda k: self._copy(g, send, recv, k, *theirs[k]))

    def start(self, refs, send, recv):
        mine, _ = self._plan(refs[0], send, recv)
        mine(0).start()
        mine(1).start()

    def middle(self, refs, send, recv):
        mine, theirs = self._plan(refs[0], send, recv)
        theirs(0).wait_recv()
        mine(4).start()
        mine(2).start()
        theirs(1).wait_recv()
        mine(5).start()
        mine(3).start()

    def finish(self, refs, send, recv):
        mine, theirs = self._plan(refs[0], send, recv)
        theirs(4).wait_recv()
        mine(6).start()
        theirs(5).wait_recv()
        mine(7).start()
        for k in (2, 3, 6, 7):
            theirs(k).wait_recv()
        for k in range(self.n_sems):
            mine(k).wait_send()


class _GatherSmall(_Stage):
    n_sems = 8

    def __init__(self, block):
        self.m = block.shape[0]
        self.bufs = [("in", block), ("out", jax.ShapeDtypeStruct((N_DEV * self.m, block.shape[1]), block.dtype))]

    def _rows(self, out, px, py, pc):
        return out.at[pl.ds((4 * px + 2 * py + pc) * self.m, self.m), :]

    def _copy(self, refs, send, recv, k, blk, to, own=False):
        dst = self._rows(refs[1], *blk)
        return _remote(refs[0] if own else dst, dst, send, recv, self.base + k, to)

    def _local(self, refs, send):
        return pltpu.make_async_copy(refs[0], self._rows(refs[1], *_position()), send.at[self.base + 7])

    def start(self, refs, send, recv):
        x, y, c = _position()
        self._local(refs, send).start()
        self._copy(refs, send, recv, 0, (x, y, c), (x, y, 1 - c), own=True).start()
        for j, chip in enumerate(_other_chips(x, y)):
            self._copy(refs, send, recv, 1 + j, (x, y, c), (*chip, c), own=True).start()

    def finish(self, refs, send, recv):
        x, y, c = _position()
        me, sib, chips = (x, y, c), (x, y, 1 - c), _other_chips(x, y)
        for j, chip in enumerate(chips):
            self._copy(refs, send, recv, 1 + j, (*chip, c), me).wait_recv()
            self._copy(refs, send, recv, 4 + j, (*chip, c), sib).start()
        self._copy(refs, send, recv, 0, sib, me).wait_recv()
        for j, chip in enumerate(chips):
            self._copy(refs, send, recv, 4 + j, (*chip, 1 - c), me).wait_recv()
        self._copy(refs, send, recv, 0, me, sib, own=True).wait_send()
        for j, chip in enumerate(chips):
            self._copy(refs, send, recv, 1 + j, me, (*chip, c), own=True).wait_send()
            self._copy(refs, send, recv, 4 + j, (*chip, c), sib).wait_send()
        self._local(refs, send).wait()


class _PairExchange(_Stage):
    n_sems = 1

    def __init__(self, dw):
        S, R, C = dw.shape
        self.hr = R // 2
        self.bufs = [("in", dw), ("out", jax.ShapeDtypeStruct((S, self.hr, C), dw.dtype))]

    def _copy(self, refs, send, recv):
        x, y, c = _position()
        return _remote(refs[0].at[:, pl.ds((1 - c) * self.hr, self.hr), :], refs[1], send, recv, self.base,
                       (x, y, 1 - c))

    def start(self, refs, send, recv):
        self._copy(refs, send, recv).start()

    def finish(self, refs, send, recv):
        cp = self._copy(refs, send, recv)
        cp.wait_recv()
        cp.wait_send()


class _ChipExchange(_Stage):
    n_sems = 3

    def __init__(self, part, landed=None, lo=0, n=None):
        S, hr, C = part.shape
        self.lo, self.n = lo, (hr if n is None else n)
        self.bufs = [("in", part), ("out", jax.ShapeDtypeStruct((3, hr, C), part.dtype)) if landed is None
                     else ("alias", landed)]

    def _copies(self, refs, send, recv):
        x, y, c = _position()
        rows = pl.ds(self.lo, self.n)
        return [_remote(refs[0].at[2 * chip[0] + chip[1], rows, :], refs[1].at[j, rows, :], send, recv,
                        self.base + j, (*chip, c))
                for j, chip in enumerate(_other_chips(x, y))]

    def start(self, refs, send, recv):
        for cp in self._copies(refs, send, recv):
            cp.start()

    def finish(self, refs, send, recv):
        copies = self._copies(refs, send, recv)
        for cp in copies:
            cp.wait_recv()
        for cp in copies:
            cp.wait_send()


class _HalfExchange(_Stage):
    n_sems = 1

    def __init__(self, grad):
        self.hr = grad.shape[0] // 2
        self.bufs = [("alias", grad)]

    def start(self, refs, send, recv):
        x, y, c = _position()
        mine = refs[0].at[pl.ds(c * self.hr, self.hr), :]
        _remote(mine, mine, send, recv, self.base, (x, y, 1 - c)).start()

    def finish(self, refs, send, recv):
        x, y, c = _position()
        mine = refs[0].at[pl.ds(c * self.hr, self.hr), :]
        theirs = refs[0].at[pl.ds((1 - c) * self.hr, self.hr), :]
        _remote(theirs, theirs, send, recv, self.base, (x, y, 1 - c)).wait_recv()
        _remote(mine, mine, send, recv, self.base, (x, y, 1 - c)).wait_send()


def _matmul(a, b, *, mode, name, tm, tn, tk, out_dtypes, epilogue=None, extras=(), b_shard=None, out_shard=False,
            stages=(), n_sub=1, m_blocks=None):
    if mode == "tn":
        K, M = a.shape
    else:
        M, K = a.shape
    if b_shard == "n":
        S, Kb, Ns = b.shape
        N = S * Ns
    elif b_shard == "k":
        S, N, Ks = b.shape
        Kb = S * Ks
    elif mode == "nt":
        N, Kb = b.shape
    else:
        Kb, N = b.shape
    assert Kb == K, (name, a.shape, b.shape)
    tm, tn, tk = _tile(M, tm), _tile(N, tn), _tile(K, tk)
    if b_shard == "n" or out_shard:
        n_per = N // N_CHIPS
        tn = _tile(n_per, tn)
        njs = n_per // tn
    if b_shard == "k":
        tk = _tile(K // N_CHIPS, tk)
        nks = (K // N_CHIPS) // tk
    gm, gn, gk = M // tm, N // tn, K // tk
    if gk > 1 or tn % (n_sub * LANES):
        n_sub = 1
    i0 = 0
    if m_blocks is not None:
        i0, gm = m_blocks
        M = gm * tm

    if mode == "tn":
        a_spec = pl.BlockSpec((tk, tm), lambda i, j, k: (k, i + i0))
        dims = (((0,), (0,)), ((), ()))
    else:
        a_spec = pl.BlockSpec((tm, tk), lambda i, j, k: (i + i0, k))
        dims = (((1,), (1,)), ((), ())) if mode == "nt" else (((1,), (0,)), ((), ()))
    if b_shard == "n":
        b_spec = pl.BlockSpec((None, tk, tn), lambda i, j, k: (j // njs, k, j % njs))
    elif b_shard == "k":
        b_spec = pl.BlockSpec((None, tn, tk), lambda i, j, k: (k // nks, j, k % nks))
    elif mode == "nt":
        b_spec = pl.BlockSpec((tn, tk), lambda i, j, k: (j, k))
    else:
        b_spec = pl.BlockSpec((tk, tn), lambda i, j, k: (k, j))
    mn_spec = pl.BlockSpec((tm, tn), lambda i, j, k: (i, j))
    if out_shard:
        out_spec = pl.BlockSpec((None, tm, tn), lambda i, j, k: (j // njs, i, j % njs))
        out_shape = [jax.ShapeDtypeStruct((N_CHIPS, M, N // N_CHIPS), dt) for dt in out_dtypes]
    else:
        out_spec = mn_spec
        out_shape = [jax.ShapeDtypeStruct((M, N), dt) for dt in out_dtypes]
    n_extra, n_out = len(extras), len(out_dtypes)

    def finish_tile(acc, extra_refs, out_refs):
        if epilogue is None:
            for o in out_refs:
                o[...] = acc.astype(o.dtype)
        else:
            epilogue(acc, extra_refs, out_refs)

    def body(*refs):
        a_ref, b_ref = refs[0], refs[1]
        extra_refs = refs[2:2 + n_extra]
        out_refs = refs[2 + n_extra:2 + n_extra + n_out]

        def product():
            return lax.dot_general(a_ref[...], b_ref[...], dims, preferred_element_type=F32)

        if gk == 1:
            sub = tn // n_sub
            for h in range(n_sub):
                cols = slice(h * sub, (h + 1) * sub)
                b_part = b_ref[cols, :] if mode == "nt" else b_ref[:, cols]
                acc = lax.dot_general(a_ref[...], b_part, dims, preferred_element_type=F32)
                finish_tile(acc, [e.at[:, cols] for e in extra_refs], [o.at[:, cols] for o in out_refs])
            return
        acc_ref = refs[-1]
        k = pl.program_id(2)

        @pl.when(k == 0)
        def _():
            acc_ref[...] = product()

        @pl.when((k > 0) & (k < gk - 1))
        def _():
            acc_ref[...] += product()

        @pl.when(k == gk - 1)
        def _():
            finish_tile(acc_ref[...] + product(), extra_refs, out_refs)

    outs, carried = _call(
        body, name=name, args=[a, b, *extras], grid=(gm, gn, gk),
        in_specs=[a_spec, b_spec] + [mn_spec] * n_extra, out_specs=[out_spec] * n_out, out_shape=out_shape,
        scratch_shapes=[pltpu.VMEM((tm, tn), F32)] if gk > 1 else [],
        semantics=("parallel", "parallel", "arbitrary"), stages=stages)
    return (outs[0] if n_out == 1 else outs), carried


def _ep_residual(acc, extra_refs, out_refs):
    out_refs[0][...] = extra_refs[0][...] + acc


def _ep_relu2(acc, extra_refs, out_refs):
    r = jnp.maximum(acc, 0.0)
    out_refs[0][...] = r.astype(BF16)
    out_refs[1][...] = (r * r).astype(BF16)


def _ep_relu2_bwd(acc, extra_refs, out_refs):
    out_refs[0][...] = (acc * (2.0 * extra_refs[0][...].astype(F32))).astype(BF16)


def _row_inv(x):
    return lax.rsqrt(jnp.mean(x * x, axis=-1, keepdims=True) + NORM_EPS)


def _rmsnorm_fwd(x, g, name, stages=()):
    T, D = x.shape
    tt = _tile(T, 256, SUBLANES)

    def body(x_ref, g_ref, o_ref):
        xv = x_ref[...]
        o_ref[...] = (xv * _row_inv(xv) * g_ref[...]).astype(BF16)

    outs, carried = _call(
        body, name=name, args=[x, g], grid=(T // tt,),
        in_specs=[pl.BlockSpec((tt, D), lambda i: (i, 0)), pl.BlockSpec((1, D), lambda i: (0, 0))],
        out_specs=[pl.BlockSpec((tt, D), lambda i: (i, 0))], out_shape=[jax.ShapeDtypeStruct((T, D), BF16)],
        semantics=("parallel",), stages=stages)
    return outs[0], carried


def _rmsnorm_fwd_and_casts(x, g, weights, chip, name, stages=()):
    T, D = x.shape
    tt = _tile(T, 256, SUBLANES)
    n, nw = T // tt, len(weights)
    rows = [w.shape[0] // n for w in weights]
    assert all(r % 16 == 0 and r * n == w.shape[0] for r, w in zip(rows, weights))

    def body(chip_ref, x_ref, g_ref, *refs):
        w_refs, o_ref, slot_refs = refs[:nw], refs[nw], refs[nw + 1:]
        xv = x_ref[...]
        o_ref[...] = (xv * _row_inv(xv) * g_ref[...]).astype(BF16)
        for w_ref, s_ref in zip(w_refs, slot_refs):
            s_ref[...] = w_ref[...].astype(BF16)

    outs, carried = _call(
        body, name=name, args=[x, g, *weights], grid=(n,), prefetch=chip,
        in_specs=[pl.BlockSpec((tt, D), lambda i, chip_ref: (i, 0)), pl.BlockSpec((1, D), lambda i, chip_ref: (0, 0))]
        + [pl.BlockSpec((r, w.shape[1]), lambda i, chip_ref: (i, 0)) for r, w in zip(rows, weights)],
        out_specs=[pl.BlockSpec((tt, D), lambda i, chip_ref: (i, 0))]
        + [pl.BlockSpec((None, r, w.shape[1]), lambda i, chip_ref: (chip_ref[0], i, 0)) for r, w in zip(rows, weights)],
        out_shape=[jax.ShapeDtypeStruct((T, D), BF16)]
        + [jax.ShapeDtypeStruct((N_CHIPS, *w.shape), BF16) for w in weights],
        semantics=("parallel",), stages=stages)
    return outs[0], outs[1:], carried


def _rmsnorm_bwd(dxn, h, g, dres, name, stages=()):
    T, D = h.shape
    tt = _tile(T, 128, SUBLANES)

    def body(dxn_ref, h_ref, g_ref, dres_ref, dh_ref, dhb_ref, dg_ref):
        @pl.when(pl.program_id(0) == 0)
        def _():
            dg_ref[...] = jnp.zeros_like(dg_ref)

        hv = h_ref[...]
        inv = _row_inv(hv)
        n = hv * inv
        d = dxn_ref[...]
        dg_ref[...] += jnp.sum(d * n, axis=0, keepdims=True)
        dn = d * g_ref[...]
        dh = dres_ref[...] + inv * (dn - n * jnp.mean(dn * n, axis=-1, keepdims=True))
        dh_ref[...] = dh
        dhb_ref[...] = dh.astype(BF16)

    row = pl.BlockSpec((tt, D), lambda i: (i, 0))
    vec = pl.BlockSpec((1, D), lambda i: (0, 0))
    return _call(
        body, name=name, args=[dxn, h, g, dres], grid=(T // tt,), in_specs=[row, row, vec, row],
        out_specs=[row, row, vec],
        out_shape=[jax.ShapeDtypeStruct((T, D), F32), jax.ShapeDtypeStruct((T, D), BF16),
                   jax.ShapeDtypeStruct((1, D), F32)],
        semantics=("arbitrary",), stages=stages)


def _loss_and_final_norm_bwd(h1, d2, tgt, g, name):
    T, D = h1.shape
    tt = _tile(T, 128, SUBLANES)

    def body(h1_ref, d2_ref, t_ref, g_ref, dh_ref, dhb_ref, dg_ref, loss_ref):
        @pl.when(pl.program_id(0) == 0)
        def _():
            dg_ref[...] = jnp.zeros_like(dg_ref)
            loss_ref[...] = jnp.zeros_like(loss_ref)

        hv = h1_ref[...] + d2_ref[...]
        gv = g_ref[...]
        inv = _row_inv(hv)
        n = hv * inv
        err = n * gv - t_ref[...]
        loss_ref[...] += 0.5 * jnp.sum(jnp.mean(err * err, axis=-1, keepdims=True))
        dy = err * (1.0 / D)
        dg_ref[...] += jnp.sum(dy * n, axis=0, keepdims=True)
        dn = dy * gv
        dh = inv * (dn - n * jnp.mean(dn * n, axis=-1, keepdims=True))
        dh_ref[...] = dh
        dhb_ref[...] = dh.astype(BF16)

    row = pl.BlockSpec((tt, D), lambda i: (i, 0))
    vec = pl.BlockSpec((1, D), lambda i: (0, 0))
    one = pl.BlockSpec((1, LANES), lambda i: (0, 0))
    return _call(
        body, name=name, args=[h1, d2, tgt, g], grid=(T // tt,), in_specs=[row, row, row, vec],
        out_specs=[row, row, vec, one],
        out_shape=[jax.ShapeDtypeStruct((T, D), F32), jax.ShapeDtypeStruct((T, D), BF16),
                   jax.ShapeDtypeStruct((1, D), F32), jax.ShapeDtypeStruct((1, LANES), F32)],
        semantics=("arbitrary",))[0]


def _gelu(x):
    th = jnp.tanh(GELU_K * (x + GELU_A * (x * x * x)))
    return 0.5 * x * (1.0 + th), th


def _gelu_grad(x, th):
    return 0.5 * (1.0 + th) + 0.5 * x * (1.0 - th * th) * (GELU_K * (1.0 + 3.0 * GELU_A * (x * x)))


def _shift_rows(cur, prev_rows, k):
    rolled = pltpu.roll(cur, k, 0)
    row = lax.broadcasted_iota(jnp.int32, cur.shape, 0)
    out = rolled
    for r in range(k):
        out = jnp.where(row == r, prev_rows[SUBLANES - k + r:SUBLANES - k + r + 1, :], out)
    return out


def _unshift_rows(cur, next_rows, k):
    n = cur.shape[0]
    rolled = pltpu.roll(cur, n - k, 0)
    row = lax.broadcasted_iota(jnp.int32, cur.shape, 0)
    out = rolled
    for r in range(k):
        out = jnp.where(row == n - k + r, next_rows[r:r + 1, :], out)
    return out


def _mixer_specs(W, blk, halo):
    cols = [pl.BlockSpec((CHUNK, W), functools.partial(lambda i, col: (blk(i), col), col=col)) for col in range(5)]
    halos = [pl.BlockSpec((SUBLANES, W), functools.partial(lambda i, col: (halo(i), col), col=col)) for col in (1, 2)]
    return cols, halos


def _mixers_fwd(proj, conv_w, wm, bias_e, g_a, g_b, seq_len, name, stages=()):
    T, W5 = proj.shape
    W = W5 // 5
    H = W // HEAD_DIM
    per_seq = seq_len // CHUNK
    rb = CHUNK // SUBLANES
    cols, halos = _mixer_specs(W, lambda i: i, lambda i: jnp.maximum(i * rb - 1, 0))

    def body(b_ref, c_ref, hin_ref, u_ref, v_ref, ch_ref, hh_ref, cw_ref, wm_ref, be_ref, ga_ref, gb_ref, y_ref, s_ref):
        first = (pl.program_id(0) % per_seq) == 0
        hc = c_ref[...] * hin_ref[...]
        hc_prev = jnp.where(first, 0.0, ch_ref[...] * hh_ref[...])
        cw = cw_ref[...]
        ya = b_ref[...] * (cw[0:1, :] * _shift_rows(hc, hc_prev, 2) + cw[1:2, :] * _shift_rows(hc, hc_prev, 1)
                           + cw[2:3, :] * hc)
        y_ref[:, 0:W] = (ya * _row_inv(ya) * ga_ref[...]).astype(BF16)
        gu, _ = _gelu(u_ref[...])
        gv, _ = _gelu(v_ref[...])
        gvb = gv.astype(BF16)
        for hd in range(H):
            sl = slice(hd * HEAD_DIM, (hd + 1) * HEAD_DIM)
            s_ref[:, sl] = jnp.dot(wm_ref[hd], gvb[:, sl], preferred_element_type=F32)
        yb = gu * (s_ref[...] + be_ref[...])
        y_ref[:, W:2 * W] = (yb * _row_inv(yb) * gb_ref[...]).astype(BF16)

    full = lambda shape: pl.BlockSpec(shape, lambda i: (0,) * len(shape))
    outs, carried = _call(
        body, name=name, args=[proj, proj, proj, proj, proj, proj, proj, conv_w, wm, bias_e, g_a, g_b],
        grid=(T // CHUNK,),
        in_specs=cols + halos + [full((CONV_K, W)), full((H, CHUNK, CHUNK)), full((CHUNK, W)), full((1, W)), full((1, W))],
        out_specs=[pl.BlockSpec((CHUNK, 2 * W), lambda i: (i, 0))], out_shape=[jax.ShapeDtypeStruct((T, 2 * W), BF16)],
        scratch_shapes=[pltpu.VMEM((CHUNK, W), F32)], semantics=("parallel",), stages=stages)
    return outs[0], carried


def _mixers_bwd(dy, proj, conv_w, wm, wmt, bias_e, g_a, g_b, head_onehot, seq_len, name, stages=()):
    T, W5 = proj.shape
    W = W5 // 5
    H = W // HEAD_DIM
    nb = T // CHUNK
    per_seq = seq_len // CHUNK
    rb = CHUNK // SUBLANES
    blk = lambda i: nb - 1 - i
    cols, halos = _mixer_specs(W, blk, lambda i: jnp.maximum(blk(i) * rb - 1, 0))

    def body(dy_ref, b_ref, c_ref, hin_ref, u_ref, v_ref, ch_ref, hh_ref, cw_ref, wm_ref, wmt_ref, be_ref, ga_ref,
             gb_ref, oh_ref, dp_ref, dcw_ref, dga_ref, dgb_ref, dws_ref, dbt_ref, carry_ref, s_ref, dgv_ref):
        i = pl.program_id(0)
        j = nb - 1 - i

        @pl.when(i == 0)
        def _():
            for r in (dcw_ref, dga_ref, dgb_ref, dws_ref, dbt_ref, carry_ref):
                r[...] = jnp.zeros_like(r)

        first = (j % per_seq) == 0
        last = (j % per_seq) == per_seq - 1
        b, c, hin = b_ref[...], c_ref[...], hin_ref[...]
        cw = cw_ref[...]
        hc = c * hin
        hc_prev = jnp.where(first, 0.0, ch_ref[...] * hh_ref[...])
        hc1 = _shift_rows(hc, hc_prev, 1)
        hc2 = _shift_rows(hc, hc_prev, 2)
        conv = cw[0:1, :] * hc2 + cw[1:2, :] * hc1 + cw[2:3, :] * hc
        ya = b * conv
        inv_a = _row_inv(ya)
        na = ya * inv_a
        do_a = dy_ref[:, 0:W]
        dga_ref[...] += jnp.sum(do_a * na, axis=0, keepdims=True)
        dna = do_a * ga_ref[...]
        dya = inv_a * (dna - na * jnp.mean(dna * na, axis=-1, keepdims=True))
        dp_ref[:, 0:W] = (dya * conv).astype(BF16)
        dconv = dya * b
        dcw_ref[0:1, :] += jnp.sum(dconv * hc2, axis=0, keepdims=True)
        dcw_ref[1:2, :] += jnp.sum(dconv * hc1, axis=0, keepdims=True)
        dcw_ref[2:3, :] += jnp.sum(dconv * hc, axis=0, keepdims=True)
        nxt = jnp.where(last, 0.0, carry_ref[...])
        dhc = cw[2:3, :] * dconv + cw[1:2, :] * _unshift_rows(dconv, nxt, 1) + cw[0:1, :] * _unshift_rows(dconv, nxt, 2)
        carry_ref[...] = dconv[0:SUBLANES, :]
        dp_ref[:, W:2 * W] = (dhc * hin).astype(BF16)
        dp_ref[:, 2 * W:3 * W] = (dhc * c).astype(BF16)
        u, v = u_ref[...], v_ref[...]
        gu, thu = _gelu(u)
        gv, thv = _gelu(v)
        gvb = gv.astype(BF16)
        for hd in range(H):
            sl = slice(hd * HEAD_DIM, (hd + 1) * HEAD_DIM)
            s_ref[:, sl] = jnp.dot(wm_ref[hd], gvb[:, sl], preferred_element_type=F32)
        s = s_ref[...] + be_ref[...]
        yb = gu * s
        inv_b = _row_inv(yb)
        nbv = yb * inv_b
        do_b = dy_ref[:, W:2 * W]
        dgb_ref[...] += jnp.sum(do_b * nbv, axis=0, keepdims=True)
        dnb = do_b * gb_ref[...]
        dyb = inv_b * (dnb - nbv * jnp.mean(dnb * nbv, axis=-1, keepdims=True))
        dp_ref[:, 3 * W:4 * W] = (dyb * s * _gelu_grad(u, thu)).astype(BF16)
        dsb = (dyb * gu).astype(BF16)
        dbt_ref[...] += jnp.dot(dsb, oh_ref[...], preferred_element_type=F32)
        for hd in range(H):
            sl = slice(hd * HEAD_DIM, (hd + 1) * HEAD_DIM)
            dws_ref[hd] += lax.dot_general(dsb[:, sl], gvb[:, sl], (((1,), (1,)), ((), ())), preferred_element_type=F32)
            dgv_ref[:, sl] = jnp.dot(wmt_ref[hd], dsb[:, sl], preferred_element_type=F32)
        dp_ref[:, 4 * W:5 * W] = (dgv_ref[...] * _gelu_grad(v, thv)).astype(BF16)

    full = lambda shape: pl.BlockSpec(shape, lambda i: (0,) * len(shape))
    return _call(
        body, name=name, grid=(nb,),
        args=[dy, proj, proj, proj, proj, proj, proj, proj, conv_w, wm, wmt, bias_e, g_a, g_b, head_onehot],
        in_specs=[pl.BlockSpec((CHUNK, 2 * W), lambda i: (blk(i), 0))] + cols + halos
        + [full((CONV_K, W)), full((H, CHUNK, CHUNK)), full((H, CHUNK, CHUNK)), full((CHUNK, W)), full((1, W)),
           full((1, W)), full((W, LANES))],
        out_specs=[pl.BlockSpec((CHUNK, 5 * W), lambda i: (blk(i), 0)), full((SUBLANES, W)), full((1, W)), full((1, W)),
                   full((H, CHUNK, CHUNK)), full((CHUNK, LANES))],
        out_shape=[jax.ShapeDtypeStruct((T, 5 * W), BF16), jax.ShapeDtypeStruct((SUBLANES, W), F32),
                   jax.ShapeDtypeStruct((1, W), F32), jax.ShapeDtypeStruct((1, W), F32),
                   jax.ShapeDtypeStruct((H, CHUNK, CHUNK), F32), jax.ShapeDtypeStruct((CHUNK, LANES), F32)],
        scratch_shapes=[pltpu.VMEM((SUBLANES, W), F32), pltpu.VMEM((CHUNK, W), F32), pltpu.VMEM((CHUNK, W), F32)],
        semantics=("arbitrary",), stages=stages)


def _cast_into_slot(w, chip, name):
    R, C = w.shape
    tr = _tile(R, 256, 16)

    def body(chip_ref, w_ref, o_ref):
        o_ref[...] = w_ref[...].astype(BF16)

    return pl.pallas_call(
        body, name=name,
        grid_spec=pltpu.PrefetchScalarGridSpec(
            num_scalar_prefetch=1, grid=(R // tr,),
            in_specs=[pl.BlockSpec((tr, C), lambda i, chip_ref: (i, 0))],
            out_specs=pl.BlockSpec((None, tr, C), lambda i, chip_ref: (chip_ref[0], i, 0))),
        out_shape=jax.ShapeDtypeStruct((N_CHIPS, R, C), BF16),
        compiler_params=_params(("parallel",)),
    )(chip, w)


def _rs_pair_add(dw, got, chip_core, name):
    S, R, C = dw.shape
    hr = R // 2
    tr = _tile(hr, 256, 16)
    nrb = hr // tr

    def body(cc_ref, dw_ref, got_ref, send_ref, own_ref):
        s = dw_ref[...].astype(F32) + got_ref[...].astype(F32)
        send_ref[...] = s.astype(BF16)

        @pl.when(pl.program_id(1) == cc_ref[0])
        def _():
            own_ref[...] = s

    return pl.pallas_call(
        body, name=name,
        grid_spec=pltpu.PrefetchScalarGridSpec(
            num_scalar_prefetch=1, grid=(nrb, S),
            in_specs=[pl.BlockSpec((None, tr, C), lambda i, q, cc: (q, cc[1] * nrb + i, 0)),
                      pl.BlockSpec((None, tr, C), lambda i, q, cc: (q, i, 0))],
            out_specs=[pl.BlockSpec((None, tr, C), lambda i, q, cc: (q, i, 0)),
                       pl.BlockSpec((tr, C), lambda i, q, cc: (i, 0))]),
        out_shape=[jax.ShapeDtypeStruct((S, hr, C), BF16), jax.ShapeDtypeStruct((hr, C), F32)],
        compiler_params=_params(("parallel", "arbitrary")),
    )(chip_core, dw, got)


def _rs_final_add(own, got, chip_core, name):
    hr, C = own.shape
    tr = _tile(hr, 256, 16)
    nrb = hr // tr

    def body(cc_ref, own_ref, got_ref, o_ref):
        o_ref[...] = ((own_ref[...] + got_ref[0].astype(F32)) + got_ref[1].astype(F32)) + got_ref[2].astype(F32)

    return pl.pallas_call(
        body, name=name,
        grid_spec=pltpu.PrefetchScalarGridSpec(
            num_scalar_prefetch=1, grid=(nrb,),
            in_specs=[pl.BlockSpec((tr, C), lambda i, cc: (i, 0)), pl.BlockSpec((3, tr, C), lambda i, cc: (0, i, 0))],
            out_specs=pl.BlockSpec((tr, C), lambda i, cc: (cc[1] * nrb + i, 0))),
        out_shape=jax.ShapeDtypeStruct((2 * hr, C), F32),
        compiler_params=_params(("parallel",)),
    )(chip_core, own, got)


def _adamw_math(w, g, m, v):
    m2 = ADAM_B1 * m + (1.0 - ADAM_B1) * g
    v2 = ADAM_B2 * v + (1.0 - ADAM_B2) * (g * g)
    delta = -ADAM_LR * ((m2 / ADAM_C1) / (jnp.sqrt(v2 / ADAM_C2) + ADAM_EPS) + ADAM_WD * w)
    return delta, m2, v2


def _adamw(w, g, m, v, name):
    R, C = w.shape
    parts = list(g) if isinstance(g, (list, tuple)) else [g]
    n_parts = len(parts)
    tr = _tile(R // n_parts, max(SUBLANES, (256 * 1024) // C), SUBLANES)
    per = R // n_parts // tr

    def body(w_ref, m_ref, v_ref, *refs):
        g_refs, (g2_ref, d_ref, m2_ref, v2_ref) = refs[:n_parts], refs[n_parts:]
        g = g_refs[0][...]
        for p in range(1, n_parts):
            g = jnp.where(pl.program_id(0) >= p * per, g_refs[p][...], g)
        g2_ref[...] = g
        d_ref[...], m2_ref[...], v2_ref[...] = _adamw_math(w_ref[...], g, m_ref[...], v_ref[...])

    blk = pl.BlockSpec((tr, C), lambda i: (i, 0))
    g_specs = [pl.BlockSpec((tr, C), functools.partial(lambda i, p: (jnp.clip(i - p * per, 0, per - 1), 0), p=p))
               for p in range(n_parts)]
    return _call(body, name=name, args=[w, m, v, *parts], grid=(R // tr,), in_specs=[blk] * 3 + g_specs,
                 out_specs=[blk] * 4, out_shape=[jax.ShapeDtypeStruct((R, C), F32)] * 4,
                 semantics=("arbitrary",))[0]


def _adamw_sc(w, g, m, v, name):
    R, C = w.shape
    blk = (SUBLANES, 512)
    assert R % blk[0] == 0 and C % blk[1] == 0
    mesh = plsc.VectorSubcoreMesh(core_axis_name="sc_core", subcore_axis_name="sc_tile")
    spec = pl.BlockSpec(block_shape=blk, index_map=lambda i, j: (i, j))

    def kern(w_hbm, g_hbm, m_hbm, v_hbm, g2_hbm, d_hbm, m2_hbm, v2_hbm):
        def body(w_v, g_v, m_v, v_v, g2_v, d_v, m2_v, v2_v):
            for r in range(blk[0]):
                @plsc.parallel_loop(0, blk[1], SC_LANES, unroll=8)
                def _(c):
                    at = (pl.ds(r, 1), pl.ds(c, SC_LANES))
                    gv = g_v.at[*at][...]
                    g2_v.at[*at][...] = gv
                    d_v.at[*at][...], m2_v.at[*at][...], v2_v.at[*at][...] = _adamw_math(
                        w_v.at[*at][...], gv, m_v.at[*at][...], v_v.at[*at][...])

        pltpu.emit_pipeline(
            body, grid=(R // blk[0], C // blk[1]), in_specs=[spec] * 4, out_specs=[spec] * 4,
            core_axis_name=("sc_core", "sc_tile"), dimension_semantics=(pltpu.PARALLEL, pltpu.PARALLEL),
        )(w_hbm, g_hbm, m_hbm, v_hbm, g2_hbm, d_hbm, m2_hbm, v2_hbm)

    return pl.kernel(kern, name=name, out_type=[jax.ShapeDtypeStruct((R, C), F32)] * 4, mesh=mesh,
                     scratch_types=[])(w, g, m, v)


def _all_gather_small(block, name):
    m_per, n = block.shape

    def body(x_ref, out_ref, send_sems, recv_sems, local_sem):
        x, y, c = _position()
        me, sibling = (x, y, c), (x, y, 1 - c)
        chips = _other_chips(x, y)

        def rows(px, py, pc):
            return out_ref.at[pl.ds((4 * px + 2 * py + pc) * m_per, m_per), :]

        def copy(k, blk, to, src=None):
            return pltpu.make_async_remote_copy(src_ref=rows(*blk) if src is None else src, dst_ref=rows(*blk),
                                                send_sem=send_sems.at[k], recv_sem=recv_sems.at[k], device_id=to,
                                                device_id_type=MESH)

        mine = pltpu.make_async_copy(x_ref, rows(*me), local_sem)
        mine.start()
        first = [copy(0, me, sibling, src=x_ref)]
        first += [copy(1 + j, me, (*chip, c), src=x_ref) for j, chip in enumerate(chips)]
        for cp in first:
            cp.start()
        passed = [copy(4 + j, (*chip, c), sibling) for j, chip in enumerate(chips)]
        for j, chip in enumerate(chips):
            copy(1 + j, (*chip, c), me).wait_recv()
            passed[j].start()
        copy(0, sibling, me).wait_recv()
        for j, chip in enumerate(chips):
            copy(4 + j, (*chip, 1 - c), me).wait_recv()
        for cp in first + passed:
            cp.wait_send()
        mine.wait()

    return pl.pallas_call(
        body, name=name,
        in_specs=[pl.BlockSpec(memory_space=pltpu.VMEM)],
        out_specs=pl.BlockSpec(memory_space=pltpu.VMEM),
        out_shape=jax.ShapeDtypeStruct((N_DEV * m_per, n), block.dtype),
        scratch_shapes=[pltpu.SemaphoreType.DMA((7,)), pltpu.SemaphoreType.DMA((7,)), pltpu.SemaphoreType.DMA],
        compiler_params=pltpu.CompilerParams(vmem_limit_bytes=VMEM_LIMIT_V7X, has_side_effects=True),
    )(block)


def _sum_and_adamw_small(gathered, w, m, v, name):
    rows, n = w.shape
    tr = _tile(rows, 32, SUBLANES)

    def body(p_ref, w_ref, m_ref, v_ref, g_ref, d_ref, m2_ref, v2_ref):
        g = p_ref[0]
        for d in range(1, N_DEV):
            g = g + p_ref[d]
        g_ref[...] = g
        d_ref[...], m2_ref[...], v2_ref[...] = _adamw_math(w_ref[...], g, m_ref[...], v_ref[...])

    blk = pl.BlockSpec((tr, n), lambda i: (i, 0))
    return pl.pallas_call(
        body, name=name, grid=(rows // tr,),
        in_specs=[pl.BlockSpec((N_DEV, tr, n), lambda i: (0, i, 0))] + [blk] * 3,
        out_specs=[blk] * 4,
        out_shape=[jax.ShapeDtypeStruct((rows, n), F32)] * 4,
        compiler_params=_params(("parallel",)),
    )(gathered.reshape(N_DEV, rows, n), w, m, v)


def _pad_rows(a):
    pad = (-a.shape[0]) % SUBLANES
    return jnp.pad(a, ((0, pad), (0, 0))) if pad else a


class _SmallPack:
    def __init__(self, W, D, H, chip):
        self.W, self.D, self.H, self.chip = W, D, H, chip
        self.offsets = {}
        self.rows = 0

    def pack(self, pieces):
        out = []
        self.offsets, self.rows = {}, 0
        for name, a in pieces:
            a = _pad_rows(a.astype(F32))
            self.offsets[name] = (self.rows, a.shape[0])
            self.rows += a.shape[0]
            out.append(a)
        return jnp.concatenate(out, axis=0)

    def piece(self, packed, name):
        start, n = self.offsets[name]
        return packed[start:start + n]


def _bias_rows(b, W):
    bt = jnp.pad(b.T, ((0, 0), (0, LANES - b.shape[0])))
    return bt.reshape(-1, W)


def _bias_from_rows(rows, H):
    return rows.reshape(-1)[:CHUNK * LANES].reshape(CHUNK, LANES)[:, :H].T


def kernel(x, mix_norm_g, w_in, conv_w, spatial_w, spatial_b, conv_out_norm_g, gmlp_out_norm_g, w_out, mlp_norm_g, w_up, w_down, final_norm_g, loss_target, m_mix_norm_g, m_w_in, m_conv_w, m_spatial_w, m_spatial_b, m_conv_out_norm_g, m_gmlp_out_norm_g, m_w_out, m_mlp_norm_g, m_w_up, m_w_down, m_final_norm_g, v_mix_norm_g, v_w_in, v_conv_w, v_spatial_w, v_spatial_b, v_conv_out_norm_g, v_gmlp_out_norm_g, v_w_out, v_mlp_norm_g, v_w_up, v_w_down, v_final_norm_g):
    Bl, S, D = x.shape
    T = Bl * S
    W = conv_out_norm_g.shape[-1]
    H = W // HEAD_DIM
    Wl = conv_w.shape[-1]
    xi, yi, ci = _position()
    chip = (2 * xi + yi).astype(jnp.int32)
    chip_arr = chip.reshape(1)
    chip_core = jnp.stack([chip, ci.astype(jnp.int32)])

    x2 = x.reshape(T, D)
    tgt2 = loss_target.reshape(T, D)

    s_in = _cast_into_slot(w_in[0], chip_arr, "cast_w_in")
    up_rows = w_up.shape[1] // 2
    up_cuts = [0] + [up_rows * pct // 100 // 16 * 16 for pct in (43, 57, 82)] + [up_rows]
    up_part = [(lo, hi - lo) for lo, hi in zip(up_cuts[:-1], up_cuts[1:])]

    causal = jnp.tril(jnp.ones((CHUNK, CHUNK), dtype=bool))
    wm = jnp.where(causal[None], spatial_w[0], 0.0).astype(BF16)
    wmt = jnp.swapaxes(wm, 1, 2)
    bias_e = jnp.repeat(spatial_b[0].T, HEAD_DIM, axis=1)
    conv_full = lax.dynamic_update_slice(jnp.zeros((CONV_K, W), F32), conv_w[0], (0, chip * Wl))
    head_onehot = (jnp.arange(W)[:, None] // HEAD_DIM == jnp.arange(LANES)[None, :]).astype(BF16)
    g_a, g_b = conv_out_norm_g, gmlp_out_norm_g

    xn, (s_out, s_up, s_down), ((g_in,), (conv_gathered,)) = _rmsnorm_fwd_and_casts(
        x2, mix_norm_g, [w_out[0], w_up[0], w_down[0]], chip_arr, "mix_norm_fwd",
        stages=[_GatherRows(s_in), _GatherSmall(_pad_rows(conv_full))])
    conv_w_all = conv_gathered.reshape(N_DEV, SUBLANES, W)[:, :CONV_K]
    conv_w_all = conv_w_all[0] + conv_w_all[2] + conv_w_all[4] + conv_w_all[6]
    proj, ((g_out,), (g_up,)) = _matmul(xn, g_in, mode="nn", name="proj_fwd", tm=1024, tn=512, tk=4096,
                                        out_dtypes=[F32], b_shard="n",
                                        stages=[_GatherRows(s_out), _GatherRows(s_up, *up_part[0])])
    g_out = g_out.reshape(-1, D)
    y, ((g_up,),) = _mixers_fwd(proj, conv_w_all, wm, bias_e, g_a, g_b, S, "mixers_fwd",
                                stages=[_GatherRows(g_up, *up_part[1])])
    h1, ((g_up,),) = _matmul(y, g_out, mode="nn", name="out_proj_fwd", tm=1024, tn=512, tk=4096, out_dtypes=[F32],
                             epilogue=_ep_residual, extras=(x2,), stages=[_GatherRows(g_up, *up_part[2])])
    xn2, ((g_up,),) = _rmsnorm_fwd(h1, mlp_norm_g, "mlp_norm_fwd", stages=[_GatherRows(g_up, *up_part[3])])
    (r, a), ((g_down,),) = _matmul(xn2, g_up, mode="nn", name="up_fwd", tm=1024, tn=1024, tk=4096, n_sub=2,
                                   out_dtypes=[BF16, BF16], epilogue=_ep_relu2, b_shard="n",
                                   stages=[_GatherRows(s_down)])
    g_down = g_down.reshape(-1, D)
    d2, _ = _matmul(a, g_down, mode="nn", name="down_fwd", tm=2048, tn=1024, tk=1024, out_dtypes=[F32])
    dh2, dh2b, d_final_g, loss_part = _loss_and_final_norm_bwd(h1, d2, tgt2, final_norm_g.reshape(1, D),
                                                               "loss_final_norm")

    def rs_adds(dw, got, tag):
        return _rs_pair_add(dw, got, chip_core, f"rs_pair_add_{tag}")

    dw_down, _ = _matmul(a, dh2b, mode="tn", name="down_dw", tm=1024, tn=1024, tk=4096, n_sub=2, out_dtypes=[BF16])
    dw_down = dw_down.reshape(N_CHIPS, -1, D)
    dpre, ((got_down,),) = _matmul(dh2b, g_down, mode="nt", name="down_dx", tm=1024, tn=1024, tk=4096, n_sub=2,
                                   out_dtypes=[BF16], epilogue=_ep_relu2_bwd, extras=(r,),
                                   stages=[_PairExchange(dw_down)])
    part_down, own_down = rs_adds(dw_down, got_down, "w_down")
    rows_down = part_down.shape[1]
    down_a = rows_down * 3 // 4 // 16 * 16
    dw_up, ((landed_down,),) = _matmul(xn2, dpre, mode="tn", name="up_dw", tm=1024, tn=1024, tk=4096, n_sub=2,
                                       out_dtypes=[BF16], out_shard=True,
                                       stages=[_ChipExchange(part_down, None, 0, down_a)])
    dxn2, ((got_up,), (landed_down,)) = _matmul(
        dpre, g_up, mode="nt", name="up_dx", tm=2048, tn=1024, tk=1024, out_dtypes=[F32], b_shard="k",
        stages=[_PairExchange(dw_up), _ChipExchange(part_down, landed_down, down_a, rows_down - down_a)])
    half_down = _rs_final_add(own_down, landed_down, chip_core, "rs_final_add_w_down")
    part_up, own_up = rs_adds(dw_up, got_up, "w_up")
    rows_up = part_up.shape[1]
    up_cut = [0] + [rows_up * pct // 100 // 16 * 16 for pct in (12, 32, 57, 83)] + [rows_up]
    up_rs = [(lo, hi - lo) for lo, hi in zip(up_cut[:-1], up_cut[1:])]
    (dh1, dh1b, d_mlp_g), ((grad_down,), (landed_up,)) = _rmsnorm_bwd(
        dxn2, h1, mlp_norm_g, dh2, "mlp_norm_bwd",
        stages=[_HalfExchange(half_down), _ChipExchange(part_up, None, *up_rs[0])])
    dy, ((landed_up,),) = _matmul(dh1b, g_out, mode="nt", name="out_proj_dx", tm=1024, tn=512, tk=4096,
                                  out_dtypes=[F32], stages=[_ChipExchange(part_up, landed_up, *up_rs[1])])
    (dproj, d_conv, d_ga, d_gb, d_ws, d_bt), ((landed_up,),) = _mixers_bwd(
        dy, proj, conv_w_all, wm, wmt, bias_e, g_a, g_b, head_onehot, S, "mixers_bwd",
        stages=[_ChipExchange(part_up, landed_up, *up_rs[2])])
    pack = _SmallPack(W, D, H, chip)
    loss_row = jnp.pad(loss_part[:, :1], ((0, 0), (0, W - 1)))
    g_part = pack.pack([("spatial_w", (d_ws * causal.astype(F32)[None]).reshape(-1, W)), ("conv_w", d_conv),
                        ("mlp_norm_g", d_mlp_g.reshape(-1, W)), ("final_norm_g", d_final_g.reshape(-1, W)),
                        ("conv_out_norm_g", d_ga), ("gmlp_out_norm_g", d_gb), ("spatial_b", d_bt.reshape(-1, W)),
                        ("loss", loss_row)])
    tm_in = min(1024, D // 2)
    nb_in = D // tm_in // 2
    dw_in_a, ((landed_up,),) = _matmul(
        xn, dproj, mode="tn", name="proj_dw_a", tm=tm_in, tn=512, tk=4096, out_dtypes=[BF16], out_shard=True,
        m_blocks=(0, nb_in), stages=[_ChipExchange(part_up, landed_up, *up_rs[3])])
    dw_in_b, ((landed_up,), (got_in_a,), (g_all,)) = _matmul(
        xn, dproj, mode="tn", name="proj_dw_b", tm=tm_in, tn=512, tk=4096, out_dtypes=[BF16], out_shard=True,
        m_blocks=(nb_in, nb_in),
        stages=[_ChipExchange(part_up, landed_up, *up_rs[4]), _PairExchange(dw_in_a), _GatherSmall(g_part)])
    half_up = _rs_final_add(own_up, landed_up, chip_core, "rs_final_add_w_up")
    part_in_a, own_in_a = rs_adds(dw_in_a, got_in_a, "w_in_a")
    rows_in = part_in_a.shape[1]
    in_a = rows_in * 7 // 10 // 16 * 16
    dw_out, ((landed_in_a,), (got_in_b,), (grad_up,)) = _matmul(
        y, dh1b, mode="tn", name="out_proj_dw", tm=1024, tn=1024, tk=4096, n_sub=2, out_dtypes=[BF16],
        stages=[_ChipExchange(part_in_a, None, 0, in_a), _PairExchange(dw_in_b), _HalfExchange(half_up)])
    dw_out = dw_out.reshape(N_CHIPS, -1, D)
    part_in_b, own_in_b = rs_adds(dw_in_b, got_in_b, "w_in_b")
    dxn, ((landed_in_a,), (landed_in_b,), (got_out,)) = _matmul(
        dproj, g_in, mode="nt", name="proj_dx", tm=2048, tn=1024, tk=1280, out_dtypes=[F32], b_shard="k",
        stages=[_ChipExchange(part_in_a, landed_in_a, in_a, rows_in - in_a), _ChipExchange(part_in_b),
                _PairExchange(dw_out)])
    part_out, own_out = rs_adds(dw_out, got_out, "w_out")
    half_in_a = _rs_final_add(own_in_a, landed_in_a, chip_core, "rs_final_add_w_in_a")
    half_in_b = _rs_final_add(own_in_b, landed_in_b, chip_core, "rs_final_add_w_in_b")
    (grad_x, _unused, d_mix_g), ((landed_out,), (grad_in_a,), (grad_in_b,)) = _rmsnorm_bwd(
        dxn, x2, mix_norm_g, dh1, "mix_norm_bwd",
        stages=[_ChipExchange(part_out), _HalfExchange(half_in_a), _HalfExchange(half_in_b)])
    grad_in = [grad_in_a, grad_in_b]
    half_out = _rs_final_add(own_out, landed_out, chip_core, "rs_final_add_w_out")
    ((grad_out,),) = _run_stages([_HalfExchange(half_out)], "rs_half_exchange_w_out")
    big = {"w_down": _adamw_sc(w_down[0], grad_down, m_w_down[0], v_w_down[0], "adamw_sc_w_down"),
           "w_up": _adamw_sc(w_up[0], grad_up, m_w_up[0], v_w_up[0], "adamw_sc_w_up"),
           "w_in": _adamw(w_in[0], grad_in, m_w_in[0], v_w_in[0], "adamw_w_in"),
           "w_out": _adamw(w_out[0], grad_out, m_w_out[0], v_w_out[0], "adamw_w_out")}
    big = {k: [t[None] for t in v] for k, v in big.items()}

    def small(conv, sw, sb, ga, gb, mlp, fin):
        return pack.pack([("spatial_w", sw.reshape(-1, W)), ("conv_w", conv), ("mlp_norm_g", mlp.reshape(-1, W)),
                          ("final_norm_g", fin.reshape(-1, W)), ("conv_out_norm_g", ga.reshape(-1, W)),
                          ("gmlp_out_norm_g", gb.reshape(-1, W)), ("spatial_b", _bias_rows(sb, W)),
                          ("loss", jnp.zeros((1, W), F32))])

    def full_conv(cw):
        return lax.dynamic_update_slice(jnp.zeros((CONV_K, W), F32), cw[0], (0, chip * Wl))

    def mix_rows(a):
        return _pad_rows(a.reshape(-1, W))

    w_s = small(full_conv(conv_w), spatial_w, spatial_b[0], conv_out_norm_g, gmlp_out_norm_g, mlp_norm_g, final_norm_g)
    m_s = small(full_conv(m_conv_w), m_spatial_w, m_spatial_b[0], m_conv_out_norm_g, m_gmlp_out_norm_g, m_mlp_norm_g,
                m_final_norm_g)
    v_s = small(full_conv(v_conv_w), v_spatial_w, v_spatial_b[0], v_conv_out_norm_g, v_gmlp_out_norm_g, v_mlp_norm_g,
                v_final_norm_g)
    small_outs = _sum_and_adamw_small(g_all, w_s, m_s, v_s, "sum_adamw_small")
    mix_all = _all_gather_small(mix_rows(d_mix_g), "all_gather_mix_norm_grad")
    mix_outs = _sum_and_adamw_small(mix_all, mix_rows(mix_norm_g), mix_rows(m_mix_norm_g), mix_rows(v_mix_norm_g),
                                    "sum_adamw_mix_norm")

    def unpack(kind, name):
        if name == "mix_norm_g":
            return mix_outs[kind].reshape(-1)[:D].reshape(1, D)
        rows = pack.piece(small_outs[kind], name)
        if name == "spatial_w":
            return rows.reshape(1, H, CHUNK, CHUNK)
        if name == "conv_w":
            return lax.dynamic_slice(rows[:CONV_K], (0, chip * Wl), (CONV_K, Wl))[None]
        if name == "spatial_b":
            return _bias_from_rows(rows, H)[None]
        if name == "final_norm_g":
            return rows.reshape(-1)[:D]
        n = D if name == "mlp_norm_g" else W
        return rows.reshape(-1)[:n].reshape(1, n)

    loss = pack.piece(small_outs[0], "loss")[0, 0]
    order = ["mix_norm_g", "w_in", "conv_w", "spatial_w", "spatial_b", "conv_out_norm_g", "gmlp_out_norm_g", "w_out",
             "mlp_norm_g", "w_up", "w_down", "final_norm_g"]
    outs = [loss, grad_x.reshape(Bl, S, D)]
    for kind in range(4):
        for name in order:
            outs.append(big[name][kind] if name in big else unpack(kind, name))
    return tuple(outs)
```

```python
import functools
import math

import jax
import jax.numpy as jnp
from jax import lax
from jax.experimental import pallas as pl
from jax.experimental.pallas import tpu as pltpu
from jax.experimental.pallas import tpu_sc as plsc

F32 = jnp.float32
BF16 = jnp.bfloat16
MESH = pl.DeviceIdType.MESH

NORM_EPS = 1e-5
HEAD_DIM = 128
CHUNK = 128
CONV_K = 3
N_CHIPS = 4
N_DEV = 8

ADAM_LR = 0.001
ADAM_B1 = 0.9
ADAM_B2 = 0.999
ADAM_EPS = 1e-08
ADAM_WD = 0.01
ADAM_STEP = 10
ADAM_C1 = 1.0 - ADAM_B1 ** ADAM_STEP
ADAM_C2 = 1.0 - ADAM_B2 ** ADAM_STEP

GELU_K = math.sqrt(2.0 / math.pi)
GELU_A = 0.044715

VMEM_LIMIT_V7X = 56 * 1024 * 1024
SUBLANES = 8
LANES = 128
SC_LANES = 16


def _tile(dim, target, mult=LANES):
    if dim <= target:
        return dim
    t = (target // mult) * mult
    while t > mult and dim % t:
        t -= mult
    assert dim % t == 0, (dim, target, mult)
    return t


def _params(sem=None):
    return pltpu.CompilerParams(dimension_semantics=sem, vmem_limit_bytes=VMEM_LIMIT_V7X)


class _Stage:
    bufs = ()
    n_sems = 0
    MIDDLE_AT = 0.6
    base = 0

    def start(self, refs, send, recv):
        raise NotImplementedError

    def middle(self, refs, send, recv):
        pass

    def finish(self, refs, send, recv):
        raise NotImplementedError


def _position():
    return lax.axis_index("x"), lax.axis_index("y"), lax.axis_index("c")


def _other_chips(x, y):
    return [(1 - x, y), (x, 1 - y), (1 - x, 1 - y)]


def _remote(src, dst, send, recv, k, to):
    return pltpu.make_async_remote_copy(src_ref=src, dst_ref=dst, send_sem=send.at[k], recv_sem=recv.at[k],
                                        device_id=to, device_id_type=MESH)


def _call(body, *, name, args, in_specs, out_specs, out_shape, grid=(), scratch_shapes=(), semantics=None, stages=(),
          prefetch=None):
    n_in, n_out, n_scratch = len(args), len(out_shape), len(scratch_shapes)
    n_pre = 0 if prefetch is None else 1
    any_spec = pl.BlockSpec(memory_space=pl.ANY)
    extra_args, extra_out, aliases, layout = [], [], {}, []
    for st in stages:
        where = []
        for kind, buf in st.bufs:
            if kind in ("in", "alias"):
                extra_args.append(buf)
                pos_in = n_in + len(extra_args) - 1
            if kind in ("out", "alias"):
                extra_out.append(jax.ShapeDtypeStruct(buf.shape, buf.dtype))
                pos_out = n_out + len(extra_out) - 1
            if kind == "alias":
                aliases[n_pre + pos_in] = pos_out
            where.append(("in", pos_in) if kind == "in" else ("out", pos_out))
        layout.append(where)
    n_sems = sum(st.n_sems for st in stages)
    n_xin, n_xout = len(extra_args), len(extra_out)

    def wrapped(*refs):
        pre, refs = refs[:n_pre], refs[n_pre:]
        ins = refs[:n_in + n_xin]
        outs = refs[n_in + n_xin:n_in + n_xin + n_out + n_xout]
        scratch = refs[n_in + n_xin + n_out + n_xout:]
        main = pre + ins[:n_in] + outs[:n_out] + scratch[:n_scratch]
        if not stages:
            body(*main)
            return
        send, recv = scratch[n_scratch], scratch[n_scratch + 1]
        step, n_steps = 0, 1
        for d, g in enumerate(grid):
            step = step * g + pl.program_id(d)
            n_steps *= g
        base, views = 0, []
        for st, where in zip(stages, layout):
            st_refs = [ins[p] if side == "in" else outs[p] for side, p in where]
            st.base = base
            views.append((st, st_refs, send, recv))
            base += st.n_sems

        def starts():
            for st, r, s, v in views:
                st.start(r, s, v)

        def middles():
            for st, r, s, v in views:
                st.middle(r, s, v)

        def finishes():
            for st, r, s, v in views:
                st.finish(r, s, v)

        if not grid:
            starts()
            body(*main)
            middles()
            finishes()
        else:
            pl.when(step == 0)(starts)
            body(*main)
            pl.when(step == min(int(n_steps * _Stage.MIDDLE_AT), n_steps - 1))(middles)
            pl.when(step == n_steps - 1)(finishes)

    sems = [pltpu.SemaphoreType.DMA((n_sems,)), pltpu.SemaphoreType.DMA((n_sems,))] if stages else []
    specs = dict(in_specs=list(in_specs) + [any_spec] * n_xin, out_specs=list(out_specs) + [any_spec] * n_xout,
                 scratch_shapes=list(scratch_shapes) + sems)
    if prefetch is None:
        kw = dict(specs, **(dict(grid=grid) if grid else {}))
    else:
        kw = dict(grid_spec=pltpu.PrefetchScalarGridSpec(num_scalar_prefetch=1, grid=grid, **specs))
    res = pl.pallas_call(
        wrapped, name=name,
        out_shape=list(out_shape) + extra_out,
        input_output_aliases=aliases,
        compiler_params=pltpu.CompilerParams(
            dimension_semantics=("arbitrary",) * len(grid) if stages and grid else semantics,
            vmem_limit_bytes=VMEM_LIMIT_V7X, has_side_effects=bool(stages)),
        **kw,
    )(*([] if prefetch is None else [prefetch]), *args, *extra_args)
    main_res, stage_res, pos = list(res[:n_out]), [], n_out
    for st in stages:
        k = sum(kind in ("out", "alias") for kind, _ in st.bufs)
        stage_res.append(list(res[pos:pos + k]))
        pos += k
    return main_res, stage_res


def _run_stages(stages, name):
    return _call(lambda: None, name=name, args=[], in_specs=[], out_specs=[], out_shape=[], stages=stages)[1]


class _GatherRows(_Stage):
    n_sems = 8

    def __init__(self, g, lo=0, n=None):
        self.hr = g.shape[1] // 2
        self.lo, self.n = lo, (self.hr if n is None else n)
        self.n0 = self.n // 2 // 16 * 16
        self.bufs = [("alias", g)]

    def _copy(self, g_ref, send, recv, k, chip_xy, half, to, lo=0, n=None):
        n = self.n if n is None else n
        blk = g_ref.at[2 * chip_xy[0] + chip_xy[1], pl.ds(half * self.hr + self.lo + lo, n), :]
        return _remote(blk, blk, send, recv, self.base + k, to)

    def _plan(self, g, send, recv):
        x, y, c = _position()
        me, sib = (x, y, c), (x, y, 1 - c)
        cx, cy, cd = _other_chips(x, y)
        n0, n1 = self.n0, self.n - self.n0
        mine = [((x, y), c, (*cx, c)), ((x, y), c, (*cy, c)), (cx, c, sib), (cy, c, sib),
                (cx, c, (*cy, c), 0, n0), (cy, c, (*cx, c), n0, n1), (cd, c, sib, 0, n0), (cd, c, sib, n0, n1)]
        theirs = [(cx, c, me), (cy, c, me), (cx, 1 - c, me), (cy, 1 - c, me),
                  (cd, c, me, 0, n0), (cd, c, me, n0, n1), (cd, 1 - c, me, 0, n0), (cd, 1 - c, me, n0, n1)]
        return (lambda k: self._copy(g, send, recv, k, *mine[k])), (lambda k: self._copy(g, send, recv, k, *theirs[k]))

    def start(self, refs, send, recv):
        mine, _ = self._plan(refs[0], send, recv)
        mine(0).start()
        mine(1).start()

    def middle(self, refs, send, recv):
        mine, theirs = self._plan(refs[0], send, recv)
        theirs(0).wait_recv()
        mine(4).start()
        mine(2).start()
        theirs(1).wait_recv()
        mine(5).start()
        mine(3).start()

    def finish(self, refs, send, recv):
        mine, theirs = self._plan(refs[0], send, recv)
        theirs(4).wait_recv()
        mine(6).start()
        theirs(5).wait_recv()
        mine(7).start()
        for k in (2, 3, 6, 7):
            theirs(k).wait_recv()
        for k in range(self.n_sems):
            mine(k).wait_send()


class _GatherSmall(_Stage):
    n_sems = 8

    def __init__(self, block):
        self.m = block.shape[0]
        self.bufs = [("in", block), ("out", jax.ShapeDtypeStruct((N_DEV * self.m, block.shape[1]), block.dtype))]

    def _rows(self, out, px, py, pc):
        return out.at[pl.ds((4 * px + 2 * py + pc) * self.m, self.m), :]

    def _copy(self, refs, send, recv, k, blk, to, own=False):
        dst = self._rows(refs[1], *blk)
        return _remote(refs[0] if own else dst, dst, send, recv, self.base + k, to)

    def _local(self, refs, send):
        return pltpu.make_async_copy(refs[0], self._rows(refs[1], *_position()), send.at[self.base + 7])

    def start(self, refs, send, recv):
        x, y, c = _position()
        self._local(refs, send).start()
        self._copy(refs, send, recv, 0, (x, y, c), (x, y, 1 - c), own=True).start()
        for j, chip in enumerate(_other_chips(x, y)):
            self._copy(refs, send, recv, 1 + j, (x, y, c), (*chip, c), own=True).start()

    def finish(self, refs, send, recv):
        x, y, c = _position()
        me, sib, chips = (x, y, c), (x, y, 1 - c), _other_chips(x, y)
        for j, chip in enumerate(chips):
            self._copy(refs, send, recv, 1 + j, (*chip, c), me).wait_recv()
            self._copy(refs, send, recv, 4 + j, (*chip, c), sib).start()
        self._copy(refs, send, recv, 0, sib, me).wait_recv()
        for j, chip in enumerate(chips):
            self._copy(refs, send, recv, 4 + j, (*chip, 1 - c), me).wait_recv()
        self._copy(refs, send, recv, 0, me, sib, own=True).wait_send()
        for j, chip in enumerate(chips):
            self._copy(refs, send, recv, 1 + j, me, (*chip, c), own=True).wait_send()
            self._copy(refs, send, recv, 4 + j, (*chip, c), sib).wait_send()
        self._local(refs, send).wait()


class _PairExchange(_Stage):
    n_sems = 1

    def __init__(self, dw):
        S, R, C = dw.shape
        self.hr = R // 2
        self.bufs = [("in", dw), ("out", jax.ShapeDtypeStruct((S, self.hr, C), dw.dtype))]

    def _copy(self, refs, send, recv):
        x, y, c = _position()
        return _remote(refs[0].at[:, pl.ds((1 - c) * self.hr, self.hr), :], refs[1], send, recv, self.base,
                       (x, y, 1 - c))

    def start(self, refs, send, recv):
        self._copy(refs, send, recv).start()

    def finish(self, refs, send, recv):
        cp = self._copy(refs, send, recv)
        cp.wait_recv()
        cp.wait_send()


class _ChipExchange(_Stage):
    n_sems = 3

    def __init__(self, part, landed=None, lo=0, n=None):
        S, hr, C = part.shape
        self.lo, self.n = lo, (hr if n is None else n)
        self.bufs = [("in", part), ("out", jax.ShapeDtypeStruct((3, hr, C), part.dtype)) if landed is None
                     else ("alias", landed)]

    def _copies(self, refs, send, recv):
        x, y, c = _position()
        rows = pl.ds(self.lo, self.n)
        return [_remote(refs[0].at[2 * chip[0] + chip[1], rows, :], refs[1].at[j, rows, :], send, recv,
                        self.base + j, (*chip, c))
                for j, chip in enumerate(_other_chips(x, y))]

    def start(self, refs, send, recv):
        for cp in self._copies(refs, send, recv):
            cp.start()

    def finish(self, refs, send, recv):
        copies = self._copies(refs, send, recv)
        for cp in copies:
            cp.wait_recv()
        for cp in copies:
            cp.wait_send()


class _HalfExchange(_Stage):
    n_sems = 1

    def __init__(self, grad):
        self.hr = grad.shape[0] // 2
        self.bufs = [("alias", grad)]

    def start(self, refs, send, recv):
        x, y, c = _position()
        mine = refs[0].at[pl.ds(c * self.hr, self.hr), :]
        _remote(mine, mine, send, recv, self.base, (x, y, 1 - c)).start()

    def finish(self, refs, send, recv):
        x, y, c = _position()
        mine = refs[0].at[pl.ds(c * self.hr, self.hr), :]
        theirs = refs[0].at[pl.ds((1 - c) * self.hr, self.hr), :]
        _remote(theirs, theirs, send, recv, self.base, (x, y, 1 - c)).wait_recv()
        _remote(mine, mine, send, recv, self.base, (x, y, 1 - c)).wait_send()


def _matmul(a, b, *, mode, name, tm, tn, tk, out_dtypes, epilogue=None, extras=(), b_shard=None, out_shard=False,
            stages=(), n_sub=1, m_blocks=None):
    if mode == "tn":
        K, M = a.shape
    else:
        M, K = a.shape
    if b_shard == "n":
        S, Kb, Ns = b.shape
        N = S * Ns
    elif b_shard == "k":
        S, N, Ks = b.shape
        Kb = S * Ks
    elif mode == "nt":
        N, Kb = b.shape
    else:
        Kb, N = b.shape
    assert Kb == K, (name, a.shape, b.shape)
    tm, tn, tk = _tile(M, tm), _tile(N, tn), _tile(K, tk)
    if b_shard == "n" or out_shard:
        n_per = N // N_CHIPS
        tn = _tile(n_per, tn)
        njs = n_per // tn
    if b_shard == "k":
        tk = _tile(K // N_CHIPS, tk)
        nks = (K // N_CHIPS) // tk
    gm, gn, gk = M // tm, N // tn, K // tk
    if gk > 1 or tn % (n_sub * LANES):
        n_sub = 1
    i0 = 0
    if m_blocks is not None:
        i0, gm = m_blocks
        M = gm * tm

    if mode == "tn":
        a_spec = pl.BlockSpec((tk, tm), lambda i, j, k: (k, i + i0))
        dims = (((0,), (0,)), ((), ()))
    else:
        a_spec = pl.BlockSpec((tm, tk), lambda i, j, k: (i + i0, k))
        dims = (((1,), (1,)), ((), ())) if mode == "nt" else (((1,), (0,)), ((), ()))
    if b_shard == "n":
        b_spec = pl.BlockSpec((None, tk, tn), lambda i, j, k: (j // njs, k, j % njs))
    elif b_shard == "k":
        b_spec = pl.BlockSpec((None, tn, tk), lambda i, j, k: (k // nks, j, k % nks))
    elif mode == "nt":
        b_spec = pl.BlockSpec((tn, tk), lambda i, j, k: (j, k))
    else:
        b_spec = pl.BlockSpec((tk, tn), lambda i, j, k: (k, j))
    mn_spec = pl.BlockSpec((tm, tn), lambda i, j, k: (i, j))
    if out_shard:
        out_spec = pl.BlockSpec((None, tm, tn), lambda i, j, k: (j // njs, i, j % njs))
        out_shape = [jax.ShapeDtypeStruct((N_CHIPS, M, N // N_CHIPS), dt) for dt in out_dtypes]
    else:
        out_spec = mn_spec
        out_shape = [jax.ShapeDtypeStruct((M, N), dt) for dt in out_dtypes]
    n_extra, n_out = len(extras), len(out_dtypes)

    def finish_tile(acc, extra_refs, out_refs):
        if epilogue is None:
            for o in out_refs:
                o[...] = acc.astype(o.dtype)
        else:
            epilogue(acc, extra_refs, out_refs)

    def body(*refs):
        a_ref, b_ref = refs[0], refs[1]
        extra_refs = refs[2:2 + n_extra]
        out_refs = refs[2 + n_extra:2 + n_extra + n_out]

        def product():
            return lax.dot_general(a_ref[...], b_ref[...], dims, preferred_element_type=F32)

        if gk == 1:
            sub = tn // n_sub
            for h in range(n_sub):
                cols = slice(h * sub, (h + 1) * sub)
                b_part = b_ref[cols, :] if mode == "nt" else b_ref[:, cols]
                acc = lax.dot_general(a_ref[...], b_part, dims, preferred_element_type=F32)
                finish_tile(acc, [e.at[:, cols] for e in extra_refs], [o.at[:, cols] for o in out_refs])
            return
        acc_ref = refs[-1]
        k = pl.program_id(2)

        @pl.when(k == 0)
        def _():
            acc_ref[...] = product()

        @pl.when((k > 0) & (k < gk - 1))
        def _():
            acc_ref[...] += product()

        @pl.when(k == gk - 1)
        def _():
            finish_tile(acc_ref[...] + product(), extra_refs, out_refs)

    outs, carried = _call(
        body, name=name, args=[a, b, *extras], grid=(gm, gn, gk),
        in_specs=[a_spec, b_spec] + [mn_spec] * n_extra, out_specs=[out_spec] * n_out, out_shape=out_shape,
        scratch_shapes=[pltpu.VMEM((tm, tn), F32)] if gk > 1 else [],
        semantics=("parallel", "parallel", "arbitrary"), stages=stages)
    return (outs[0] if n_out == 1 else outs), carried


def _ep_residual(acc, extra_refs, out_refs):
    out_refs[0][...] = extra_refs[0][...] + acc


def _ep_relu2(acc, extra_refs, out_refs):
    r = jnp.maximum(acc, 0.0)
    out_refs[0][...] = r.astype(BF16)
    out_refs[1][...] = (r * r).astype(BF16)


def _ep_relu2_bwd(acc, extra_refs, out_refs):
    out_refs[0][...] = (acc * (2.0 * extra_refs[0][...].astype(F32))).astype(BF16)


def _row_inv(x):
    return lax.rsqrt(jnp.mean(x * x, axis=-1, keepdims=True) + NORM_EPS)


def _rmsnorm_fwd(x, g, name, stages=()):
    T, D = x.shape
    tt = _tile(T, 256, SUBLANES)

    def body(x_ref, g_ref, o_ref):
        xv = x_ref[...]
        o_ref[...] = (xv * _row_inv(xv) * g_ref[...]).astype(BF16)

    outs, carried = _call(
        body, name=name, args=[x, g], grid=(T // tt,),
        in_specs=[pl.BlockSpec((tt, D), lambda i: (i, 0)), pl.BlockSpec((1, D), lambda i: (0, 0))],
        out_specs=[pl.BlockSpec((tt, D), lambda i: (i, 0))], out_shape=[jax.ShapeDtypeStruct((T, D), BF16)],
        semantics=("parallel",), stages=stages)
    return outs[0], carried


def _rmsnorm_fwd_and_casts(x, g, weights, chip, name, stages=()):
    T, D = x.shape
    tt = _tile(T, 256, SUBLANES)
    n, nw = T // tt, len(weights)
    rows = [w.shape[0] // n for w in weights]
    assert all(r % 16 == 0 and r * n == w.shape[0] for r, w in zip(rows, weights))

    def body(chip_ref, x_ref, g_ref, *refs):
        w_refs, o_ref, slot_refs = refs[:nw], refs[nw], refs[nw + 1:]
        xv = x_ref[...]
        o_ref[...] = (xv * _row_inv(xv) * g_ref[...]).astype(BF16)
        for w_ref, s_ref in zip(w_refs, slot_refs):
            s_ref[...] = w_ref[...].astype(BF16)

    outs, carried = _call(
        body, name=name, args=[x, g, *weights], grid=(n,), prefetch=chip,
        in_specs=[pl.BlockSpec((tt, D), lambda i, chip_ref: (i, 0)), pl.BlockSpec((1, D), lambda i, chip_ref: (0, 0))]
        + [pl.BlockSpec((r, w.shape[1]), lambda i, chip_ref: (i, 0)) for r, w in zip(rows, weights)],
        out_specs=[pl.BlockSpec((tt, D), lambda i, chip_ref: (i, 0))]
        + [pl.BlockSpec((None, r, w.shape[1]), lambda i, chip_ref: (chip_ref[0], i, 0)) for r, w in zip(rows, weights)],
        out_shape=[jax.ShapeDtypeStruct((T, D), BF16)]
        + [jax.ShapeDtypeStruct((N_CHIPS, *w.shape), BF16) for w in weights],
        semantics=("parallel",), stages=stages)
    return outs[0], outs[1:], carried


def _rmsnorm_bwd(dxn, h, g, dres, name, stages=()):
    T, D = h.shape
    tt = _tile(T, 128, SUBLANES)

    def body(dxn_ref, h_ref, g_ref, dres_ref, dh_ref, dhb_ref, dg_ref):
        @pl.when(pl.program_id(0) == 0)
        def _():
            dg_ref[...] = jnp.zeros_like(dg_ref)

        hv = h_ref[...]
        inv = _row_inv(hv)
        n = hv * inv
        d = dxn_ref[...]
        dg_ref[...] += jnp.sum(d * n, axis=0, keepdims=True)
        dn = d * g_ref[...]
        dh = dres_ref[...] + inv * (dn - n * jnp.mean(dn * n, axis=-1, keepdims=True))
        dh_ref[...] = dh
        dhb_ref[...] = dh.astype(BF16)

    row = pl.BlockSpec((tt, D), lambda i: (i, 0))
    vec = pl.BlockSpec((1, D), lambda i: (0, 0))
    return _call(
        body, name=name, args=[dxn, h, g, dres], grid=(T // tt,), in_specs=[row, row, vec, row],
        out_specs=[row, row, vec],
        out_shape=[jax.ShapeDtypeStruct((T, D), F32), jax.ShapeDtypeStruct((T, D), BF16),
                   jax.ShapeDtypeStruct((1, D), F32)],
        semantics=("arbitrary",), stages=stages)


def _loss_and_final_norm_bwd(h1, d2, tgt, g, name):
    T, D = h1.shape
    tt = _tile(T, 128, SUBLANES)

    def body(h1_ref, d2_ref, t_ref, g_ref, dh_ref, dhb_ref, dg_ref, loss_ref):
        @pl.when(pl.program_id(0) == 0)
        def _():
            dg_ref[...] = jnp.zeros_like(dg_ref)
            loss_ref[...] = jnp.zeros_like(loss_ref)

        hv = h1_ref[...] + d2_ref[...]
        gv = g_ref[...]
        inv = _row_inv(hv)
        n = hv * inv
        err = n * gv - t_ref[...]
        loss_ref[...] += 0.5 * jnp.sum(jnp.mean(err * err, axis=-1, keepdims=True))
        dy = err * (1.0 / D)
        dg_ref[...] += jnp.sum(dy * n, axis=0, keepdims=True)
        dn = dy * gv
        dh = inv * (dn - n * jnp.mean(dn * n, axis=-1, keepdims=True))
        dh_ref[...] = dh
        dhb_ref[...] = dh.astype(BF16)

    row = pl.BlockSpec((tt, D), lambda i: (i, 0))
    vec = pl.BlockSpec((1, D), lambda i: (0, 0))
    one = pl.BlockSpec((1, LANES), lambda i: (0, 0))
    return _call(
        body, name=name, args=[h1, d2, tgt, g], grid=(T // tt,), in_specs=[row, row, row, vec],
        out_specs=[row, row, vec, one],
        out_shape=[jax.ShapeDtypeStruct((T, D), F32), jax.ShapeDtypeStruct((T, D), BF16),
                   jax.ShapeDtypeStruct((1, D), F32), jax.ShapeDtypeStruct((1, LANES), F32)],
        semantics=("arbitrary",))[0]


def _gelu(x):
    th = jnp.tanh(GELU_K * (x + GELU_A * (x * x * x)))
    return 0.5 * x * (1.0 + th), th


def _gelu_grad(x, th):
    return 0.5 * (1.0 + th) + 0.5 * x * (1.0 - th * th) * (GELU_K * (1.0 + 3.0 * GELU_A * (x * x)))


def _shift_rows(cur, prev_rows, k):
    rolled = pltpu.roll(cur, k, 0)
    row = lax.broadcasted_iota(jnp.int32, cur.shape, 0)
    out = rolled
    for r in range(k):
        out = jnp.where(row == r, prev_rows[SUBLANES - k + r:SUBLANES - k + r + 1, :], out)
    return out


def _unshift_rows(cur, next_rows, k):
    n = cur.shape[0]
    rolled = pltpu.roll(cur, n - k, 0)
    row = lax.broadcasted_iota(jnp.int32, cur.shape, 0)
    out = rolled
    for r in range(k):
        out = jnp.where(row == n - k + r, next_rows[r:r + 1, :], out)
    return out


def _mixer_specs(W, blk, halo):
    cols = [pl.BlockSpec((CHUNK, W), functools.partial(lambda i, col: (blk(i), col), col=col)) for col in range(5)]
    halos = [pl.BlockSpec((SUBLANES, W), functools.partial(lambda i, col: (halo(i), col), col=col)) for col in (1, 2)]
    return cols, halos


def _mixers_fwd(proj, conv_w, wm, bias_e, g_a, g_b, seq_len, name, stages=()):
    T, W5 = proj.shape
    W = W5 // 5
    H = W // HEAD_DIM
    per_seq = seq_len // CHUNK
    rb = CHUNK // SUBLANES
    cols, halos = _mixer_specs(W, lambda i: i, lambda i: jnp.maximum(i * rb - 1, 0))

    def body(b_ref, c_ref, hin_ref, u_ref, v_ref, ch_ref, hh_ref, cw_ref, wm_ref, be_ref, ga_ref, gb_ref, y_ref, s_ref):
        first = (pl.program_id(0) % per_seq) == 0
        hc = c_ref[...] * hin_ref[...]
        hc_prev = jnp.where(first, 0.0, ch_ref[...] * hh_ref[...])
        cw = cw_ref[...]
        ya = b_ref[...] * (cw[0:1, :] * _shift_rows(hc, hc_prev, 2) + cw[1:2, :] * _shift_rows(hc, hc_prev, 1)
                           + cw[2:3, :] * hc)
        y_ref[:, 0:W] = (ya * _row_inv(ya) * ga_ref[...]).astype(BF16)
        gu, _ = _gelu(u_ref[...])
        gv, _ = _gelu(v_ref[...])
        gvb = gv.astype(BF16)
        for hd in range(H):
            sl = slice(hd * HEAD_DIM, (hd + 1) * HEAD_DIM)
            s_ref[:, sl] = jnp.dot(wm_ref[hd], gvb[:, sl], preferred_element_type=F32)
        yb = gu * (s_ref[...] + be_ref[...])
        y_ref[:, W:2 * W] = (yb * _row_inv(yb) * gb_ref[...]).astype(BF16)

    full = lambda shape: pl.BlockSpec(shape, lambda i: (0,) * len(shape))
    outs, carried = _call(
        body, name=name, args=[proj, proj, proj, proj, proj, proj, proj, conv_w, wm, bias_e, g_a, g_b],
        grid=(T // CHUNK,),
        in_specs=cols + halos + [full((CONV_K, W)), full((H, CHUNK, CHUNK)), full((CHUNK, W)), full((1, W)), full((1, W))],
        out_specs=[pl.BlockSpec((CHUNK, 2 * W), lambda i: (i, 0))], out_shape=[jax.ShapeDtypeStruct((T, 2 * W), BF16)],
        scratch_shapes=[pltpu.VMEM((CHUNK, W), F32)], semantics=("parallel",), stages=stages)
    return outs[0], carried


def _mixers_bwd(dy, proj, conv_w, wm, wmt, bias_e, g_a, g_b, head_onehot, seq_len, name, stages=()):
    T, W5 = proj.shape
    W = W5 // 5
    H = W // HEAD_DIM
    nb = T // CHUNK
    per_seq = seq_len // CHUNK
    rb = CHUNK // SUBLANES
    blk = lambda i: nb - 1 - i
    cols, halos = _mixer_specs(W, blk, lambda i: jnp.maximum(blk(i) * rb - 1, 0))

    def body(dy_ref, b_ref, c_ref, hin_ref, u_ref, v_ref, ch_ref, hh_ref, cw_ref, wm_ref, wmt_ref, be_ref, ga_ref,
             gb_ref, oh_ref, dp_ref, dcw_ref, dga_ref, dgb_ref, dws_ref, dbt_ref, carry_ref, s_ref, dgv_ref):
        i = pl.program_id(0)
        j = nb - 1 - i

        @pl.when(i == 0)
        def _():
            for r in (dcw_ref, dga_ref, dgb_ref, dws_ref, dbt_ref, carry_ref):
                r[...] = jnp.zeros_like(r)

        first = (j % per_seq) == 0
        last = (j % per_seq) == per_seq - 1
        b, c, hin = b_ref[...], c_ref[...], hin_ref[...]
        cw = cw_ref[...]
        hc = c * hin
        hc_prev = jnp.where(first, 0.0, ch_ref[...] * hh_ref[...])
        hc1 = _shift_rows(hc, hc_prev, 1)
        hc2 = _shift_rows(hc, hc_prev, 2)
        conv = cw[0:1, :] * hc2 + cw[1:2, :] * hc1 + cw[2:3, :] * hc
        ya = b * conv
        inv_a = _row_inv(ya)
        na = ya * inv_a
        do_a = dy_ref[:, 0:W]
        dga_ref[...] += jnp.sum(do_a * na, axis=0, keepdims=True)
        dna = do_a * ga_ref[...]
        dya = inv_a * (dna - na * jnp.mean(dna * na, axis=-1, keepdims=True))
        dp_ref[:, 0:W] = (dya * conv).astype(BF16)
        dconv = dya * b
        dcw_ref[0:1, :] += jnp.sum(dconv * hc2, axis=0, keepdims=True)
        dcw_ref[1:2, :] += jnp.sum(dconv * hc1, axis=0, keepdims=True)
        dcw_ref[2:3, :] += jnp.sum(dconv * hc, axis=0, keepdims=True)
        nxt = jnp.where(last, 0.0, carry_ref[...])
        dhc = cw[2:3, :] * dconv + cw[1:2, :] * _unshift_rows(dconv, nxt, 1) + cw[0:1, :] * _unshift_rows(dconv, nxt, 2)
        carry_ref[...] = dconv[0:SUBLANES, :]
        dp_ref[:, W:2 * W] = (dhc * hin).astype(BF16)
        dp_ref[:, 2 * W:3 * W] = (dhc * c).astype(BF16)
        u, v = u_ref[...], v_ref[...]
        gu, thu = _gelu(u)
        gv, thv = _gelu(v)
        gvb = gv.astype(BF16)
        for hd in range(H):
            sl = slice(hd * HEAD_DIM, (hd + 1) * HEAD_DIM)
            s_ref[:, sl] = jnp.dot(wm_ref[hd], gvb[:, sl], preferred_element_type=F32)
        s = s_ref[...] + be_ref[...]
        yb = gu * s
        inv_b = _row_inv(yb)
        nbv = yb * inv_b
        do_b = dy_ref[:, W:2 * W]
        dgb_ref[...] += jnp.sum(do_b * nbv, axis=0, keepdims=True)
        dnb = do_b * gb_ref[...]
        dyb = inv_b * (dnb - nbv * jnp.mean(dnb * nbv, axis=-1, keepdims=True))
        dp_ref[:, 3 * W:4 * W] = (dyb * s * _gelu_grad(u, thu)).astype(BF16)
        dsb = (dyb * gu).astype(BF16)
        dbt_ref[...] += jnp.dot(dsb, oh_ref[...], preferred_element_type=F32)
        for hd in range(H):
            sl = slice(hd * HEAD_DIM, (hd + 1) * HEAD_DIM)
            dws_ref[hd] += lax.dot_general(dsb[:, sl], gvb[:, sl], (((1,), (1,)), ((), ())), preferred_element_type=F32)
            dgv_ref[:, sl] = jnp.dot(wmt_ref[hd], dsb[:, sl], preferred_element_type=F32)
        dp_ref[:, 4 * W:5 * W] = (dgv_ref[...] * _gelu_grad(v, thv)).astype(BF16)

    full = lambda shape: pl.BlockSpec(shape, lambda i: (0,) * len(shape))
    return _call(
        body, name=name, grid=(nb,),
        args=[dy, proj, proj, proj, proj, proj, proj, proj, conv_w, wm, wmt, bias_e, g_a, g_b, head_onehot],
        in_specs=[pl.BlockSpec((CHUNK, 2 * W), lambda i: (blk(i), 0))] + cols + halos
        + [full((CONV_K, W)), full((H, CHUNK, CHUNK)), full((H, CHUNK, CHUNK)), full((CHUNK, W)), full((1, W)),
           full((1, W)), full((W, LANES))],
        out_specs=[pl.BlockSpec((CHUNK, 5 * W), lambda i: (blk(i), 0)), full((SUBLANES, W)), full((1, W)), full((1, W)),
                   full((H, CHUNK, CHUNK)), full((CHUNK, LANES))],
        out_shape=[jax.ShapeDtypeStruct((T, 5 * W), BF16), jax.ShapeDtypeStruct((SUBLANES, W), F32),
                   jax.ShapeDtypeStruct((1, W), F32), jax.ShapeDtypeStruct((1, W), F32),
                   jax.ShapeDtypeStruct((H, CHUNK, CHUNK), F32), jax.ShapeDtypeStruct((CHUNK, LANES), F32)],
        scratch_shapes=[pltpu.VMEM((SUBLANES, W), F32), pltpu.VMEM((CHUNK, W), F32), pltpu.VMEM((CHUNK, W), F32)],
        semantics=("arbitrary",), stages=stages)


def _cast_into_slot(w, chip, name):
    R, C = w.shape
    tr = _tile(R, 256, 16)

    def body(chip_ref, w_ref, o_ref):
        o_ref[...] = w_ref[...].astype(BF16)

    return pl.pallas_call(
        body, name=name,
        grid_spec=pltpu.PrefetchScalarGridSpec(
            num_scalar_prefetch=1, grid=(R // tr,),
            in_specs=[pl.BlockSpec((tr, C), lambda i, chip_ref: (i, 0))],
            out_specs=pl.BlockSpec((None, tr, C), lambda i, chip_ref: (chip_ref[0], i, 0))),
        out_shape=jax.ShapeDtypeStruct((N_CHIPS, R, C), BF16),
        compiler_params=_params(("parallel",)),
    )(chip, w)


def _rs_pair_add(dw, got, chip_core, name):
    S, R, C = dw.shape
    hr = R // 2
    tr = _tile(hr, 256, 16)
    nrb = hr // tr

    def body(cc_ref, dw_ref, got_ref, send_ref, own_ref):
        s = dw_ref[...].astype(F32) + got_ref[...].astype(F32)
        send_ref[...] = s.astype(BF16)

        @pl.when(pl.program_id(1) == cc_ref[0])
        def _():
            own_ref[...] = s

    return pl.pallas_call(
        body, name=name,
        grid_spec=pltpu.PrefetchScalarGridSpec(
            num_scalar_prefetch=1, grid=(nrb, S),
            in_specs=[pl.BlockSpec((None, tr, C), lambda i, q, cc: (q, cc[1] * nrb + i, 0)),
                      pl.BlockSpec((None, tr, C), lambda i, q, cc: (q, i, 0))],
            out_specs=[pl.BlockSpec((None, tr, C), lambda i, q, cc: (q, i, 0)),
                       pl.BlockSpec((tr, C), lambda i, q, cc: (i, 0))]),
        out_shape=[jax.ShapeDtypeStruct((S, hr, C), BF16), jax.ShapeDtypeStruct((hr, C), F32)],
        compiler_params=_params(("parallel", "arbitrary")),
    )(chip_core, dw, got)


def _rs_final_add(own, got, chip_core, name):
    hr, C = own.shape
    tr = _tile(hr, 256, 16)
    nrb = hr // tr

    def body(cc_ref, own_ref, got_ref, o_ref):
        o_ref[...] = ((own_ref[...] + got_ref[0].astype(F32)) + got_ref[1].astype(F32)) + got_ref[2].astype(F32)

    return pl.pallas_call(
        body, name=name,
        grid_spec=pltpu.PrefetchScalarGridSpec(
            num_scalar_prefetch=1, grid=(nrb,),
            in_specs=[pl.BlockSpec((tr, C), lambda i, cc: (i, 0)), pl.BlockSpec((3, tr, C), lambda i, cc: (0, i, 0))],
            out_specs=pl.BlockSpec((tr, C), lambda i, cc: (cc[1] * nrb + i, 0))),
        out_shape=jax.ShapeDtypeStruct((2 * hr, C), F32),
        compiler_params=_params(("parallel",)),
    )(chip_core, own, got)


def _adamw_math(w, g, m, v):
    m2 = ADAM_B1 * m + (1.0 - ADAM_B1) * g
    v2 = ADAM_B2 * v + (1.0 - ADAM_B2) * (g * g)
    delta = -ADAM_LR * ((m2 / ADAM_C1) / (jnp.sqrt(v2 / ADAM_C2) + ADAM_EPS) + ADAM_WD * w)
    return delta, m2, v2


def _adamw(w, g, m, v, name):
    R, C = w.shape
    parts = list(g) if isinstance(g, (list, tuple)) else [g]
    n_parts = len(parts)
    tr = _tile(R // n_parts, max(SUBLANES, (256 * 1024) // C), SUBLANES)
    per = R // n_parts // tr

    def body(w_ref, m_ref, v_ref, *refs):
        g_refs, (g2_ref, d_ref, m2_ref, v2_ref) = refs[:n_parts], refs[n_parts:]
        g = g_refs[0][...]
        for p in range(1, n_parts):
            g = jnp.where(pl.program_id(0) >= p * per, g_refs[p][...], g)
        g2_ref[...] = g
        d_ref[...], m2_ref[...], v2_ref[...] = _adamw_math(w_ref[...], g, m_ref[...], v_ref[...])

    blk = pl.BlockSpec((tr, C), lambda i: (i, 0))
    g_specs = [pl.BlockSpec((tr, C), functools.partial(lambda i, p: (jnp.clip(i - p * per, 0, per - 1), 0), p=p))
               for p in range(n_parts)]
    return _call(body, name=name, args=[w, m, v, *parts], grid=(R // tr,), in_specs=[blk] * 3 + g_specs,
                 out_specs=[blk] * 4, out_shape=[jax.ShapeDtypeStruct((R, C), F32)] * 4,
                 semantics=("arbitrary",))[0]


def _adamw_sc(w, g, m, v, name):
    R, C = w.shape
    blk = (SUBLANES, 512)
    assert R % blk[0] == 0 and C % blk[1] == 0
    mesh = plsc.VectorSubcoreMesh(core_axis_name="sc_core", subcore_axis_name="sc_tile", num_cores=1)
    spec = pl.BlockSpec(block_shape=blk, index_map=lambda i, j: (i, j))

    def kern(w_hbm, g_hbm, m_hbm, v_hbm, g2_hbm, d_hbm, m2_hbm, v2_hbm):
        def body(w_v, g_v, m_v, v_v, g2_v, d_v, m2_v, v2_v):
            for r in range(blk[0]):
                @plsc.parallel_loop(0, blk[1], SC_LANES, unroll=8)
                def _(c):
                    at = (pl.ds(r, 1), pl.ds(c, SC_LANES))
                    gv = g_v.at[*at][...]
                    g2_v.at[*at][...] = gv
                    d_v.at[*at][...], m2_v.at[*at][...], v2_v.at[*at][...] = _adamw_math(
                        w_v.at[*at][...], gv, m_v.at[*at][...], v_v.at[*at][...])

        pltpu.emit_pipeline(
            body, grid=(R // blk[0], C // blk[1]), in_specs=[spec] * 4, out_specs=[spec] * 4,
            core_axis_name=("sc_core", "sc_tile"), dimension_semantics=(pltpu.PARALLEL, pltpu.PARALLEL),
        )(w_hbm, g_hbm, m_hbm, v_hbm, g2_hbm, d_hbm, m2_hbm, v2_hbm)

    return pl.kernel(kern, name=name, out_type=[jax.ShapeDtypeStruct((R, C), F32)] * 4, mesh=mesh,
                     scratch_types=[])(w, g, m, v)


def _all_gather_small(block, name):
    m_per, n = block.shape

    def body(x_ref, out_ref, send_sems, recv_sems, local_sem):
        x, y, c = _position()
        me, sibling = (x, y, c), (x, y, 1 - c)
        chips = _other_chips(x, y)

        def rows(px, py, pc):
            return out_ref.at[pl.ds((4 * px + 2 * py + pc) * m_per, m_per), :]

        def copy(k, blk, to, src=None):
            return pltpu.make_async_remote_copy(src_ref=rows(*blk) if src is None else src, dst_ref=rows(*blk),
                                                send_sem=send_sems.at[k], recv_sem=recv_sems.at[k], device_id=to,
                                                device_id_type=MESH)

        mine = pltpu.make_async_copy(x_ref, rows(*me), local_sem)
        mine.start()
        first = [copy(0, me, sibling, src=x_ref)]
        first += [copy(1 + j, me, (*chip, c), src=x_ref) for j, chip in enumerate(chips)]
        for cp in first:
            cp.start()
        passed = [copy(4 + j, (*chip, c), sibling) for j, chip in enumerate(chips)]
        for j, chip in enumerate(chips):
            copy(1 + j, (*chip, c), me).wait_recv()
            passed[j].start()
        copy(0, sibling, me).wait_recv()
        for j, chip in enumerate(chips):
            copy(4 + j, (*chip, 1 - c), me).wait_recv()
        for cp in first + passed:
            cp.wait_send()
        mine.wait()

    return pl.pallas_call(
        body, name=name,
        in_specs=[pl.BlockSpec(memory_space=pltpu.VMEM)],
        out_specs=pl.BlockSpec(memory_space=pltpu.VMEM),
        out_shape=jax.ShapeDtypeStruct((N_DEV * m_per, n), block.dtype),
        scratch_shapes=[pltpu.SemaphoreType.DMA((7,)), pltpu.SemaphoreType.DMA((7,)), pltpu.SemaphoreType.DMA],
        compiler_params=pltpu.CompilerParams(vmem_limit_bytes=VMEM_LIMIT_V7X, has_side_effects=True),
    )(block)


def _sum_and_adamw_small(gathered, w, m, v, name):
    rows, n = w.shape
    tr = _tile(rows, 32, SUBLANES)

    def body(p_ref, w_ref, m_ref, v_ref, g_ref, d_ref, m2_ref, v2_ref):
        g = p_ref[0]
        for d in range(1, N_DEV):
            g = g + p_ref[d]
        g_ref[...] = g
        d_ref[...], m2_ref[...], v2_ref[...] = _adamw_math(w_ref[...], g, m_ref[...], v_ref[...])

    blk = pl.BlockSpec((tr, n), lambda i: (i, 0))
    return pl.pallas_call(
        body, name=name, grid=(rows // tr,),
        in_specs=[pl.BlockSpec((N_DEV, tr, n), lambda i: (0, i, 0))] + [blk] * 3,
        out_specs=[blk] * 4,
        out_shape=[jax.ShapeDtypeStruct((rows, n), F32)] * 4,
        compiler_params=_params(("parallel",)),
    )(gathered.reshape(N_DEV, rows, n), w, m, v)


def _pad_rows(a):
    pad = (-a.shape[0]) % SUBLANES
    return jnp.pad(a, ((0, pad), (0, 0))) if pad else a


class _SmallPack:
    def __init__(self, W, D, H, chip):
        self.W, self.D, self.H, self.chip = W, D, H, chip
        self.offsets = {}
        self.rows = 0

    def pack(self, pieces):
        out = []
        self.offsets, self.rows = {}, 0
        for name, a in pieces:
            a = _pad_rows(a.astype(F32))
            self.offsets[name] = (self.rows, a.shape[0])
            self.rows += a.shape[0]
            out.append(a)
        return jnp.concatenate(out, axis=0)

    def piece(self, packed, name):
        start, n = self.offsets[name]
        return packed[start:start + n]


def _bias_rows(b, W):
    bt = jnp.pad(b.T, ((0, 0), (0, LANES - b.shape[0])))
    return bt.reshape(-1, W)


def _bias_from_rows(rows, H):
    return rows.reshape(-1)[:CHUNK * LANES].reshape(CHUNK, LANES)[:, :H].T


def kernel(x, mix_norm_g, w_in, conv_w, spatial_w, spatial_b, conv_out_norm_g, gmlp_out_norm_g, w_out, mlp_norm_g, w_up, w_down, final_norm_g, loss_target, m_mix_norm_g, m_w_in, m_conv_w, m_spatial_w, m_spatial_b, m_conv_out_norm_g, m_gmlp_out_norm_g, m_w_out, m_mlp_norm_g, m_w_up, m_w_down, m_final_norm_g, v_mix_norm_g, v_w_in, v_conv_w, v_spatial_w, v_spatial_b, v_conv_out_norm_g, v_gmlp_out_norm_g, v_w_out, v_mlp_norm_g, v_w_up, v_w_down, v_final_norm_g):
    Bl, S, D = x.shape
    T = Bl * S
    W = conv_out_norm_g.shape[-1]
    H = W // HEAD_DIM
    Wl = conv_w.shape[-1]
    xi, yi, ci = _position()
    chip = (2 * xi + yi).astype(jnp.int32)
    chip_arr = chip.reshape(1)
    chip_core = jnp.stack([chip, ci.astype(jnp.int32)])

    x2 = x.reshape(T, D)
    tgt2 = loss_target.reshape(T, D)

    s_in = _cast_into_slot(w_in[0], chip_arr, "cast_w_in")
    up_rows = w_up.shape[1] // 2
    up_cuts = [0] + [up_rows * pct // 100 // 16 * 16 for pct in (43, 57, 82)] + [up_rows]
    up_part = [(lo, hi - lo) for lo, hi in zip(up_cuts[:-1], up_cuts[1:])]

    causal = jnp.tril(jnp.ones((CHUNK, CHUNK), dtype=bool))
    wm = jnp.where(causal[None], spatial_w[0], 0.0).astype(BF16)
    wmt = jnp.swapaxes(wm, 1, 2)
    bias_e = jnp.repeat(spatial_b[0].T, HEAD_DIM, axis=1)
    conv_full = lax.dynamic_update_slice(jnp.zeros((CONV_K, W), F32), conv_w[0], (0, chip * Wl))
    head_onehot = (jnp.arange(W)[:, None] // HEAD_DIM == jnp.arange(LANES)[None, :]).astype(BF16)
    g_a, g_b = conv_out_norm_g, gmlp_out_norm_g

    xn, (s_out, s_up, s_down), ((g_in,), (conv_gathered,)) = _rmsnorm_fwd_and_casts(
        x2, mix_norm_g, [w_out[0], w_up[0], w_down[0]], chip_arr, "mix_norm_fwd",
        stages=[_GatherRows(s_in), _GatherSmall(_pad_rows(conv_full))])
    conv_w_all = conv_gathered.reshape(N_DEV, SUBLANES, W)[:, :CONV_K]
    conv_w_all = conv_w_all[0] + conv_w_all[2] + conv_w_all[4] + conv_w_all[6]
    proj, ((g_out,), (g_up,)) = _matmul(xn, g_in, mode="nn", name="proj_fwd", tm=1024, tn=512, tk=4096,
                                        out_dtypes=[F32], b_shard="n",
                                        stages=[_GatherRows(s_out), _GatherRows(s_up, *up_part[0])])
    g_out = g_out.reshape(-1, D)
    y, ((g_up,),) = _mixers_fwd(proj, conv_w_all, wm, bias_e, g_a, g_b, S, "mixers_fwd",
                                stages=[_GatherRows(g_up, *up_part[1])])
    h1, ((g_up,),) = _matmul(y, g_out, mode="nn", name="out_proj_fwd", tm=1024, tn=512, tk=4096, out_dtypes=[F32],
                             epilogue=_ep_residual, extras=(x2,), stages=[_GatherRows(g_up, *up_part[2])])
    xn2, ((g_up,),) = _rmsnorm_fwd(h1, mlp_norm_g, "mlp_norm_fwd", stages=[_GatherRows(g_up, *up_part[3])])
    (r, a), ((g_down,),) = _matmul(xn2, g_up, mode="nn", name="up_fwd", tm=1024, tn=1024, tk=4096, n_sub=2,
                                   out_dtypes=[BF16, BF16], epilogue=_ep_relu2, b_shard="n",
                                   stages=[_GatherRows(s_down)])
    g_down = g_down.reshape(-1, D)
    d2, _ = _matmul(a, g_down, mode="nn", name="down_fwd", tm=2048, tn=1024, tk=1024, out_dtypes=[F32])
    dh2, dh2b, d_final_g, loss_part = _loss_and_final_norm_bwd(h1, d2, tgt2, final_norm_g.reshape(1, D),
                                                               "loss_final_norm")

    def rs_adds(dw, got, tag):
        return _rs_pair_add(dw, got, chip_core, f"rs_pair_add_{tag}")

    dw_down, _ = _matmul(a, dh2b, mode="tn", name="down_dw", tm=1024, tn=1024, tk=4096, n_sub=2, out_dtypes=[BF16])
    dw_down = dw_down.reshape(N_CHIPS, -1, D)
    dpre, ((got_down,),) = _matmul(dh2b, g_down, mode="nt", name="down_dx", tm=1024, tn=1024, tk=4096, n_sub=2,
                                   out_dtypes=[BF16], epilogue=_ep_relu2_bwd, extras=(r,),
                                   stages=[_PairExchange(dw_down)])
    part_down, own_down = rs_adds(dw_down, got_down, "w_down")
    rows_down = part_down.shape[1]
    down_a = rows_down * 3 // 4 // 16 * 16
    dw_up, ((landed_down,),) = _matmul(xn2, dpre, mode="tn", name="up_dw", tm=1024, tn=1024, tk=4096, n_sub=2,
                                       out_dtypes=[BF16], out_shard=True,
                                       stages=[_ChipExchange(part_down, None, 0, down_a)])
    dxn2, ((got_up,), (landed_down,)) = _matmul(
        dpre, g_up, mode="nt", name="up_dx", tm=2048, tn=1024, tk=1024, out_dtypes=[F32], b_shard="k",
        stages=[_PairExchange(dw_up), _ChipExchange(part_down, landed_down, down_a, rows_down - down_a)])
    half_down = _rs_final_add(own_down, landed_down, chip_core, "rs_final_add_w_down")
    part_up, own_up = rs_adds(dw_up, got_up, "w_up")
    rows_up = part_up.shape[1]
    up_cut = [0] + [rows_up * pct // 100 // 16 * 16 for pct in (12, 32, 57, 83)] + [rows_up]
    up_rs = [(lo, hi - lo) for lo, hi in zip(up_cut[:-1], up_cut[1:])]
    (dh1, dh1b, d_mlp_g), ((grad_down,), (landed_up,)) = _rmsnorm_bwd(
        dxn2, h1, mlp_norm_g, dh2, "mlp_norm_bwd",
        stages=[_HalfExchange(half_down), _ChipExchange(part_up, None, *up_rs[0])])
    dy, ((landed_up,),) = _matmul(dh1b, g_out, mode="nt", name="out_proj_dx", tm=1024, tn=512, tk=4096,
                                  out_dtypes=[F32], stages=[_ChipExchange(part_up, landed_up, *up_rs[1])])
    (dproj, d_conv, d_ga, d_gb, d_ws, d_bt), ((landed_up,),) = _mixers_bwd(
        dy, proj, conv_w_all, wm, wmt, bias_e, g_a, g_b, head_onehot, S, "mixers_bwd",
        stages=[_ChipExchange(part_up, landed_up, *up_rs[2])])
    pack = _SmallPack(W, D, H, chip)
    loss_row = jnp.pad(loss_part[:, :1], ((0, 0), (0, W - 1)))
    g_part = pack.pack([("spatial_w", (d_ws * causal.astype(F32)[None]).reshape(-1, W)), ("conv_w", d_conv),
                        ("mlp_norm_g", d_mlp_g.reshape(-1, W)), ("final_norm_g", d_final_g.reshape(-1, W)),
                        ("conv_out_norm_g", d_ga), ("gmlp_out_norm_g", d_gb), ("spatial_b", d_bt.reshape(-1, W)),
                        ("loss", loss_row)])
    tm_in = min(1024, D // 2)
    nb_in = D // tm_in // 2
    dw_in_a, ((landed_up,),) = _matmul(
        xn, dproj, mode="tn", name="proj_dw_a", tm=tm_in, tn=512, tk=4096, out_dtypes=[BF16], out_shard=True,
        m_blocks=(0, nb_in), stages=[_ChipExchange(part_up, landed_up, *up_rs[3])])
    dw_in_b, ((landed_up,), (got_in_a,), (g_all,)) = _matmul(
        xn, dproj, mode="tn", name="proj_dw_b", tm=tm_in, tn=512, tk=4096, out_dtypes=[BF16], out_shard=True,
        m_blocks=(nb_in, nb_in),
        stages=[_ChipExchange(part_up, landed_up, *up_rs[4]), _PairExchange(dw_in_a), _GatherSmall(g_part)])
    half_up = _rs_final_add(own_up, landed_up, chip_core, "rs_final_add_w_up")
    part_in_a, own_in_a = rs_adds(dw_in_a, got_in_a, "w_in_a")
    rows_in = part_in_a.shape[1]
    in_a = rows_in * 7 // 10 // 16 * 16
    dw_out, ((landed_in_a,), (got_in_b,), (grad_up,)) = _matmul(
        y, dh1b, mode="tn", name="out_proj_dw", tm=1024, tn=1024, tk=4096, n_sub=2, out_dtypes=[BF16],
        stages=[_ChipExchange(part_in_a, None, 0, in_a), _PairExchange(dw_in_b), _HalfExchange(half_up)])
    dw_out = dw_out.reshape(N_CHIPS, -1, D)
    part_in_b, own_in_b = rs_adds(dw_in_b, got_in_b, "w_in_b")
    dxn, ((landed_in_a,), (landed_in_b,), (got_out,)) = _matmul(
        dproj, g_in, mode="nt", name="proj_dx", tm=2048, tn=1024, tk=1280, out_dtypes=[F32], b_shard="k",
        stages=[_ChipExchange(part_in_a, landed_in_a, in_a, rows_in - in_a), _ChipExchange(part_in_b),
                _PairExchange(dw_out)])
    part_out, own_out = rs_adds(dw_out, got_out, "w_out")
    half_in_a = _rs_final_add(own_in_a, landed_in_a, chip_core, "rs_final_add_w_in_a")
    half_in_b = _rs_final_add(own_in_b, landed_in_b, chip_core, "rs_final_add_w_in_b")
    (grad_x, _unused, d_mix_g), ((landed_out,), (grad_in_a,), (grad_in_b,)) = _rmsnorm_bwd(
        dxn, x2, mix_norm_g, dh1, "mix_norm_bwd",
        stages=[_ChipExchange(part_out), _HalfExchange(half_in_a), _HalfExchange(half_in_b)])
    grad_in = [grad_in_a, grad_in_b]
    half_out = _rs_final_add(own_out, landed_out, chip_core, "rs_final_add_w_out")
    ((grad_out,),) = _run_stages([_HalfExchange(half_out)], "rs_half_exchange_w_out")
    big = {"w_down": _adamw_sc(w_down[0], grad_down, m_w_down[0], v_w_down[0], "adamw_sc_w_down"),
           "w_up": _adamw_sc(w_up[0], grad_up, m_w_up[0], v_w_up[0], "adamw_sc_w_up"),
           "w_in": _adamw(w_in[0], grad_in, m_w_in[0], v_w_in[0], "adamw_w_in"),
           "w_out": _adamw(w_out[0], grad_out, m_w_out[0], v_w_out[0], "adamw_w_out")}
    big = {k: [t[None] for t in v] for k, v in big.items()}

    def small(conv, sw, sb, ga, gb, mlp, fin):
        return pack.pack([("spatial_w", sw.reshape(-1, W)), ("conv_w", conv), ("mlp_norm_g", mlp.reshape(-1, W)),
                          ("final_norm_g", fin.reshape(-1, W)), ("conv_out_norm_g", ga.reshape(-1, W)),
                          ("gmlp_out_norm_g", gb.reshape(-1, W)), ("spatial_b", _bias_rows(sb, W)),
                          ("loss", jnp.zeros((1, W), F32))])

    def full_conv(cw):
        return lax.dynamic_update_slice(jnp.zeros((CONV_K, W), F32), cw[0], (0, chip * Wl))

    def mix_rows(a):
        return _pad_rows(a.reshape(-1, W))

    w_s = small(full_conv(conv_w), spatial_w, spatial_b[0], conv_out_norm_g, gmlp_out_norm_g, mlp_norm_g, final_norm_g)
    m_s = small(full_conv(m_conv_w), m_spatial_w, m_spatial_b[0], m_conv_out_norm_g, m_gmlp_out_norm_g, m_mlp_norm_g,
                m_final_norm_g)
    v_s = small(full_conv(v_conv_w), v_spatial_w, v_spatial_b[0], v_conv_out_norm_g, v_gmlp_out_norm_g, v_mlp_norm_g,
                v_final_norm_g)
    small_outs = _sum_and_adamw_small(g_all, w_s, m_s, v_s, "sum_adamw_small")
    mix_all = _all_gather_small(mix_rows(d_mix_g), "all_gather_mix_norm_grad")
    mix_outs = _sum_and_adamw_small(mix_all, mix_rows(mix_norm_g), mix_rows(m_mix_norm_g), mix_rows(v_mix_norm_g),
                                    "sum_adamw_mix_norm")

    def unpack(kind, name):
        if name == "mix_norm_g":
            return mix_outs[kind].reshape(-1)[:D].reshape(1, D)
        rows = pack.piece(small_outs[kind], name)
        if name == "spatial_w":
            return rows.reshape(1, H, CHUNK, CHUNK)
        if name == "conv_w":
            return lax.dynamic_slice(rows[:CONV_K], (0, chip * Wl), (CONV_K, Wl))[None]
        if name == "spatial_b":
            return _bias_from_rows(rows, H)[None]
        if name == "final_norm_g":
            return rows.reshape(-1)[:D]
        n = D if name == "mlp_norm_g" else W
        return rows.reshape(-1)[:n].reshape(1, n)

    loss = pack.piece(small_outs[0], "loss")[0, 0]
    order = ["mix_norm_g", "w_in", "conv_w", "spatial_w", "spatial_b", "conv_out_norm_g", "gmlp_out_norm_g", "w_out",
             "mlp_norm_g", "w_up", "w_down", "final_norm_g"]
    outs = [loss, grad_x.reshape(Bl, S, D)]
    for kind in range(4):
        for name in order:
            outs.append(big[name][kind] if name in big else unpack(kind, name))
    return tuple(outs)
```

```python
import functools
import math

import jax
import jax.numpy as jnp
from jax import lax
from jax.experimental import pallas as pl
from jax.experimental.pallas import tpu as pltpu
from jax.experimental.pallas import tpu_sc as plsc

F32 = jnp.float32
BF16 = jnp.bfloat16
MESH = pl.DeviceIdType.MESH

NORM_EPS = 1e-5
HEAD_DIM = 128
CHUNK = 128
CONV_K = 3
N_CHIPS = 4
N_DEV = 8

ADAM_LR = 0.001
ADAM_B1 = 0.9
ADAM_B2 = 0.999
ADAM_EPS = 1e-08
ADAM_WD = 0.01
ADAM_STEP = 10
ADAM_C1 = 1.0 - ADAM_B1 ** ADAM_STEP
ADAM_C2 = 1.0 - ADAM_B2 ** ADAM_STEP

GELU_K = math.sqrt(2.0 / math.pi)
GELU_A = 0.044715

VMEM_LIMIT_V7X = 56 * 1024 * 1024
SUBLANES = 8
LANES = 128
SC_LANES = 16


def _tile(dim, target, mult=LANES):
    if dim <= target:
        return dim
    t = (target // mult) * mult
    while t > mult and dim % t:
        t -= mult
    assert dim % t == 0, (dim, target, mult)
    return t


def _params(sem=None):
    return pltpu.CompilerParams(dimension_semantics=sem, vmem_limit_bytes=VMEM_LIMIT_V7X)


class _Stage:
    bufs = ()
    n_sems = 0
    MIDDLE_AT = 0.6
    base = 0

    def start(self, refs, send, recv):
        raise NotImplementedError

    def middle(self, refs, send, recv):
        pass

    def finish(self, refs, send, recv):
        raise NotImplementedError


def _position():
    return lax.axis_index("x"), lax.axis_index("y"), lax.axis_index("c")


def _other_chips(x, y):
    return [(1 - x, y), (x, 1 - y), (1 - x, 1 - y)]


def _remote(src, dst, send, recv, k, to):
    return pltpu.make_async_remote_copy(src_ref=src, dst_ref=dst, send_sem=send.at[k], recv_sem=recv.at[k],
                                        device_id=to, device_id_type=MESH)


def _call(body, *, name, args, in_specs, out_specs, out_shape, grid=(), scratch_shapes=(), semantics=None, stages=(),
          prefetch=None):
    n_in, n_out, n_scratch = len(args), len(out_shape), len(scratch_shapes)
    n_pre = 0 if prefetch is None else 1
    any_spec = pl.BlockSpec(memory_space=pl.ANY)
    extra_args, extra_out, aliases, layout = [], [], {}, []
    for st in stages:
        where = []
        for kind, buf in st.bufs:
            if kind in ("in", "alias"):
                extra_args.append(buf)
                pos_in = n_in + len(extra_args) - 1
            if kind in ("out", "alias"):
                extra_out.append(jax.ShapeDtypeStruct(buf.shape, buf.dtype))
                pos_out = n_out + len(extra_out) - 1
            if kind == "alias":
                aliases[n_pre + pos_in] = pos_out
            where.append(("in", pos_in) if kind == "in" else ("out", pos_out))
        layout.append(where)
    n_sems = sum(st.n_sems for st in stages)
    n_xin, n_xout = len(extra_args), len(extra_out)

    def wrapped(*refs):
        pre, refs = refs[:n_pre], refs[n_pre:]
        ins = refs[:n_in + n_xin]
        outs = refs[n_in + n_xin:n_in + n_xin + n_out + n_xout]
        scratch = refs[n_in + n_xin + n_out + n_xout:]
        main = pre + ins[:n_in] + outs[:n_out] + scratch[:n_scratch]
        if not stages:
            body(*main)
            return
        send, recv = scratch[n_scratch], scratch[n_scratch + 1]
        step, n_steps = 0, 1
        for d, g in enumerate(grid):
            step = step * g + pl.program_id(d)
            n_steps *= g
        base, views = 0, []
        for st, where in zip(stages, layout):
            st_refs = [ins[p] if side == "in" else outs[p] for side, p in where]
            st.base = base
            views.append((st, st_refs, send, recv))
            base += st.n_sems

        def starts():
            for st, r, s, v in views:
                st.start(r, s, v)

        def middles():
            for st, r, s, v in views:
                st.middle(r, s, v)

        def finishes():
            for st, r, s, v in views:
                st.finish(r, s, v)

        if not grid:
            starts()
            body(*main)
            middles()
            finishes()
        else:
            pl.when(step == 0)(starts)
            body(*main)
            pl.when(step == min(int(n_steps * _Stage.MIDDLE_AT), n_steps - 1))(middles)
            pl.when(step == n_steps - 1)(finishes)

    sems = [pltpu.SemaphoreType.DMA((n_sems,)), pltpu.SemaphoreType.DMA((n_sems,))] if stages else []
    specs = dict(in_specs=list(in_specs) + [any_spec] * n_xin, out_specs=list(out_specs) + [any_spec] * n_xout,
                 scratch_shapes=list(scratch_shapes) + sems)
    if prefetch is None:
        kw = dict(specs, **(dict(grid=grid) if grid else {}))
    else:
        kw = dict(grid_spec=pltpu.PrefetchScalarGridSpec(num_scalar_prefetch=1, grid=grid, **specs))
    res = pl.pallas_call(
        wrapped, name=name,
        out_shape=list(out_shape) + extra_out,
        input_output_aliases=aliases,
        compiler_params=pltpu.CompilerParams(
            dimension_semantics=("arbitrary",) * len(grid) if stages and grid else semantics,
            vmem_limit_bytes=VMEM_LIMIT_V7X, has_side_effects=bool(stages)),
        **kw,
    )(*([] if prefetch is None else [prefetch]), *args, *extra_args)
    main_res, stage_res, pos = list(res[:n_out]), [], n_out
    for st in stages:
        k = sum(kind in ("out", "alias") for kind, _ in st.bufs)
        stage_res.append(list(res[pos:pos + k]))
        pos += k
    return main_res, stage_res


def _run_stages(stages, name):
    return _call(lambda: None, name=name, args=[], in_specs=[], out_specs=[], out_shape=[], stages=stages)[1]


class _GatherRows(_Stage):
    n_sems = 8

    def __init__(self, g, lo=0, n=None):
        self.hr = g.shape[1] // 2
        self.lo, self.n = lo, (self.hr if n is None else n)
        self.n0 = self.n // 2 // 16 * 16
        self.bufs = [("alias", g)]

    def _copy(self, g_ref, send, recv, k, chip_xy, half, to, lo=0, n=None):
        n = self.n if n is None else n
        blk = g_ref.at[2 * chip_xy[0] + chip_xy[1], pl.ds(half * self.hr + self.lo + lo, n), :]
        return _remote(blk, blk, send, recv, self.base + k, to)

    def _plan(self, g, send, recv):
        x, y, c = _position()
        me, sib = (x, y, c), (x, y, 1 - c)
        cx, cy, cd = _other_chips(x, y)
        n0, n1 = self.n0, self.n - self.n0
        mine = [((x, y), c, (*cx, c)), ((x, y), c, (*cy, c)), (cx, c, sib), (cy, c, sib),
                (cx, c, (*cy, c), 0, n0), (cy, c, (*cx, c), n0, n1), (cd, c, sib, 0, n0), (cd, c, sib, n0, n1)]
        theirs = [(cx, c, me), (cy, c, me), (cx, 1 - c, me), (cy, 1 - c, me),
                  (cd, c, me, 0, n0), (cd, c, me, n0, n1), (cd, 1 - c, me, 0, n0), (cd, 1 - c, me, n0, n1)]
        return (lambda k: self._copy(g, send, recv, k, *mine[k])), (lambda k: self._copy(g, send, recv, k, *theirs[k]))

    def start(self, refs, send, recv):
        mine, _ = self._plan(refs[0], send, recv)
        mine(0).start()
        mine(1).start()

    def middle(self, refs, send, recv):
        mine, theirs = self._plan(refs[0], send, recv)
        theirs(0).wait_recv()
        mine(4).start()
        mine(2).start()
        theirs(1).wait_recv()
        mine(5).start()
        mine(3).start()

    def finish(self, refs, send, recv):
        mine, theirs = self._plan(refs[0], send, recv)
        theirs(4).wait_recv()
        mine(6).start()
        theirs(5).wait_recv()
        mine(7).start()
        for k in (2, 3, 6, 7):
            theirs(k).wait_recv()
        for k in range(self.n_sems):
            mine(k).wait_send()


class _GatherSmall(_Stage):
    n_sems = 8

    def __init__(self, block):
        self.m = block.shape[0]
        self.bufs = [("in", block), ("out", jax.ShapeDtypeStruct((N_DEV * self.m, block.shape[1]), block.dtype))]

    def _rows(self, out, px, py, pc):
        return out.at[pl.ds((4 * px + 2 * py + pc) * self.m, self.m), :]

    def _copy(self, refs, send, recv, k, blk, to, own=False):
        dst = self._rows(refs[1], *blk)
        return _remote(refs[0] if own else dst, dst, send, recv, self.base + k, to)

    def _local(self, refs, send):
        return pltpu.make_async_copy(refs[0], self._rows(refs[1], *_position()), send.at[self.base + 7])

    def start(self, refs, send, recv):
        x, y, c = _position()
        self._local(refs, send).start()
        self._copy(refs, send, recv, 0, (x, y, c), (x, y, 1 - c), own=True).start()
        for j, chip in enumerate(_other_chips(x, y)):
            self._copy(refs, send, recv, 1 + j, (x, y, c), (*chip, c), own=True).start()

    def finish(self, refs, send, recv):
        x, y, c = _position()
        me, sib, chips = (x, y, c), (x, y, 1 - c), _other_chips(x, y)
        for j, chip in enumerate(chips):
            self._copy(refs, send, recv, 1 + j, (*chip, c), me).wait_recv()
            self._copy(refs, send, recv, 4 + j, (*chip, c), sib).start()
        self._copy(refs, send, recv, 0, sib, me).wait_recv()
        for j, chip in enumerate(chips):
            self._copy(refs, send, recv, 4 + j, (*chip, 1 - c), me).wait_recv()
        self._copy(refs, send, recv, 0, me, sib, own=True).wait_send()
        for j, chip in enumerate(chips):
            self._copy(refs, send, recv, 1 + j, me, (*chip, c), own=True).wait_send()
            self._copy(refs, send, recv, 4 + j, (*chip, c), sib).wait_send()
        self._local(refs, send).wait()


class _PairExchange(_Stage):
    n_sems = 1

    def __init__(self, dw):
        S, R, C = dw.shape
        self.hr = R // 2
        self.bufs = [("in", dw), ("out", jax.ShapeDtypeStruct((S, self.hr, C), dw.dtype))]

    def _copy(self, refs, send, recv):
        x, y, c = _position()
        return _remote(refs[0].at[:, pl.ds((1 - c) * self.hr, self.hr), :], refs[1], send, recv, self.base,
                       (x, y, 1 - c))

    def start(self, refs, send, recv):
        self._copy(refs, send, recv).start()

    def finish(self, refs, send, recv):
        cp = self._copy(refs, send, recv)
        cp.wait_recv()
        cp.wait_send()


class _ChipExchange(_Stage):
    n_sems = 3

    def __init__(self, part, landed=None, lo=0, n=None):
        S, hr, C = part.shape
        self.lo, self.n = lo, (hr if n is None else n)
        self.bufs = [("in", part), ("out", jax.ShapeDtypeStruct((3, hr, C), part.dtype)) if landed is None
                     else ("alias", landed)]

    def _copies(self, refs, send, recv):
        x, y, c = _position()
        rows = pl.ds(self.lo, self.n)
        return [_remote(refs[0].at[2 * chip[0] + chip[1], rows, :], refs[1].at[j, rows, :], send, recv,
                        self.base + j, (*chip, c))
                for j, chip in enumerate(_other_chips(x, y))]

    def start(self, refs, send, recv):
        for cp in self._copies(refs, send, recv):
            cp.start()

    def finish(self, refs, send, recv):
        copies = self._copies(refs, send, recv)
        for cp in copies:
            cp.wait_recv()
        for cp in copies:
            cp.wait_send()


class _HalfExchange(_Stage):
    n_sems = 1

    def __init__(self, grad):
        self.hr = grad.shape[0] // 2
        self.bufs = [("alias", grad)]

    def start(self, refs, send, recv):
        x, y, c = _position()
        mine = refs[0].at[pl.ds(c * self.hr, self.hr), :]
        _remote(mine, mine, send, recv, self.base, (x, y, 1 - c)).start()

    def finish(self, refs, send, recv):
        x, y, c = _position()
        mine = refs[0].at[pl.ds(c * self.hr, self.hr), :]
        theirs = refs[0].at[pl.ds((1 - c) * self.hr, self.hr), :]
        _remote(theirs, theirs, send, recv, self.base, (x, y, 1 - c)).wait_recv()
        _remote(mine, mine, send, recv, self.base, (x, y, 1 - c)).wait_send()


def _matmul(a, b, *, mode, name, tm, tn, tk, out_dtypes, epilogue=None, extras=(), b_shard=None, out_shard=False,
            stages=(), n_sub=1, m_blocks=None, single_a=False):
    if mode == "tn":
        K, M = a.shape
    else:
        M, K = a.shape
    if b_shard == "n":
        S, Kb, Ns = b.shape
        N = S * Ns
    elif b_shard == "k":
        S, N, Ks = b.shape
        Kb = S * Ks
    elif mode == "nt":
        N, Kb = b.shape
    else:
        Kb, N = b.shape
    assert Kb == K, (name, a.shape, b.shape)
    tm, tn, tk = _tile(M, tm), _tile(N, tn), _tile(K, tk)
    if b_shard == "n" or out_shard:
        n_per = N // N_CHIPS
        tn = _tile(n_per, tn)
        njs = n_per // tn
    if b_shard == "k":
        tk = _tile(K // N_CHIPS, tk)
        nks = (K // N_CHIPS) // tk
    gm, gn, gk = M // tm, N // tn, K // tk
    if gk > 1 or tn % (n_sub * LANES):
        n_sub = 1
    i0 = 0
    if m_blocks is not None:
        i0, gm = m_blocks
        M = gm * tm

    if mode == "tn":
        a_spec = pl.BlockSpec((tk, tm), lambda i, j, k: (k, i + i0))
        dims = (((0,), (0,)), ((), ()))
    else:
        a_spec = pl.BlockSpec((tm, tk), lambda i, j, k: (i + i0, k))
        dims = (((1,), (1,)), ((), ())) if mode == "nt" else (((1,), (0,)), ((), ()))
    if b_shard == "n":
        b_spec = pl.BlockSpec((None, tk, tn), lambda i, j, k: (j // njs, k, j % njs))
    elif b_shard == "k":
        b_spec = pl.BlockSpec((None, tn, tk), lambda i, j, k: (k // nks, j, k % nks))
    elif mode == "nt":
        b_spec = pl.BlockSpec((tn, tk), lambda i, j, k: (j, k))
    else:
        b_spec = pl.BlockSpec((tk, tn), lambda i, j, k: (k, j))
    if single_a and gk == 1:
        a_spec = pl.BlockSpec(a_spec.block_shape, a_spec.index_map, pipeline_mode=pl.Buffered(1))
    mn_spec = pl.BlockSpec((tm, tn), lambda i, j, k: (i, j))
    if out_shard:
        out_spec = pl.BlockSpec((None, tm, tn), lambda i, j, k: (j // njs, i, j % njs))
        out_shape = [jax.ShapeDtypeStruct((N_CHIPS, M, N // N_CHIPS), dt) for dt in out_dtypes]
    else:
        out_spec = mn_spec
        out_shape = [jax.ShapeDtypeStruct((M, N), dt) for dt in out_dtypes]
    n_extra, n_out = len(extras), len(out_dtypes)

    def finish_tile(acc, extra_refs, out_refs):
        if epilogue is None:
            for o in out_refs:
                o[...] = acc.astype(o.dtype)
        else:
            epilogue(acc, extra_refs, out_refs)

    def body(*refs):
        a_ref, b_ref = refs[0], refs[1]
        extra_refs = refs[2:2 + n_extra]
        out_refs = refs[2 + n_extra:2 + n_extra + n_out]

        def product():
            return lax.dot_general(a_ref[...], b_ref[...], dims, preferred_element_type=F32)

        if gk == 1:
            sub = tn // n_sub
            for h in range(n_sub):
                cols = slice(h * sub, (h + 1) * sub)
                b_part = b_ref[cols, :] if mode == "nt" else b_ref[:, cols]
                acc = lax.dot_general(a_ref[...], b_part, dims, preferred_element_type=F32)
                finish_tile(acc, [e.at[:, cols] for e in extra_refs], [o.at[:, cols] for o in out_refs])
            return
        acc_ref = refs[-1]
        k = pl.program_id(2)

        @pl.when(k == 0)
        def _():
            acc_ref[...] = product()

        @pl.when((k > 0) & (k < gk - 1))
        def _():
            acc_ref[...] += product()

        @pl.when(k == gk - 1)
        def _():
            finish_tile(acc_ref[...] + product(), extra_refs, out_refs)

    outs, carried = _call(
        body, name=name, args=[a, b, *extras], grid=(gm, gn, gk),
        in_specs=[a_spec, b_spec] + [mn_spec] * n_extra, out_specs=[out_spec] * n_out, out_shape=out_shape,
        scratch_shapes=[pltpu.VMEM((tm, tn), F32)] if gk > 1 else [],
        semantics=("parallel", "parallel", "arbitrary"), stages=stages)
    return (outs[0] if n_out == 1 else outs), carried


def _ep_residual(acc, extra_refs, out_refs):
    out_refs[0][...] = extra_refs[0][...] + acc


def _ep_relu2(acc, extra_refs, out_refs):
    r = jnp.maximum(acc, 0.0)
    out_refs[0][...] = r.astype(BF16)
    out_refs[1][...] = (r * r).astype(BF16)


def _ep_relu2_bwd(acc, extra_refs, out_refs):
    out_refs[0][...] = (acc * (2.0 * extra_refs[0][...].astype(F32))).astype(BF16)


def _row_inv(x):
    return lax.rsqrt(jnp.mean(x * x, axis=-1, keepdims=True) + NORM_EPS)


def _rmsnorm_fwd(x, g, name, stages=()):
    T, D = x.shape
    tt = _tile(T, 256, SUBLANES)

    def body(x_ref, g_ref, o_ref):
        xv = x_ref[...]
        o_ref[...] = (xv * _row_inv(xv) * g_ref[...]).astype(BF16)

    outs, carried = _call(
        body, name=name, args=[x, g], grid=(T // tt,),
        in_specs=[pl.BlockSpec((tt, D), lambda i: (i, 0)), pl.BlockSpec((1, D), lambda i: (0, 0))],
        out_specs=[pl.BlockSpec((tt, D), lambda i: (i, 0))], out_shape=[jax.ShapeDtypeStruct((T, D), BF16)],
        semantics=("parallel",), stages=stages)
    return outs[0], carried


def _rmsnorm_fwd_and_casts(x, g, weights, chip, name, stages=()):
    T, D = x.shape
    tt = _tile(T, 256, SUBLANES)
    n, nw = T // tt, len(weights)
    rows = [w.shape[0] // n for w in weights]
    assert all(r % 16 == 0 and r * n == w.shape[0] for r, w in zip(rows, weights))

    def body(chip_ref, x_ref, g_ref, *refs):
        w_refs, o_ref, slot_refs = refs[:nw], refs[nw], refs[nw + 1:]
        xv = x_ref[...]
        o_ref[...] = (xv * _row_inv(xv) * g_ref[...]).astype(BF16)
        for w_ref, s_ref in zip(w_refs, slot_refs):
            s_ref[...] = w_ref[...].astype(BF16)

    outs, carried = _call(
        body, name=name, args=[x, g, *weights], grid=(n,), prefetch=chip,
        in_specs=[pl.BlockSpec((tt, D), lambda i, chip_ref: (i, 0)), pl.BlockSpec((1, D), lambda i, chip_ref: (0, 0))]
        + [pl.BlockSpec((r, w.shape[1]), lambda i, chip_ref: (i, 0)) for r, w in zip(rows, weights)],
        out_specs=[pl.BlockSpec((tt, D), lambda i, chip_ref: (i, 0))]
        + [pl.BlockSpec((None, r, w.shape[1]), lambda i, chip_ref: (chip_ref[0], i, 0)) for r, w in zip(rows, weights)],
        out_shape=[jax.ShapeDtypeStruct((T, D), BF16)]
        + [jax.ShapeDtypeStruct((N_CHIPS, *w.shape), BF16) for w in weights],
        semantics=("parallel",), stages=stages)
    return outs[0], outs[1:], carried


def _rmsnorm_bwd(dxn, h, g, dres, name, stages=()):
    T, D = h.shape
    tt = _tile(T, 128, SUBLANES)

    def body(dxn_ref, h_ref, g_ref, dres_ref, dh_ref, dhb_ref, dg_ref):
        @pl.when(pl.program_id(0) == 0)
        def _():
            dg_ref[...] = jnp.zeros_like(dg_ref)

        hv = h_ref[...]
        inv = _row_inv(hv)
        n = hv * inv
        d = dxn_ref[...]
        dg_ref[...] += jnp.sum(d * n, axis=0, keepdims=True)
        dn = d * g_ref[...]
        dh = dres_ref[...] + inv * (dn - n * jnp.mean(dn * n, axis=-1, keepdims=True))
        dh_ref[...] = dh
        dhb_ref[...] = dh.astype(BF16)

    row = pl.BlockSpec((tt, D), lambda i: (i, 0))
    vec = pl.BlockSpec((1, D), lambda i: (0, 0))
    return _call(
        body, name=name, args=[dxn, h, g, dres], grid=(T // tt,), in_specs=[row, row, vec, row],
        out_specs=[row, row, vec],
        out_shape=[jax.ShapeDtypeStruct((T, D), F32), jax.ShapeDtypeStruct((T, D), BF16),
                   jax.ShapeDtypeStruct((1, D), F32)],
        semantics=("arbitrary",), stages=stages)


def _loss_and_final_norm_bwd(h1, d2, tgt, g, name):
    T, D = h1.shape
    tt = _tile(T, 128, SUBLANES)

    def body(h1_ref, d2_ref, t_ref, g_ref, dh_ref, dhb_ref, dg_ref, loss_ref):
        @pl.when(pl.program_id(0) == 0)
        def _():
            dg_ref[...] = jnp.zeros_like(dg_ref)
            loss_ref[...] = jnp.zeros_like(loss_ref)

        hv = h1_ref[...] + d2_ref[...]
        gv = g_ref[...]
        inv = _row_inv(hv)
        n = hv * inv
        err = n * gv - t_ref[...]
        loss_ref[...] += 0.5 * jnp.sum(jnp.mean(err * err, axis=-1, keepdims=True))
        dy = err * (1.0 / D)
        dg_ref[...] += jnp.sum(dy * n, axis=0, keepdims=True)
        dn = dy * gv
        dh = inv * (dn - n * jnp.mean(dn * n, axis=-1, keepdims=True))
        dh_ref[...] = dh
        dhb_ref[...] = dh.astype(BF16)

    row = pl.BlockSpec((tt, D), lambda i: (i, 0))
    vec = pl.BlockSpec((1, D), lambda i: (0, 0))
    one = pl.BlockSpec((1, LANES), lambda i: (0, 0))
    return _call(
        body, name=name, args=[h1, d2, tgt, g], grid=(T // tt,), in_specs=[row, row, row, vec],
        out_specs=[row, row, vec, one],
        out_shape=[jax.ShapeDtypeStruct((T, D), F32), jax.ShapeDtypeStruct((T, D), BF16),
                   jax.ShapeDtypeStruct((1, D), F32), jax.ShapeDtypeStruct((1, LANES), F32)],
        semantics=("arbitrary",))[0]


def _gelu(x):
    th = jnp.tanh(GELU_K * (x + GELU_A * (x * x * x)))
    return 0.5 * x * (1.0 + th), th


def _gelu_grad(x, th):
    return 0.5 * (1.0 + th) + 0.5 * x * (1.0 - th * th) * (GELU_K * (1.0 + 3.0 * GELU_A * (x * x)))


def _shift_rows(cur, prev_rows, k):
    rolled = pltpu.roll(cur, k, 0)
    row = lax.broadcasted_iota(jnp.int32, cur.shape, 0)
    out = rolled
    for r in range(k):
        out = jnp.where(row == r, prev_rows[SUBLANES - k + r:SUBLANES - k + r + 1, :], out)
    return out


def _unshift_rows(cur, next_rows, k):
    n = cur.shape[0]
    rolled = pltpu.roll(cur, n - k, 0)
    row = lax.broadcasted_iota(jnp.int32, cur.shape, 0)
    out = rolled
    for r in range(k):
        out = jnp.where(row == n - k + r, next_rows[r:r + 1, :], out)
    return out


def _mixer_specs(W, blk, halo):
    cols = [pl.BlockSpec((CHUNK, W), functools.partial(lambda i, col: (blk(i), col), col=col)) for col in range(5)]
    halos = [pl.BlockSpec((SUBLANES, W), functools.partial(lambda i, col: (halo(i), col), col=col)) for col in (1, 2)]
    return cols, halos


def _mixers_fwd(proj, conv_w, wm, bias_e, g_a, g_b, seq_len, name, stages=()):
    T, W5 = proj.shape
    W = W5 // 5
    H = W // HEAD_DIM
    per_seq = seq_len // CHUNK
    rb = CHUNK // SUBLANES
    cols, halos = _mixer_specs(W, lambda i: i, lambda i: jnp.maximum(i * rb - 1, 0))

    def body(b_ref, c_ref, hin_ref, u_ref, v_ref, ch_ref, hh_ref, cw_ref, wm_ref, be_ref, ga_ref, gb_ref, y_ref, s_ref):
        first = (pl.program_id(0) % per_seq) == 0
        hc = c_ref[...] * hin_ref[...]
        hc_prev = jnp.where(first, 0.0, ch_ref[...] * hh_ref[...])
        cw = cw_ref[...]
        ya = b_ref[...] * (cw[0:1, :] * _shift_rows(hc, hc_prev, 2) + cw[1:2, :] * _shift_rows(hc, hc_prev, 1)
                           + cw[2:3, :] * hc)
        y_ref[:, 0:W] = (ya * _row_inv(ya) * ga_ref[...]).astype(BF16)
        gu, _ = _gelu(u_ref[...])
        gv, _ = _gelu(v_ref[...])
        gvb = gv.astype(BF16)
        for hd in range(H):
            sl = slice(hd * HEAD_DIM, (hd + 1) * HEAD_DIM)
            s_ref[:, sl] = jnp.dot(wm_ref[hd], gvb[:, sl], preferred_element_type=F32)
        yb = gu * (s_ref[...] + be_ref[...])
        y_ref[:, W:2 * W] = (yb * _row_inv(yb) * gb_ref[...]).astype(BF16)

    full = lambda shape: pl.BlockSpec(shape, lambda i: (0,) * len(shape))
    outs, carried = _call(
        body, name=name, args=[proj, proj, proj, proj, proj, proj, proj, conv_w, wm, bias_e, g_a, g_b],
        grid=(T // CHUNK,),
        in_specs=cols + halos + [full((CONV_K, W)), full((H, CHUNK, CHUNK)), full((CHUNK, W)), full((1, W)), full((1, W))],
        out_specs=[pl.BlockSpec((CHUNK, 2 * W), lambda i: (i, 0))], out_shape=[jax.ShapeDtypeStruct((T, 2 * W), BF16)],
        scratch_shapes=[pltpu.VMEM((CHUNK, W), F32)], semantics=("parallel",), stages=stages)
    return outs[0], carried


def _mixers_bwd(dy, proj, conv_w, wm, wmt, bias_e, g_a, g_b, head_onehot, seq_len, name, stages=()):
    T, W5 = proj.shape
    W = W5 // 5
    H = W // HEAD_DIM
    nb = T // CHUNK
    per_seq = seq_len // CHUNK
    rb = CHUNK // SUBLANES
    blk = lambda i: nb - 1 - i
    cols, halos = _mixer_specs(W, blk, lambda i: jnp.maximum(blk(i) * rb - 1, 0))

    def body(dy_ref, b_ref, c_ref, hin_ref, u_ref, v_ref, ch_ref, hh_ref, cw_ref, wm_ref, wmt_ref, be_ref, ga_ref,
             gb_ref, oh_ref, dp_ref, dcw_ref, dga_ref, dgb_ref, dws_ref, dbt_ref, carry_ref, s_ref, dgv_ref):
        i = pl.program_id(0)
        j = nb - 1 - i

        @pl.when(i == 0)
        def _():
            for r in (dcw_ref, dga_ref, dgb_ref, dws_ref, dbt_ref, carry_ref):
                r[...] = jnp.zeros_like(r)

        first = (j % per_seq) == 0
        last = (j % per_seq) == per_seq - 1
        b, c, hin = b_ref[...], c_ref[...], hin_ref[...]
        cw = cw_ref[...]
        hc = c * hin
        hc_prev = jnp.where(first, 0.0, ch_ref[...] * hh_ref[...])
        hc1 = _shift_rows(hc, hc_prev, 1)
        hc2 = _shift_rows(hc, hc_prev, 2)
        conv = cw[0:1, :] * hc2 + cw[1:2, :] * hc1 + cw[2:3, :] * hc
        ya = b * conv
        inv_a = _row_inv(ya)
        na = ya * inv_a
        do_a = dy_ref[:, 0:W]
        dga_ref[...] += jnp.sum(do_a * na, axis=0, keepdims=True)
        dna = do_a * ga_ref[...]
        dya = inv_a * (dna - na * jnp.mean(dna * na, axis=-1, keepdims=True))
        dp_ref[:, 0:W] = (dya * conv).astype(BF16)
        dconv = dya * b
        dcw_ref[0:1, :] += jnp.sum(dconv * hc2, axis=0, keepdims=True)
        dcw_ref[1:2, :] += jnp.sum(dconv * hc1, axis=0, keepdims=True)
        dcw_ref[2:3, :] += jnp.sum(dconv * hc, axis=0, keepdims=True)
        nxt = jnp.where(last, 0.0, carry_ref[...])
        dhc = cw[2:3, :] * dconv + cw[1:2, :] * _unshift_rows(dconv, nxt, 1) + cw[0:1, :] * _unshift_rows(dconv, nxt, 2)
        carry_ref[...] = dconv[0:SUBLANES, :]
        dp_ref[:, W:2 * W] = (dhc * hin).astype(BF16)
        dp_ref[:, 2 * W:3 * W] = (dhc * c).astype(BF16)
        u, v = u_ref[...], v_ref[...]
        gu, thu = _gelu(u)
        gv, thv = _gelu(v)
        gvb = gv.astype(BF16)
        for hd in range(H):
            sl = slice(hd * HEAD_DIM, (hd + 1) * HEAD_DIM)
            s_ref[:, sl] = jnp.dot(wm_ref[hd], gvb[:, sl], preferred_element_type=F32)
        s = s_ref[...] + be_ref[...]
        yb = gu * s
        inv_b = _row_inv(yb)
        nbv = yb * inv_b
        do_b = dy_ref[:, W:2 * W]
        dgb_ref[...] += jnp.sum(do_b * nbv, axis=0, keepdims=True)
        dnb = do_b * gb_ref[...]
        dyb = inv_b * (dnb - nbv * jnp.mean(dnb * nbv, axis=-1, keepdims=True))
        dp_ref[:, 3 * W:4 * W] = (dyb * s * _gelu_grad(u, thu)).astype(BF16)
        dsb = (dyb * gu).astype(BF16)
        dbt_ref[...] += jnp.dot(dsb, oh_ref[...], preferred_element_type=F32)
        for hd in range(H):
            sl = slice(hd * HEAD_DIM, (hd + 1) * HEAD_DIM)
            dws_ref[hd] += lax.dot_general(dsb[:, sl], gvb[:, sl], (((1,), (1,)), ((), ())), preferred_element_type=F32)
            dgv_ref[:, sl] = jnp.dot(wmt_ref[hd], dsb[:, sl], preferred_element_type=F32)
        dp_ref[:, 4 * W:5 * W] = (dgv_ref[...] * _gelu_grad(v, thv)).astype(BF16)

    full = lambda shape: pl.BlockSpec(shape, lambda i: (0,) * len(shape))
    return _call(
        body, name=name, grid=(nb,),
        args=[dy, proj, proj, proj, proj, proj, proj, proj, conv_w, wm, wmt, bias_e, g_a, g_b, head_onehot],
        in_specs=[pl.BlockSpec((CHUNK, 2 * W), lambda i: (blk(i), 0))] + cols + halos
        + [full((CONV_K, W)), full((H, CHUNK, CHUNK)), full((H, CHUNK, CHUNK)), full((CHUNK, W)), full((1, W)),
           full((1, W)), full((W, LANES))],
        out_specs=[pl.BlockSpec((CHUNK, 5 * W), lambda i: (blk(i), 0)), full((SUBLANES, W)), full((1, W)), full((1, W)),
                   full((H, CHUNK, CHUNK)), full((CHUNK, LANES))],
        out_shape=[jax.ShapeDtypeStruct((T, 5 * W), BF16), jax.ShapeDtypeStruct((SUBLANES, W), F32),
                   jax.ShapeDtypeStruct((1, W), F32), jax.ShapeDtypeStruct((1, W), F32),
                   jax.ShapeDtypeStruct((H, CHUNK, CHUNK), F32), jax.ShapeDtypeStruct((CHUNK, LANES), F32)],
        scratch_shapes=[pltpu.VMEM((SUBLANES, W), F32), pltpu.VMEM((CHUNK, W), F32), pltpu.VMEM((CHUNK, W), F32)],
        semantics=("arbitrary",), stages=stages)


def _cast_into_slot(w, chip, name):
    R, C = w.shape
    tr = _tile(R, 256, 16)

    def body(chip_ref, w_ref, o_ref):
        o_ref[...] = w_ref[...].astype(BF16)

    return pl.pallas_call(
        body, name=name,
        grid_spec=pltpu.PrefetchScalarGridSpec(
            num_scalar_prefetch=1, grid=(R // tr,),
            in_specs=[pl.BlockSpec((tr, C), lambda i, chip_ref: (i, 0))],
            out_specs=pl.BlockSpec((None, tr, C), lambda i, chip_ref: (chip_ref[0], i, 0))),
        out_shape=jax.ShapeDtypeStruct((N_CHIPS, R, C), BF16),
        compiler_params=_params(("parallel",)),
    )(chip, w)


def _rs_pair_add(dw, got, chip_core, name):
    S, R, C = dw.shape
    hr = R // 2
    tr = _tile(hr, 256, 16)
    nrb = hr // tr

    def body(cc_ref, dw_ref, got_ref, send_ref, own_ref):
        s = dw_ref[...].astype(F32) + got_ref[...].astype(F32)
        send_ref[...] = s.astype(BF16)

        @pl.when(pl.program_id(1) == cc_ref[0])
        def _():
            own_ref[...] = s

    return pl.pallas_call(
        body, name=name,
        grid_spec=pltpu.PrefetchScalarGridSpec(
            num_scalar_prefetch=1, grid=(nrb, S),
            in_specs=[pl.BlockSpec((None, tr, C), lambda i, q, cc: (q, cc[1] * nrb + i, 0)),
                      pl.BlockSpec((None, tr, C), lambda i, q, cc: (q, i, 0))],
            out_specs=[pl.BlockSpec((None, tr, C), lambda i, q, cc: (q, i, 0)),
                       pl.BlockSpec((tr, C), lambda i, q, cc: (i, 0))]),
        out_shape=[jax.ShapeDtypeStruct((S, hr, C), BF16), jax.ShapeDtypeStruct((hr, C), F32)],
        compiler_params=_params(("parallel", "arbitrary")),
    )(chip_core, dw, got)


def _rs_final_add(own, got, chip_core, name):
    hr, C = own.shape
    tr = _tile(hr, 256, 16)
    nrb = hr // tr

    def body(cc_ref, own_ref, got_ref, o_ref):
        o_ref[...] = ((own_ref[...] + got_ref[0].astype(F32)) + got_ref[1].astype(F32)) + got_ref[2].astype(F32)

    return pl.pallas_call(
        body, name=name,
        grid_spec=pltpu.PrefetchScalarGridSpec(
            num_scalar_prefetch=1, grid=(nrb,),
            in_specs=[pl.BlockSpec((tr, C), lambda i, cc: (i, 0)), pl.BlockSpec((3, tr, C), lambda i, cc: (0, i, 0))],
            out_specs=pl.BlockSpec((tr, C), lambda i, cc: (cc[1] * nrb + i, 0))),
        out_shape=jax.ShapeDtypeStruct((2 * hr, C), F32),
        compiler_params=_params(("parallel",)),
    )(chip_core, own, got)


def _adamw_math(w, g, m, v):
    m2 = ADAM_B1 * m + (1.0 - ADAM_B1) * g
    v2 = ADAM_B2 * v + (1.0 - ADAM_B2) * (g * g)
    delta = -ADAM_LR * ((m2 / ADAM_C1) / (jnp.sqrt(v2 / ADAM_C2) + ADAM_EPS) + ADAM_WD * w)
    return delta, m2, v2


def _adamw(w, g, m, v, name):
    R, C = w.shape
    parts = list(g) if isinstance(g, (list, tuple)) else [g]
    n_parts = len(parts)
    tr = _tile(R // n_parts, max(SUBLANES, (256 * 1024) // C), SUBLANES)
    per = R // n_parts // tr

    def body(w_ref, m_ref, v_ref, *refs):
        g_refs, (g2_ref, d_ref, m2_ref, v2_ref) = refs[:n_parts], refs[n_parts:]
        g = g_refs[0][...]
        for p in range(1, n_parts):
            g = jnp.where(pl.program_id(0) >= p * per, g_refs[p][...], g)
        g2_ref[...] = g
        d_ref[...], m2_ref[...], v2_ref[...] = _adamw_math(w_ref[...], g, m_ref[...], v_ref[...])

    blk = pl.BlockSpec((tr, C), lambda i: (i, 0))
    g_specs = [pl.BlockSpec((tr, C), functools.partial(lambda i, p: (jnp.clip(i - p * per, 0, per - 1), 0), p=p))
               for p in range(n_parts)]
    return _call(body, name=name, args=[w, m, v, *parts], grid=(R // tr,), in_specs=[blk] * 3 + g_specs,
                 out_specs=[blk] * 4, out_shape=[jax.ShapeDtypeStruct((R, C), F32)] * 4,
                 semantics=("arbitrary",))[0]


def _adamw_sc(w, g, m, v, name):
    R, C = w.shape
    blk = (SUBLANES, 512)
    assert R % blk[0] == 0 and C % blk[1] == 0
    mesh = plsc.VectorSubcoreMesh(core_axis_name="sc_core", subcore_axis_name="sc_tile", num_cores=1)
    spec = pl.BlockSpec(block_shape=blk, index_map=lambda i, j: (i, j))

    def kern(w_hbm, g_hbm, m_hbm, v_hbm, g2_hbm, d_hbm, m2_hbm, v2_hbm):
        def body(w_v, g_v, m_v, v_v, g2_v, d_v, m2_v, v2_v):
            for r in range(blk[0]):
                @plsc.parallel_loop(0, blk[1], SC_LANES, unroll=8)
                def _(c):
                    at = (pl.ds(r, 1), pl.ds(c, SC_LANES))
                    gv = g_v.at[*at][...]
                    g2_v.at[*at][...] = gv
                    d_v.at[*at][...], m2_v.at[*at][...], v2_v.at[*at][...] = _adamw_math(
                        w_v.at[*at][...], gv, m_v.at[*at][...], v_v.at[*at][...])

        pltpu.emit_pipeline(
            body, grid=(R // blk[0], C // blk[1]), in_specs=[spec] * 4, out_specs=[spec] * 4,
            core_axis_name=("sc_core", "sc_tile"), dimension_semantics=(pltpu.PARALLEL, pltpu.PARALLEL),
        )(w_hbm, g_hbm, m_hbm, v_hbm, g2_hbm, d_hbm, m2_hbm, v2_hbm)

    return pl.kernel(kern, name=name, out_type=[jax.ShapeDtypeStruct((R, C), F32)] * 4, mesh=mesh,
                     scratch_types=[])(w, g, m, v)


def _all_gather_small(block, name):
    m_per, n = block.shape

    def body(x_ref, out_ref, send_sems, recv_sems, local_sem):
        x, y, c = _position()
        me, sibling = (x, y, c), (x, y, 1 - c)
        chips = _other_chips(x, y)

        def rows(px, py, pc):
            return out_ref.at[pl.ds((4 * px + 2 * py + pc) * m_per, m_per), :]

        def copy(k, blk, to, src=None):
            return pltpu.make_async_remote_copy(src_ref=rows(*blk) if src is None else src, dst_ref=rows(*blk),
                                                send_sem=send_sems.at[k], recv_sem=recv_sems.at[k], device_id=to,
                                                device_id_type=MESH)

        mine = pltpu.make_async_copy(x_ref, rows(*me), local_sem)
        mine.start()
        first = [copy(0, me, sibling, src=x_ref)]
        first += [copy(1 + j, me, (*chip, c), src=x_ref) for j, chip in enumerate(chips)]
        for cp in first:
            cp.start()
        passed = [copy(4 + j, (*chip, c), sibling) for j, chip in enumerate(chips)]
        for j, chip in enumerate(chips):
            copy(1 + j, (*chip, c), me).wait_recv()
            passed[j].start()
        copy(0, sibling, me).wait_recv()
        for j, chip in enumerate(chips):
            copy(4 + j, (*chip, 1 - c), me).wait_recv()
        for cp in first + passed:
            cp.wait_send()
        mine.wait()

    return pl.pallas_call(
        body, name=name,
        in_specs=[pl.BlockSpec(memory_space=pltpu.VMEM)],
        out_specs=pl.BlockSpec(memory_space=pltpu.VMEM),
        out_shape=jax.ShapeDtypeStruct((N_DEV * m_per, n), block.dtype),
        scratch_shapes=[pltpu.SemaphoreType.DMA((7,)), pltpu.SemaphoreType.DMA((7,)), pltpu.SemaphoreType.DMA],
        compiler_params=pltpu.CompilerParams(vmem_limit_bytes=VMEM_LIMIT_V7X, has_side_effects=True),
    )(block)


def _sum_and_adamw_small(gathered, w, m, v, name):
    rows, n = w.shape
    tr = _tile(rows, 32, SUBLANES)

    def body(p_ref, w_ref, m_ref, v_ref, g_ref, d_ref, m2_ref, v2_ref):
        g = p_ref[0]
        for d in range(1, N_DEV):
            g = g + p_ref[d]
        g_ref[...] = g
        d_ref[...], m2_ref[...], v2_ref[...] = _adamw_math(w_ref[...], g, m_ref[...], v_ref[...])

    blk = pl.BlockSpec((tr, n), lambda i: (i, 0))
    return pl.pallas_call(
        body, name=name, grid=(rows // tr,),
        in_specs=[pl.BlockSpec((N_DEV, tr, n), lambda i: (0, i, 0))] + [blk] * 3,
        out_specs=[blk] * 4,
        out_shape=[jax.ShapeDtypeStruct((rows, n), F32)] * 4,
        compiler_params=_params(("parallel",)),
    )(gathered.reshape(N_DEV, rows, n), w, m, v)


def _pad_rows(a):
    pad = (-a.shape[0]) % SUBLANES
    return jnp.pad(a, ((0, pad), (0, 0))) if pad else a


class _SmallPack:
    def __init__(self, W, D, H, chip):
        self.W, self.D, self.H, self.chip = W, D, H, chip
        self.offsets = {}
        self.rows = 0

    def pack(self, pieces):
        out = []
        self.offsets, self.rows = {}, 0
        for name, a in pieces:
            a = _pad_rows(a.astype(F32))
            self.offsets[name] = (self.rows, a.shape[0])
            self.rows += a.shape[0]
            out.append(a)
        return jnp.concatenate(out, axis=0)

    def piece(self, packed, name):
        start, n = self.offsets[name]
        return packed[start:start + n]


def _bias_rows(b, W):
    bt = jnp.pad(b.T, ((0, 0), (0, LANES - b.shape[0])))
    return bt.reshape(-1, W)


def _bias_from_rows(rows, H):
    return rows.reshape(-1)[:CHUNK * LANES].reshape(CHUNK, LANES)[:, :H].T


def kernel(x, mix_norm_g, w_in, conv_w, spatial_w, spatial_b, conv_out_norm_g, gmlp_out_norm_g, w_out, mlp_norm_g, w_up, w_down, final_norm_g, loss_target, m_mix_norm_g, m_w_in, m_conv_w, m_spatial_w, m_spatial_b, m_conv_out_norm_g, m_gmlp_out_norm_g, m_w_out, m_mlp_norm_g, m_w_up, m_w_down, m_final_norm_g, v_mix_norm_g, v_w_in, v_conv_w, v_spatial_w, v_spatial_b, v_conv_out_norm_g, v_gmlp_out_norm_g, v_w_out, v_mlp_norm_g, v_w_up, v_w_down, v_final_norm_g):
    Bl, S, D = x.shape
    T = Bl * S
    W = conv_out_norm_g.shape[-1]
    H = W // HEAD_DIM
    Wl = conv_w.shape[-1]
    xi, yi, ci = _position()
    chip = (2 * xi + yi).astype(jnp.int32)
    chip_arr = chip.reshape(1)
    chip_core = jnp.stack([chip, ci.astype(jnp.int32)])

    x2 = x.reshape(T, D)
    tgt2 = loss_target.reshape(T, D)

    s_in = _cast_into_slot(w_in[0], chip_arr, "cast_w_in")
    up_rows = w_up.shape[1] // 2
    up_cuts = [0] + [up_rows * pct // 100 // 16 * 16 for pct in (43, 57, 82)] + [up_rows]
    up_part = [(lo, hi - lo) for lo, hi in zip(up_cuts[:-1], up_cuts[1:])]

    causal = jnp.tril(jnp.ones((CHUNK, CHUNK), dtype=bool))
    wm = jnp.where(causal[None], spatial_w[0], 0.0).astype(BF16)
    wmt = jnp.swapaxes(wm, 1, 2)
    bias_e = jnp.repeat(spatial_b[0].T, HEAD_DIM, axis=1)
    conv_full = lax.dynamic_update_slice(jnp.zeros((CONV_K, W), F32), conv_w[0], (0, chip * Wl))
    head_onehot = (jnp.arange(W)[:, None] // HEAD_DIM == jnp.arange(LANES)[None, :]).astype(BF16)
    g_a, g_b = conv_out_norm_g, gmlp_out_norm_g

    xn, (s_out, s_up, s_down), ((g_in,), (conv_gathered,)) = _rmsnorm_fwd_and_casts(
        x2, mix_norm_g, [w_out[0], w_up[0], w_down[0]], chip_arr, "mix_norm_fwd",
        stages=[_GatherRows(s_in), _GatherSmall(_pad_rows(conv_full))])
    conv_w_all = conv_gathered.reshape(N_DEV, SUBLANES, W)[:, :CONV_K]
    conv_w_all = conv_w_all[0] + conv_w_all[2] + conv_w_all[4] + conv_w_all[6]
    proj, ((g_out,), (g_up,)) = _matmul(xn, g_in, mode="nn", name="proj_fwd", tm=1024, tn=512, tk=4096,
                                        out_dtypes=[F32], b_shard="n",
                                        stages=[_GatherRows(s_out), _GatherRows(s_up, *up_part[0])])
    g_out = g_out.reshape(-1, D)
    y, ((g_up,),) = _mixers_fwd(proj, conv_w_all, wm, bias_e, g_a, g_b, S, "mixers_fwd",
                                stages=[_GatherRows(g_up, *up_part[1])])
    h1, ((g_up,),) = _matmul(y, g_out, mode="nn", name="out_proj_fwd", tm=1024, tn=512, tk=4096, out_dtypes=[F32],
                             epilogue=_ep_residual, extras=(x2,), stages=[_GatherRows(g_up, *up_part[2])])
    xn2, ((g_up,),) = _rmsnorm_fwd(h1, mlp_norm_g, "mlp_norm_fwd", stages=[_GatherRows(g_up, *up_part[3])])
    (r, a), ((g_down,),) = _matmul(xn2, g_up, mode="nn", name="up_fwd", tm=1024, tn=1024, tk=4096, n_sub=2,
                                   out_dtypes=[BF16, BF16], epilogue=_ep_relu2, b_shard="n",
                                   stages=[_GatherRows(s_down)])
    g_down = g_down.reshape(-1, D)
    d2, _ = _matmul(a, g_down, mode="nn", name="down_fwd", tm=2048, tn=1024, tk=1024, out_dtypes=[F32])
    dh2, dh2b, d_final_g, loss_part = _loss_and_final_norm_bwd(h1, d2, tgt2, final_norm_g.reshape(1, D),
                                                               "loss_final_norm")

    def rs_adds(dw, got, tag):
        return _rs_pair_add(dw, got, chip_core, f"rs_pair_add_{tag}")

    dw_down, _ = _matmul(a, dh2b, mode="tn", name="down_dw", tm=1024, tn=1024, tk=4096, n_sub=2, out_dtypes=[BF16])
    dw_down = dw_down.reshape(N_CHIPS, -1, D)
    dpre, ((got_down,),) = _matmul(dh2b, g_down, mode="nt", name="down_dx", tm=1024, tn=1024, tk=4096, n_sub=2,
                                   out_dtypes=[BF16], epilogue=_ep_relu2_bwd, extras=(r,),
                                   stages=[_PairExchange(dw_down)])
    part_down, own_down = rs_adds(dw_down, got_down, "w_down")
    rows_down = part_down.shape[1]
    down_a = rows_down * 3 // 4 // 16 * 16
    dw_up, ((landed_down,),) = _matmul(xn2, dpre, mode="tn", name="up_dw", tm=1024, tn=1024, tk=4096, n_sub=2,
                                       out_dtypes=[BF16], out_shard=True,
                                       stages=[_ChipExchange(part_down, None, 0, down_a)])
    dxn2, ((got_up,), (landed_down,)) = _matmul(
        dpre, g_up, mode="nt", name="up_dx", tm=2048, tn=1024, tk=1024, out_dtypes=[F32], b_shard="k",
        stages=[_PairExchange(dw_up), _ChipExchange(part_down, landed_down, down_a, rows_down - down_a)])
    half_down = _rs_final_add(own_down, landed_down, chip_core, "rs_final_add_w_down")
    part_up, own_up = rs_adds(dw_up, got_up, "w_up")
    rows_up = part_up.shape[1]
    up_cut = [0] + [rows_up * pct // 100 // 16 * 16 for pct in (12, 32, 57, 83)] + [rows_up]
    up_rs = [(lo, hi - lo) for lo, hi in zip(up_cut[:-1], up_cut[1:])]
    (dh1, dh1b, d_mlp_g), ((grad_down,), (landed_up,)) = _rmsnorm_bwd(
        dxn2, h1, mlp_norm_g, dh2, "mlp_norm_bwd",
        stages=[_HalfExchange(half_down), _ChipExchange(part_up, None, *up_rs[0])])
    dy, ((landed_up,),) = _matmul(dh1b, g_out, mode="nt", name="out_proj_dx", tm=1024, tn=1024, tk=4096, n_sub=2,
                                  single_a=True, out_dtypes=[F32], stages=[_ChipExchange(part_up, landed_up, *up_rs[1])])
    (dproj, d_conv, d_ga, d_gb, d_ws, d_bt), ((landed_up,),) = _mixers_bwd(
        dy, proj, conv_w_all, wm, wmt, bias_e, g_a, g_b, head_onehot, S, "mixers_bwd",
        stages=[_ChipExchange(part_up, landed_up, *up_rs[2])])
    pack = _SmallPack(W, D, H, chip)
    loss_row = jnp.pad(loss_part[:, :1], ((0, 0), (0, W - 1)))
    g_part = pack.pack([("spatial_w", (d_ws * causal.astype(F32)[None]).reshape(-1, W)), ("conv_w", d_conv),
                        ("mlp_norm_g", d_mlp_g.reshape(-1, W)), ("final_norm_g", d_final_g.reshape(-1, W)),
                        ("conv_out_norm_g", d_ga), ("gmlp_out_norm_g", d_gb), ("spatial_b", d_bt.reshape(-1, W)),
                        ("loss", loss_row)])
    tm_in = min(1024, D // 2)
    nb_in = D // tm_in // 2
    dw_in_a, ((landed_up,),) = _matmul(
        xn, dproj, mode="tn", name="proj_dw_a", tm=tm_in, tn=512, tk=4096, out_dtypes=[BF16], out_shard=True,
        m_blocks=(0, nb_in), stages=[_ChipExchange(part_up, landed_up, *up_rs[3])])
    dw_in_b, ((landed_up,), (got_in_a,), (g_all,)) = _matmul(
        xn, dproj, mode="tn", name="proj_dw_b", tm=tm_in, tn=512, tk=4096, out_dtypes=[BF16], out_shard=True,
        m_blocks=(nb_in, nb_in),
        stages=[_ChipExchange(part_up, landed_up, *up_rs[4]), _PairExchange(dw_in_a), _GatherSmall(g_part)])
    half_up = _rs_final_add(own_up, landed_up, chip_core, "rs_final_add_w_up")
    part_in_a, own_in_a = rs_adds(dw_in_a, got_in_a, "w_in_a")
    rows_in = part_in_a.shape[1]
    in_a = rows_in * 7 // 10 // 16 * 16
    dw_out, ((landed_in_a,), (got_in_b,), (grad_up,)) = _matmul(
        y, dh1b, mode="tn", name="out_proj_dw", tm=1024, tn=1024, tk=4096, n_sub=2, out_dtypes=[BF16],
        stages=[_ChipExchange(part_in_a, None, 0, in_a), _PairExchange(dw_in_b), _HalfExchange(half_up)])
    dw_out = dw_out.reshape(N_CHIPS, -1, D)
    part_in_b, own_in_b = rs_adds(dw_in_b, got_in_b, "w_in_b")
    dxn, ((landed_in_a,), (landed_in_b,), (got_out,)) = _matmul(
        dproj, g_in, mode="nt", name="proj_dx", tm=2048, tn=1024, tk=1280, out_dtypes=[F32], b_shard="k",
        stages=[_ChipExchange(part_in_a, landed_in_a, in_a, rows_in - in_a), _ChipExchange(part_in_b),
                _PairExchange(dw_out)])
    part_out, own_out = rs_adds(dw_out, got_out, "w_out")
    half_in_a = _rs_final_add(own_in_a, landed_in_a, chip_core, "rs_final_add_w_in_a")
    half_in_b = _rs_final_add(own_in_b, landed_in_b, chip_core, "rs_final_add_w_in_b")
    (grad_x, _unused, d_mix_g), ((landed_out,), (grad_in_a,), (grad_in_b,)) = _rmsnorm_bwd(
        dxn, x2, mix_norm_g, dh1, "mix_norm_bwd",
        stages=[_ChipExchange(part_out), _HalfExchange(half_in_a), _HalfExchange(half_in_b)])
    grad_in = [grad_in_a, grad_in_b]
    half_out = _rs_final_add(own_out, landed_out, chip_core, "rs_final_add_w_out")
    ((grad_out,),) = _run_stages([_HalfExchange(half_out)], "rs_half_exchange_w_out")
    big = {"w_down": _adamw_sc(w_down[0], grad_down, m_w_down[0], v_w_down[0], "adamw_sc_w_down"),
           "w_up": _adamw_sc(w_up[0], grad_up, m_w_up[0], v_w_up[0], "adamw_sc_w_up"),
           "w_in": _adamw(w_in[0], grad_in, m_w_in[0], v_w_in[0], "adamw_w_in"),
           "w_out": _adamw(w_out[0], grad_out, m_w_out[0], v_w_out[0], "adamw_w_out")}
    big = {k: [t[None] for t in v] for k, v in big.items()}

    def small(conv, sw, sb, ga, gb, mlp, fin):
        return pack.pack([("spatial_w", sw.reshape(-1, W)), ("conv_w", conv), ("mlp_norm_g", mlp.reshape(-1, W)),
                          ("final_norm_g", fin.reshape(-1, W)), ("conv_out_norm_g", ga.reshape(-1, W)),
                          ("gmlp_out_norm_g", gb.reshape(-1, W)), ("spatial_b", _bias_rows(sb, W)),
                          ("loss", jnp.zeros((1, W), F32))])

    def full_conv(cw):
        return lax.dynamic_update_slice(jnp.zeros((CONV_K, W), F32), cw[0], (0, chip * Wl))

    def mix_rows(a):
        return _pad_rows(a.reshape(-1, W))

    w_s = small(full_conv(conv_w), spatial_w, spatial_b[0], conv_out_norm_g, gmlp_out_norm_g, mlp_norm_g, final_norm_g)
    m_s = small(full_conv(m_conv_w), m_spatial_w, m_spatial_b[0], m_conv_out_norm_g, m_gmlp_out_norm_g, m_mlp_norm_g,
                m_final_norm_g)
    v_s = small(full_conv(v_conv_w), v_spatial_w, v_spatial_b[0], v_conv_out_norm_g, v_gmlp_out_norm_g, v_mlp_norm_g,
                v_final_norm_g)
    small_outs = _sum_and_adamw_small(g_all, w_s, m_s, v_s, "sum_adamw_small")
    mix_all = _all_gather_small(mix_rows(d_mix_g), "all_gather_mix_norm_grad")
    mix_outs = _sum_and_adamw_small(mix_all, mix_rows(mix_norm_g), mix_rows(m_mix_norm_g), mix_rows(v_mix_norm_g),
                                    "sum_adamw_mix_norm")

    def unpack(kind, name):
        if name == "mix_norm_g":
            return mix_outs[kind].reshape(-1)[:D].reshape(1, D)
        rows = pack.piece(small_outs[kind], name)
        if name == "spatial_w":
            return rows.reshape(1, H, CHUNK, CHUNK)
        if name == "conv_w":
            return lax.dynamic_slice(rows[:CONV_K], (0, chip * Wl), (CONV_K, Wl))[None]
        if name == "spatial_b":
            return _bias_from_rows(rows, H)[None]
        if name == "final_norm_g":
            return rows.reshape(-1)[:D]
        n = D if name == "mlp_norm_g" else W
        return rows.reshape(-1)[:n].reshape(1, n)

    loss = pack.piece(small_outs[0], "loss")[0, 0]
    order = ["mix_norm_g", "w_in", "conv_w", "spatial_w", "spatial_b", "conv_out_norm_g", "gmlp_out_norm_g", "w_out",
             "mlp_norm_g", "w_up", "w_down", "final_norm_g"]
    outs = [loss, grad_x.reshape(Bl, S, D)]
    for kind in range(4):
        for name in order:
            outs.append(big[name][kind] if name in big else unpack(kind, name))
    return tuple(outs)
```
